```python
import math, functools
import jax, jax.numpy as jnp
from jax import lax
import numpy as np

D_MODEL = 2048
BATCH = 1
SEQ = 16384
DEPTH = 1
DEC_BATCH = 32
DEC_SEQ = 8
PAST_LEN = 16384
PAGE_SIZE = 128

ATT_WIDTH = D_MODEL // 2
GLA_WIDTH = D_MODEL - ATT_WIDTH
MIX_WIDTH = ATT_WIDTH + GLA_WIDTH
ATT_HEAD_DIM = 64
ATT_HEADS = ATT_WIDTH // ATT_HEAD_DIM
ROT_DIM = ATT_HEAD_DIM // 4
ROPE_THETA = 500000.0
DILATED_BRANCHES = ((128, 1), (512, 4), (2048, 16))
MAX_WINDOW = 2048
BAND_BLOCK = 128
GLA_HEADS = 4
GLA_DV = GLA_WIDTH // GLA_HEADS
GLA_DK = GLA_DV // 2
GLA_GATE_RANK = 16
GLA_TAU = 16.0
GLA_CHUNK = 64
D_FF = 4 * D_MODEL
EPS = 1e-6
IN_SPLITS = (ATT_WIDTH, ATT_WIDTH, ATT_WIDTH,
             GLA_HEADS * GLA_DK, GLA_HEADS * GLA_DK, GLA_WIDTH, GLA_WIDTH, GLA_GATE_RANK)
IN_COLS = sum(IN_SPLITS)

kernel_name = 'hymba_gla_dilated_attn_step'


def rmsnorm(x, g):
    x = x.astype(jnp.float32)
    return x * lax.rsqrt(jnp.mean(x * x, axis=-1, keepdims=True) + EPS) * g


def partial_rope(x, pos):
    half = ROT_DIM // 2
    inv_freq = ROPE_THETA ** (-jnp.arange(half, dtype=jnp.float32) / half)
    ang = pos.astype(jnp.float32)[:, None] * inv_freq[None, :]
    cos = jnp.cos(ang)[None, :, None, :]
    sin = jnp.sin(ang)[None, :, None, :]
    x1, x2, rest = x[..., :half], x[..., half:ROT_DIM], x[..., ROT_DIM:]
    return jnp.concatenate([x1 * cos - x2 * sin, x2 * cos + x1 * sin, rest], axis=-1)


def softmax_stats(sc, valid):
    sc = jnp.where(valid, sc, -jnp.inf)
    m = jnp.max(sc, axis=-1, keepdims=True)
    p = jnp.exp(sc - m)
    den = jnp.sum(p, axis=-1, keepdims=True)
    return m, den, p / den


def banded_attention(q, k, v, band):
    n, l, h, e = q.shape
    blk = BAND_BLOCK
    nb = -(-l // blk)
    lp = nb * blk
    qb = jnp.pad(q, ((0, 0), (0, lp - l), (0, 0), (0, 0))).reshape(n, nb, blk, h, e)
    kp = jnp.pad(k, ((0, 0), (blk, lp - l), (0, 0), (0, 0))).reshape(n, nb + 1, blk, h, e)
    vp = jnp.pad(v, ((0, 0), (blk, lp - l), (0, 0), (0, 0))).reshape(n, nb + 1, blk, h, e)
    kw = jnp.concatenate([kp[:, :-1], kp[:, 1:]], axis=2)
    vw = jnp.concatenate([vp[:, :-1], vp[:, 1:]], axis=2)
    sc = jnp.einsum('nbqhe,nbkhe->nbhqk', qb, kw)
    qi = jnp.arange(blk)[:, None]
    kj = jnp.arange(2 * blk)[None, :]
    dist = qi + blk - kj
    kpos = jnp.arange(nb)[:, None, None] * blk - blk + kj
    valid = (dist >= 0) & (dist <= band) & (kpos >= 0)
    m, den, p = softmax_stats(sc, valid[None, :, None])
    o = jnp.einsum('nbhqk,nbkhe->nbqhe', p, vw).reshape(n, lp, h, e)[:, :l]
    m = m.transpose(0, 1, 3, 2, 4).reshape(n, lp, h, 1)[:, :l]
    den = den.transpose(0, 1, 3, 2, 4).reshape(n, lp, h, 1)[:, :l]
    return m, den, o


def dilated_branch_full(q, k, v, window, dil):
    n, s, h, e = q.shape
    l = s // dil

    def to_sub(a):
        f = a.shape[-1]
        return a.reshape(n, l, dil, h, f).transpose(0, 2, 1, 3, 4).reshape(n * dil, l, h, f)

    def from_sub(a):
        f = a.shape[-1]
        return a.reshape(n, dil, l, h, f).transpose(0, 2, 1, 3, 4).reshape(n, s, h, f)

    m, den, o = banded_attention(to_sub(q), to_sub(k), to_sub(v), window // dil)
    return from_sub(m), from_sub(den), from_sub(o)


def dilated_branch_gather(q, k_all, v_all, q_idx, window, dil):
    j = jnp.arange(window // dil + 1)
    idx = q_idx[:, None] - j[None, :] * dil
    valid = idx >= 0
    idx = jnp.clip(idx, 0)
    kg = jnp.take(k_all, idx, axis=1)
    vg = jnp.take(v_all, idx, axis=1)
    sc = jnp.einsum('nthe,ntjhe->nthj', q, kg)
    m, den, p = softmax_stats(sc, valid[None, :, None, :])
    o = jnp.einsum('nthj,ntjhe->nthe', p, vg)
    return m, den, o


def combine_branches(stats):
    m = jnp.stack([st[0] for st in stats])
    den = jnp.stack([st[1] for st in stats])
    o = jnp.stack([st[2] for st in stats])
    w = den * jnp.exp(m - jnp.max(m, axis=0, keepdims=True))
    return jnp.sum(w * o, axis=0) / jnp.sum(w, axis=0)


def prompt_attention(qa, ka, va):
    return combine_branches([dilated_branch_full(qa, ka, va, w, d) for (w, d) in DILATED_BRANCHES])


def sample_attention(qa, ka, va, cache_k, cache_v):
    wb = cache_k.shape[1]
    k_all = jnp.concatenate([cache_k.astype(jnp.float32), ka], axis=1)
    v_all = jnp.concatenate([cache_v.astype(jnp.float32), va], axis=1)
    q_idx = wb + jnp.arange(qa.shape[1])
    return combine_branches([dilated_branch_gather(qa, k_all, v_all, q_idx, w, d)
                             for (w, d) in DILATED_BRANCHES])


def gla_chunked(q, k, v, log_a, s0, chunk):
    n, t, h, dk = q.shape
    nc = t // chunk

    def to_chunks(a):
        return a.reshape(n, nc, chunk, h, a.shape[-1]).transpose(1, 0, 3, 2, 4)

    causal = jnp.tril(jnp.ones((chunk, chunk), dtype=bool))[:, :, None]

    def step(state, inp):
        qc, kc, vc, ac = inp
        b = jnp.cumsum(ac, axis=2)
        o_inter = jnp.einsum('nhck,nhkv->nhcv', qc * jnp.exp(b), state)
        diff = b[:, :, :, None, :] - b[:, :, None, :, :]
        decay = jnp.exp(jnp.where(causal, diff, -jnp.inf))
        att = jnp.einsum('nhtsk,nhtk,nhsk->nhts', decay, qc, kc)
        o = o_inter + jnp.einsum('nhts,nhsv->nhtv', att, vc)
        b_last = b[:, :, -1:, :]
        state = (jnp.exp(b_last[:, :, 0, :])[..., None] * state
                 + jnp.einsum('nhsk,nhsv->nhkv', kc * jnp.exp(b_last - b), vc))
        return state, o

    s_fin, o = lax.scan(step, s0.astype(jnp.float32),
                        (to_chunks(q), to_chunks(k), to_chunks(v), to_chunks(log_a)))
    o = o.transpose(1, 0, 3, 2, 4).reshape(n, t, h, v.shape[-1])
    return o, s_fin


def mixer_inputs(xn, w_in, gla_gate_w, gla_gate_b, pos):
    n, t, _ = xn.shape
    proj = (xn @ w_in).astype(jnp.float32)
    parts = []
    start = 0
    for size in IN_SPLITS:
        parts.append(proj[..., start:start + size])
        start += size
    qa, ka, va, qg, kg, vg, gg, ag = parts
    qa = partial_rope(qa.reshape(n, t, ATT_HEADS, ATT_HEAD_DIM), pos) * ATT_HEAD_DIM ** -0.5
    ka = partial_rope(ka.reshape(n, t, ATT_HEADS, ATT_HEAD_DIM), pos)
    va = va.reshape(n, t, ATT_HEADS, ATT_HEAD_DIM)
    qg = qg.reshape(n, t, GLA_HEADS, GLA_DK) * GLA_DK ** -0.5
    kg = kg.reshape(n, t, GLA_HEADS, GLA_DK)
    vg = vg.reshape(n, t, GLA_HEADS, GLA_DV)
    log_a = (jax.nn.log_sigmoid(ag @ gla_gate_w + gla_gate_b) / GLA_TAU).reshape(n, t, GLA_HEADS, GLA_DK)
    return qa, ka, va, qg, kg, vg, gg, log_a


def decoder_layer(x, c, pos, attend, gla_s0, gla_chunk, w_in, gla_gate_w, gla_gate_b,
                  att_norm_g, gla_norm_g, w_out, norm1_g, norm2_g, w_ada, b_ada, w_up, w_down):
    n, t, _ = x.shape
    ada = jax.nn.silu(c.astype(jnp.float32)) @ w_ada + b_ada
    sh1, sc1, g1, sh2, sc2, g2 = jnp.split(ada, 6, axis=-1)
    xn = rmsnorm(x, norm1_g) * (1.0 + sc1[:, None]) + sh1[:, None]
    qa, ka, va, qg, kg, vg, gg, log_a = mixer_inputs(xn, w_in, gla_gate_w, gla_gate_b, pos)
    att_o = attend(qa, ka, va)
    gla_o, gla_state = gla_chunked(qg, kg, vg, log_a, gla_s0, gla_chunk)
    att_y = rmsnorm(att_o.reshape(n, t, ATT_WIDTH), att_norm_g)
    gla_y = rmsnorm(gla_o, gla_norm_g.reshape(GLA_HEADS, GLA_DV)).reshape(n, t, GLA_WIDTH) * jax.nn.silu(gg)
    mix = jnp.concatenate([att_y, gla_y], axis=-1) @ w_out
    h = x + g1[:, None] * mix
    hn = rmsnorm(h, norm2_g) * (1.0 + sc2[:, None]) + sh2[:, None]
    h = h + g2[:, None] * (jnp.square(jax.nn.relu(hn @ w_up)) @ w_down)
    return h, ka, va, gla_state


def setup_inputs(seed: int = 0) -> dict:
    key = jax.random.key(seed)
    ks = jax.random.split(key, 24)

    def nrm(k, shape, scale):
        return scale * jax.random.normal(k, shape, jnp.float32)

    wb = min(MAX_WINDOW, PAST_LEN)
    return {
        'x_prompt': nrm(ks[0], (BATCH, SEQ, D_MODEL), 1.0),
        'x_sample': nrm(ks[1], (DEC_BATCH, DEC_SEQ, D_MODEL), 1.0),
        'c_prompt': nrm(ks[2], (BATCH, D_MODEL), 1.0),
        'c_sample': nrm(ks[3], (DEC_BATCH, D_MODEL), 1.0),
        'cache_win_k': nrm(ks[4], (DEPTH, DEC_BATCH, wb, ATT_HEADS, ATT_HEAD_DIM), 1.0),
        'cache_win_v': nrm(ks[5], (DEPTH, DEC_BATCH, wb, ATT_HEADS, ATT_HEAD_DIM), 1.0),
        'state_gla': nrm(ks[6], (DEPTH, DEC_BATCH, GLA_HEADS, GLA_DK, GLA_DV), 0.5),
        'w_in': nrm(ks[7], (DEPTH, D_MODEL, IN_COLS), D_MODEL ** -0.5),
        'gla_gate_w': nrm(ks[8], (DEPTH, GLA_GATE_RANK, GLA_HEADS * GLA_DK), GLA_GATE_RANK ** -0.5),
        'gla_gate_b': 2.0 + nrm(ks[9], (DEPTH, GLA_HEADS * GLA_DK), 0.1),
        'att_norm_g': 1.0 + nrm(ks[10], (DEPTH, ATT_WIDTH), 0.1),
        'gla_norm_g': 1.0 + nrm(ks[11], (DEPTH, GLA_WIDTH), 0.1),
        'w_out': nrm(ks[12], (DEPTH, MIX_WIDTH, D_MODEL), MIX_WIDTH ** -0.5),
        'norm1_g': 1.0 + nrm(ks[13], (DEPTH, D_MODEL), 0.1),
        'norm2_g': 1.0 + nrm(ks[14], (DEPTH, D_MODEL), 0.1),
        'w_ada': nrm(ks[15], (DEPTH, D_MODEL, 6 * D_MODEL), D_MODEL ** -0.5),
        'b_ada': nrm(ks[16], (DEPTH, 6 * D_MODEL), 0.1),
        'w_up': nrm(ks[17], (DEPTH, D_MODEL, D_FF), D_MODEL ** -0.5),
        'w_down': nrm(ks[18], (DEPTH, D_FF, D_MODEL), D_FF ** -0.5),
        'final_g': 1.0 + nrm(ks[19], (D_MODEL,), 0.1),
    }


def reference(x_prompt, x_sample, c_prompt, c_sample, cache_win_k, cache_win_v, state_gla,
              w_in, gla_gate_w, gla_gate_b, att_norm_g, gla_norm_g, w_out, norm1_g, norm2_g,
              w_ada, b_ada, w_up, w_down, final_g):
    n_p, t_p, _ = x_prompt.shape
    n_s, t_s, _ = x_sample.shape
    pos_p = jnp.arange(t_p)
    pos_s = PAST_LEN + jnp.arange(t_s)
    wb_p = min(MAX_WINDOW, t_p)
    gla_s0_p = jnp.zeros((n_p, GLA_HEADS, GLA_DK, GLA_DV), jnp.float32)
    hp, hs = x_prompt, x_sample
    pk, pv, ps, sk, sv, ss = [], [], [], [], [], []
    for l in range(DEPTH):
        w = (w_in[l], gla_gate_w[l], gla_gate_b[l], att_norm_g[l], gla_norm_g[l], w_out[l],
             norm1_g[l], norm2_g[l], w_ada[l], b_ada[l], w_up[l], w_down[l])
        hp, ka_p, va_p, st_p = decoder_layer(hp, c_prompt, pos_p, prompt_attention,
                                             gla_s0_p, GLA_CHUNK, *w)
        attend_s = functools.partial(sample_attention, cache_k=cache_win_k[l], cache_v=cache_win_v[l])
        hs, ka_s, va_s, st_s = decoder_layer(hs, c_sample, pos_s, attend_s,
                                             state_gla[l], t_s, *w)
        pk.append(ka_p[:, t_p - wb_p:])
        pv.append(va_p[:, t_p - wb_p:])
        ps.append(st_p)
        sk.append(ka_s)
        sv.append(va_s)
        ss.append(st_s)
    y_prompt = rmsnorm(hp, final_g).astype(x_prompt.dtype)
    y_sample = rmsnorm(hs, final_g).astype(x_sample.dtype)
    prompt_win_k = jnp.stack(pk)
    prompt_win_v = jnp.stack(pv)
    prompt_gla_state = jnp.stack(ps)
    sample_new_k = jnp.stack(sk)
    sample_new_v = jnp.stack(sv)
    sample_gla_state = jnp.stack(ss)
    return (y_prompt, y_sample, prompt_win_k, prompt_win_v, prompt_gla_state,
            sample_new_k, sample_new_v, sample_gla_state)
```

```python
import functools
import math

import jax
import jax.numpy as jnp
from jax import lax
from jax.experimental import pallas as pl
from jax.experimental.pallas import tpu as pltpu

F32 = jnp.float32
BF16 = jnp.bfloat16

D_MODEL = 2048
ATT_WIDTH = 1024
ATT_HEADS = 16
ATT_HEAD_DIM = 64
ROT_DIM = 16
ROPE_THETA = 500000.0
DILATIONS = (1, 4, 16)
BAND = 128
MAX_WINDOW = 2048
GLA_HEADS = 4
GLA_DK = 128
GLA_DV = 256
GLA_WIDTH = 1024
GLA_GATE_RANK = 16
GLA_TAU = 16.0
D_FF = 8192
EPS = 1e-6
PAST_LEN = 16384
NEG = -1e30

LANES = 128
HEAD_PAIRS = ATT_WIDTH // LANES
VMEM_LIMIT = 56 * 1024 * 1024


def _cparams(sem):
    return pltpu.CompilerParams(dimension_semantics=sem, vmem_limit_bytes=VMEM_LIMIT)


def _dot(a, b):
    return jnp.dot(a, b, preferred_element_type=F32)


def _dot_nt(a, b):
    return lax.dot_general(a, b, (((1,), (1,)), ((), ())), preferred_element_type=F32)


def _dot_tn(a, b):
    return lax.dot_general(a, b, (((0,), (0,)), ((), ())), preferred_element_type=F32)


def _silu(x):
    return x / (1.0 + jnp.exp(-x))


def _ada_kernel(c_ref, w_ref, b_ref, o_ref):
    s = _silu(c_ref[...]).astype(BF16)
    o_ref[...] = _dot(s, w_ref[...].astype(BF16)) + b_ref[...]


def _ada(c, w_ada, b_ada):
    rows = c.shape[0]
    n = w_ada.shape[1]
    tn = 1024
    return pl.pallas_call(
        _ada_kernel,
        out_shape=jax.ShapeDtypeStruct((rows, n), F32),
        grid=(n // tn,),
        in_specs=[
            pl.BlockSpec((rows, D_MODEL), lambda j: (0, 0)),
            pl.BlockSpec((D_MODEL, tn), lambda j: (0, j)),
            pl.BlockSpec((1, tn), lambda j: (0, j)),
        ],
        out_specs=pl.BlockSpec((rows, tn), lambda j: (0, j)),
        compiler_params=_cparams(("arbitrary",)),
        name="ada",
    )(c, w_ada, b_ada)


def _inproj_kernel(x_ref, g_ref, sc_ref, sh_ref, w_ref, wag_ref, gw_ref, gb_ref,
                   cs_ref, sa_ref, sb_ref,
                   qa_ref, ka_ref, va_ref, gla_ref, la_ref, tk_ref, tv_ref, xn_ref,
                   *, first_tail):
    i = pl.program_id(0)
    j = pl.program_id(1)

    @pl.when(j == 0)
    def _():
        x = x_ref[...]
        rs = lax.rsqrt(jnp.mean(x * x, axis=-1, keepdims=True) + EPS)
        xn = x * rs * g_ref[...] * (1.0 + sc_ref[...]) + sh_ref[...]
        xnb = xn.astype(BF16)
        xn_ref[...] = xnb
        ag = _dot(xnb, wag_ref[...])
        z = _dot(ag.astype(BF16), gw_ref[...]) + gb_ref[...]
        log_sig = jnp.minimum(z, 0.0) - jnp.log(1.0 + jnp.exp(-jnp.abs(z)))
        la_ref[...] = log_sig / GLA_TAU

    acc = _dot(xn_ref[...], w_ref[...])

    def rope(slab):
        return (slab * cs_ref[...] + pltpu.roll(slab, LANES - 8, 1) * sa_ref[...]
                + pltpu.roll(slab, 8, 1) * sb_ref[...])

    @pl.when(j == 0)
    def _():
        for hp in range(HEAD_PAIRS):
            slab = rope(acc[:, LANES * hp:LANES * (hp + 1)]) * (ATT_HEAD_DIM ** -0.5)
            qa_ref[hp] = slab.astype(BF16)

    @pl.when(j == 1)
    def _():
        slabs = [rope(acc[:, LANES * hp:LANES * (hp + 1)]) for hp in range(HEAD_PAIRS)]
        for hp in range(HEAD_PAIRS):
            ka_ref[hp] = slabs[hp].astype(BF16)

        @pl.when(i >= first_tail)
        def _():
            for hp in range(HEAD_PAIRS):
                tk_ref[:, LANES * hp:LANES * (hp + 1)] = slabs[hp]

    @pl.when(j == 2)
    def _():
        for hp in range(HEAD_PAIRS):
            va_ref[hp] = acc[:, LANES * hp:LANES * (hp + 1)].astype(BF16)

        @pl.when(i >= first_tail)
        def _():
            tv_ref[...] = acc

    @pl.when(j == 3)
    def _():
        half = GLA_HEADS * GLA_DK
        gla_ref[:, :half] = (acc[:, :half] * (GLA_DK ** -0.5)).astype(BF16)
        gla_ref[:, half:] = acc[:, half:].astype(BF16)

    @pl.when(j >= 4)
    def _():
        gla_ref[...] = acc.astype(BF16)


def _inproj(x, g, sc, sh, w_main, w_ag, gate_w, gate_b, cs, sa, sb, *, tm, tail_rows):
    t = x.shape[0]
    nj = 6
    mod_rows = sc.shape[0]
    first_tail = (t - tail_rows) // tm
    mod_map = (lambda i, j: (0, 0)) if mod_rows == 1 else (lambda i, j: (i, 0))
    mod_block = (1, D_MODEL) if mod_rows == 1 else (tm, D_MODEL)
    hp_spec = pl.BlockSpec((HEAD_PAIRS, tm, LANES), lambda i, j: (0, i, 0))
    tail_spec = pl.BlockSpec((tm, ATT_WIDTH), lambda i, j: (jnp.maximum(i - first_tail, 0), 0))
    return pl.pallas_call(
        functools.partial(_inproj_kernel, first_tail=first_tail),
        out_shape=(
            jax.ShapeDtypeStruct((HEAD_PAIRS, t, LANES), BF16),
            jax.ShapeDtypeStruct((HEAD_PAIRS, t, LANES), BF16),
            jax.ShapeDtypeStruct((HEAD_PAIRS, t, LANES), BF16),
            jax.ShapeDtypeStruct((t, 3 * 1024), BF16),
            jax.ShapeDtypeStruct((t, GLA_HEADS * GLA_DK), F32),
            jax.ShapeDtypeStruct((tail_rows, ATT_WIDTH), F32),
            jax.ShapeDtypeStruct((tail_rows, ATT_WIDTH), F32),
        ),
        grid=(t // tm, nj),
        in_specs=[
            pl.BlockSpec((tm, D_MODEL), lambda i, j: (i, 0)),
            pl.BlockSpec((1, D_MODEL), lambda i, j: (0, 0)),
            pl.BlockSpec(mod_block, mod_map),
            pl.BlockSpec(mod_block, mod_map),
            pl.BlockSpec((D_MODEL, 1024), lambda i, j: (0, j)),
            pl.BlockSpec((D_MODEL, LANES), lambda i, j: (0, 0)),
            pl.BlockSpec((LANES, GLA_HEADS * GLA_DK), lambda i, j: (0, 0)),
            pl.BlockSpec((1, GLA_HEADS * GLA_DK), lambda i, j: (0, 0)),
            pl.BlockSpec((tm, LANES), lambda i, j: (i, 0)),
            pl.BlockSpec((tm, LANES), lambda i, j: (i, 0)),
            pl.BlockSpec((tm, LANES), lambda i, j: (i, 0)),
        ],
        out_specs=(
            hp_spec, hp_spec, hp_spec,
            pl.BlockSpec((tm, 1024), lambda i, j: (i, jnp.maximum(j - 3, 0))),
            pl.BlockSpec((tm, GLA_HEADS * GLA_DK), lambda i, j: (i, 0)),
            tail_spec, tail_spec,
        ),
        scratch_shapes=[pltpu.VMEM((tm, D_MODEL), BF16)],
        compiler_params=_cparams(("arbitrary", "arbitrary")),
        name="inproj",
    )(x, g, sc, sh, w_main, w_ag, gate_w, gate_b, cs, sa, sb)


def _rope_tables(pos):
    half = ROT_DIM // 2
    inv_freq = ROPE_THETA ** (-jnp.arange(half, dtype=F32) / half)
    ang = pos.astype(F32)[:, None] * inv_freq[None, :]
    cos, sin = jnp.cos(ang), jnp.sin(ang)
    t = pos.shape[0]
    rest = ATT_HEAD_DIM - ROT_DIM
    ones = jnp.ones((t, rest), F32)
    zeros = jnp.zeros((t, rest), F32)
    zh = jnp.zeros((t, half), F32)
    cs = jnp.concatenate([cos, cos, ones], axis=1)
    sa = jnp.concatenate([-sin, zh, zeros], axis=1)
    sb = jnp.concatenate([zh, sin, zeros], axis=1)
    return tuple(jnp.concatenate([a, a], axis=1) for a in (cs, sa, sb))


def _band_valid(prev_ok):
    qi = lax.broadcasted_iota(jnp.int32, (2 * BAND, 2 * BAND), 0) % BAND
    kk = lax.broadcasted_iota(jnp.int32, (2 * BAND, 2 * BAND), 1)
    lo = qi if prev_ok is True else qi + jnp.where(prev_ok, 0, 2 * BAND)
    return jnp.where(kk < BAND, kk - lo, qi - (kk - BAND)) >= 0


def _band_tile(q, kw, vw, valid):
    lane = lax.broadcasted_iota(jnp.int32, (1, LANES), 1)
    first = lane < ATT_HEAD_DIM
    head0 = jnp.where(first, 1.0, 0.0).astype(BF16)
    head1 = jnp.where(first, 0.0, 1.0).astype(BF16)
    q2 = jnp.concatenate([q * head0, q * head1], axis=0)
    s = jnp.where(valid, _dot_nt(q2, kw), NEG)
    m = jnp.max(s, axis=-1, keepdims=True)
    p = jnp.exp(s - m)
    l = jnp.sum(p, axis=-1, keepdims=True)
    acc = _dot(p.astype(BF16), vw)
    pick = lambda a: jnp.where(first, a[:BAND], a[BAND:])
    return pick(acc), pick(m), pick(l)


def _attn_kernel(q1, k1c, v1c, k1p, v1p, q4, k4c, v4c, k4p, v4p, q16, k16c, v16c, k16p, v16p,
                 o_ref, a1, m1, l1, a4, m4, l4):
    s = pl.program_id(1)
    valid_first = _band_valid(s > 0)
    valid_full = _band_valid(True)
    n1 = q1.shape[1] // BAND
    n4 = q4.shape[1] // BAND

    def store(refs, row, res):
        for ref, val in zip(refs, res):
            ref[pl.ds(row, BAND), :] = val

    store((a1, m1, l1), 0, _band_tile(
        q1[0, :BAND, :],
        jnp.concatenate([k1p[0], k1c[0, :BAND, :]], axis=0),
        jnp.concatenate([v1p[0], v1c[0, :BAND, :]], axis=0), valid_first))

    def body1(i, carry):
        row = pl.multiple_of(i * BAND, BAND)
        prev = pl.multiple_of(row - BAND, BAND)
        store((a1, m1, l1), row, _band_tile(
            q1[0, pl.ds(row, BAND), :], k1c[0, pl.ds(prev, 2 * BAND), :],
            v1c[0, pl.ds(prev, 2 * BAND), :], valid_full))
        return carry

    lax.fori_loop(1, n1, body1, 0)

    rows4 = q4.shape[1]
    for r in range(4):
        cols = slice(LANES * r, LANES * (r + 1))
        store((a4, m4, l4), r * rows4, _band_tile(
            q4[0, :BAND, cols],
            jnp.concatenate([k4p[0, :, cols], k4c[0, :BAND, cols]], axis=0),
            jnp.concatenate([v4p[0, :, cols], v4c[0, :BAND, cols]], axis=0), valid_first))

        def body4(i, carry, r=r, cols=cols):
            row = pl.multiple_of(i * BAND, BAND)
            prev = pl.multiple_of(row - BAND, BAND)
            store((a4, m4, l4), r * rows4 + row, _band_tile(
                q4[0, pl.ds(row, BAND), cols], k4c[0, pl.ds(prev, 2 * BAND), cols],
                v4c[0, pl.ds(prev, 2 * BAND), cols], valid_full))
            return carry

        lax.fori_loop(1, n4, body4, 0)

    for r in range(16):
        cols = slice(LANES * r, LANES * (r + 1))
        a16, m16, l16 = _band_tile(
            q16[0, :, cols],
            jnp.concatenate([k16p[0, :, cols], k16c[0, :, cols]], axis=0),
            jnp.concatenate([v16p[0, :, cols], v16c[0, :, cols]], axis=0), valid_first)
        sl1 = pl.ds(r, BAND, stride=16)
        sl4 = pl.ds((r % 4) * rows4 + r // 4, BAND, stride=4)
        ma, mb = m1[sl1, :], m4[sl4, :]
        m = jnp.maximum(jnp.maximum(ma, mb), m16)
        wa, wb, wc = jnp.exp(ma - m), jnp.exp(mb - m), jnp.exp(m16 - m)
        num = wa * a1[sl1, :] + wb * a4[sl4, :] + wc * a16
        den = wa * l1[sl1, :] + wb * l4[sl4, :] + wc * l16
        o_ref[0, :, cols] = (num / den).astype(o_ref.dtype)


def _prompt_attention(qa, ka, va):
    hp, t, _ = qa.shape
    sb = 16 * BAND
    ns = t // sb
    views = {d: [a.reshape(hp, t // d, d * LANES) for a in (qa, ka, va)] for d in DILATIONS}
    in_specs, args = [], []
    for d in DILATIONS:
        rows, width = sb // d, d * LANES
        cur = pl.BlockSpec((1, rows, width), lambda h, s: (h, s, 0))
        nb = rows // BAND
        prev = pl.BlockSpec((1, BAND, width),
                            lambda h, s, nb=nb: (h, jnp.maximum(s * nb - 1, 0), 0))
        q, k, v = views[d]
        in_specs += [cur, cur, cur, prev, prev]
        args += [q, k, v, k, v]
    out = pl.pallas_call(
        _attn_kernel,
        out_shape=jax.ShapeDtypeStruct((hp, t // 16, 16 * LANES), BF16),
        grid=(hp, ns),
        in_specs=in_specs,
        out_specs=pl.BlockSpec((1, BAND, 16 * LANES), lambda h, s: (h, s, 0)),
        scratch_shapes=[pltpu.VMEM((sb, LANES), F32) for _ in range(6)],
        compiler_params=_cparams(("arbitrary", "arbitrary")),
        name="prompt_attn",
    )(*args)
    return out.reshape(hp, t, LANES)


def _gla_chunk(q, k, v, la, st_ref, h, tri, sub):
    c = q.shape[0]
    la_hi = la.astype(BF16)
    la_lo = (la - la_hi.astype(F32)).astype(BF16)
    b = _dot(tri, la_hi) + _dot(tri, la_lo)
    st = st_ref[h]
    o = _dot_nt((q * jnp.exp(b)).astype(BF16), st.astype(BF16))
    srow = lax.broadcasted_iota(jnp.int32, (c, GLA_DK), 0)
    att_rows = []
    for i in range(c // sub):
        lo, hi = i * sub, (i + 1) * sub
        ref = b[lo:lo + 1, :]
        qh = (q[lo:hi] * jnp.exp(b[lo:hi] - ref)).astype(BF16)
        kh = (k * jnp.exp(jnp.where(srow < hi, ref - b, 0.0))).astype(BF16)
        a = _dot_nt(qh, kh)
        ti = lax.broadcasted_iota(jnp.int32, (sub, c), 0) + lo
        si = lax.broadcasted_iota(jnp.int32, (sub, c), 1)
        att_rows.append(jnp.where(si <= ti, a, 0.0))
    att = att_rows[0] if len(att_rows) == 1 else jnp.concatenate(att_rows, axis=0)
    o = o + _dot(att.astype(BF16), v)
    b_last = b[c - 1:c, :]
    kbar = (k * jnp.exp(b_last - b)).astype(BF16)
    st_ref[h] = st * jnp.exp(b_last) + _dot_tn(v, kbar)
    return o


def _gla_out(o, g, gg):
    rs = lax.rsqrt(jnp.mean(o * o, axis=-1, keepdims=True) + EPS)
    return o * rs * g * _silu(gg)


def _tri(c):
    r = lax.broadcasted_iota(jnp.int32, (c, c), 0)
    s = lax.broadcasted_iota(jnp.int32, (c, c), 1)
    return jnp.where(s <= r, 1.0, 0.0).astype(BF16)


def _gla_prompt_kernel(qk_ref, v_ref, gg_ref, la_ref, g_ref, y_ref, s_ref, st_ref, *, chunk, sub):
    i = pl.program_id(0)

    @pl.when(i == 0)
    def _():
        st_ref[...] = jnp.zeros_like(st_ref)

    tri = _tri(chunk)
    nchunks = qk_ref.shape[0] // chunk
    half = GLA_HEADS * GLA_DK

    def body(ci, carry):
        rows = pl.ds(pl.multiple_of(ci * chunk, chunk), chunk)
        for h in range(GLA_HEADS):
            kc = slice(GLA_DK * h, GLA_DK * (h + 1))
            vc = slice(GLA_DV * h, GLA_DV * (h + 1))
            q = qk_ref[rows, kc].astype(F32)
            k = qk_ref[rows, half + GLA_DK * h:half + GLA_DK * (h + 1)].astype(F32)
            o = _gla_chunk(q, k, v_ref[rows, vc], la_ref[rows, kc], st_ref, h, tri, sub)
            y_ref[rows, vc] = _gla_out(o, g_ref[:, vc], gg_ref[rows, vc].astype(F32)).astype(y_ref.dtype)
        return carry

    lax.fori_loop(0, nchunks, body, 0)

    @pl.when(i == pl.num_programs(0) - 1)
    def _():
        for h in range(GLA_HEADS):
            s_ref[h] = st_ref[h].T


def _gla_prompt(gla_in, log_a, norm_g, *, tb, chunk, sub):
    t = gla_in.shape[0]
    return pl.pallas_call(
        functools.partial(_gla_prompt_kernel, chunk=chunk, sub=sub),
        out_shape=(
            jax.ShapeDtypeStruct((t, GLA_WIDTH), BF16),
            jax.ShapeDtypeStruct((GLA_HEADS, GLA_DK, GLA_DV), F32),
        ),
        grid=(t // tb,),
        in_specs=[
            pl.BlockSpec((tb, 1024), lambda i: (i, 0)),
            pl.BlockSpec((tb, 1024), lambda i: (i, 1)),
            pl.BlockSpec((tb, 1024), lambda i: (i, 2)),
            pl.BlockSpec((tb, GLA_HEADS * GLA_DK), lambda i: (i, 0)),
            pl.BlockSpec((1, GLA_WIDTH), lambda i: (0, 0)),
        ],
        out_specs=(
            pl.BlockSpec((tb, GLA_WIDTH), lambda i: (i, 0)),
            pl.BlockSpec((GLA_HEADS, GLA_DK, GLA_DV), lambda i: (0, 0, 0)),
        ),
        scratch_shapes=[pltpu.VMEM((GLA_HEADS, GLA_DV, GLA_DK), F32)],
        compiler_params=_cparams(("arbitrary",)),
        name="gla_prompt",
    )(gla_in, gla_in, gla_in, log_a, norm_g)


def _gla_sample_kernel(qk_ref, v_ref, gg_ref, la_ref, g_ref, s0_ref, y_ref, s_ref, st_ref):
    steps = qk_ref.shape[0]
    c = 2 * steps
    tri = _tri(c)
    half = GLA_HEADS * GLA_DK
    pad = lambda a: jnp.concatenate([a, jnp.zeros_like(a)], axis=0)
    for h in range(GLA_HEADS):
        st_ref[h] = s0_ref[0, h].T
    for h in range(GLA_HEADS):
        kc = slice(GLA_DK * h, GLA_DK * (h + 1))
        vc = slice(GLA_DV * h, GLA_DV * (h + 1))
        q = pad(qk_ref[:, kc])
        k = pad(qk_ref[:, half + GLA_DK * h:half + GLA_DK * (h + 1)])
        v = pad(v_ref[:, vc]).astype(BF16)
        o = _gla_chunk(q, k, v, pad(la_ref[:, kc]), st_ref, h, tri, c)
        y_ref[:, vc] = _gla_out(o[:steps], g_ref[:, vc], gg_ref[:, vc])
    for h in range(GLA_HEADS):
        s_ref[0, h] = st_ref[h].T


def _gla_sample(gla_in, log_a, norm_g, state, *, steps):
    t = gla_in.shape[0]
    nb = t // steps
    return pl.pallas_call(
        _gla_sample_kernel,
        out_shape=(
            jax.ShapeDtypeStruct((t, GLA_WIDTH), F32),
            jax.ShapeDtypeStruct(state.shape, F32),
        ),
        grid=(nb,),
        in_specs=[
            pl.BlockSpec((steps, 1024), lambda n: (n, 0)),
            pl.BlockSpec((steps, 1024), lambda n: (n, 1)),
            pl.BlockSpec((steps, 1024), lambda n: (n, 2)),
            pl.BlockSpec((steps, GLA_HEADS * GLA_DK), lambda n: (n, 0)),
            pl.BlockSpec((1, GLA_WIDTH), lambda n: (0, 0)),
            pl.BlockSpec((1, GLA_HEADS, GLA_DK, GLA_DV), lambda n: (n, 0, 0, 0)),
        ],
        out_specs=(
            pl.BlockSpec((steps, GLA_WIDTH), lambda n: (n, 0)),
            pl.BlockSpec((1, GLA_HEADS, GLA_DK, GLA_DV), lambda n: (n, 0, 0, 0)),
        ),
        scratch_shapes=[pltpu.VMEM((GLA_HEADS, GLA_DV, GLA_DK), F32)],
        compiler_params=_cparams(("arbitrary",)),
        name="gla_sample",
    )(gla_in, gla_in, gla_in, log_a, norm_g, state)


def _sample_attn_kernel(q_ref, kn_ref, vn_ref, ck_ref, cv_ref, mc_ref, mn_ref, o_ref):
    steps = q_ref.shape[0]
    rows = ATT_HEADS * steps
    qt = jnp.concatenate([q_ref[...]] * ATT_HEADS, axis=0)
    rh = lax.broadcasted_iota(jnp.int32, (rows, ATT_WIDTH), 0) // steps
    ch = lax.broadcasted_iota(jnp.int32, (rows, ATT_WIDTH), 1) // ATT_HEAD_DIM
    own = rh == ch
    qb = jnp.where(own, qt, 0.0).astype(BF16)
    pad = jnp.zeros((LANES - steps, ATT_WIDTH), F32)
    kn = jnp.concatenate([kn_ref[...], pad], axis=0).astype(BF16)
    vn = jnp.concatenate([vn_ref[...], pad], axis=0).astype(BF16)
    sc = _dot_nt(qb, ck_ref[0].astype(BF16))
    sn = _dot_nt(qb, kn)
    mc, mn = mc_ref[...], mn_ref[...]
    sc = jnp.where(mc > 0.0, sc, NEG)
    sn = jnp.where(mn > 0.0, sn, NEG)
    m = jnp.maximum(jnp.max(sc, axis=-1, keepdims=True), jnp.max(sn, axis=-1, keepdims=True))
    pc = mc * jnp.exp(sc - m)
    pn = mn * jnp.exp(sn - m)
    den = jnp.sum(pc, axis=-1, keepdims=True) + jnp.sum(pn, axis=-1, keepdims=True)
    o = _dot(pc.astype(BF16), cv_ref[0].astype(BF16)) + _dot(pn.astype(BF16), vn)
    o = jnp.where(own, o / den, 0.0)
    o_ref[...] = jnp.sum(o.reshape(ATT_HEADS, steps, ATT_WIDTH), axis=0)


def _sample_multiplicity(steps, cache_len):
    t = jnp.arange(steps)[:, None]
    row_c = jnp.arange(cache_len)[None, :]
    row_n = cache_len + jnp.arange(LANES)[None, :]

    def mult(rows, in_range):
        delta = cache_len + t - rows
        total = jnp.zeros(delta.shape, F32)
        for d in DILATIONS:
            hit = (delta >= 0) & (delta <= BAND * d) & (delta % d == 0)
            total = total + hit.astype(F32)
        return jnp.where(in_range, total, 0.0)

    mc = mult(row_c, True)
    mn = mult(row_n, jnp.arange(LANES)[None, :] < steps)
    tile = lambda a: jnp.tile(a, (ATT_HEADS, 1))
    return tile(mc), tile(mn)


def _sample_attention(qa, k_new, v_new, cache_k, cache_v, *, steps):
    nb, cache_len, _ = cache_k.shape
    mc, mn = _sample_multiplicity(steps, cache_len)
    rows = ATT_HEADS * steps
    return pl.pallas_call(
        _sample_attn_kernel,
        out_shape=jax.ShapeDtypeStruct((nb * steps, ATT_WIDTH), F32),
        grid=(nb,),
        in_specs=[
            pl.BlockSpec((steps, ATT_WIDTH), lambda n: (n, 0)),
            pl.BlockSpec((steps, ATT_WIDTH), lambda n: (n, 0)),
            pl.BlockSpec((steps, ATT_WIDTH), lambda n: (n, 0)),
            pl.BlockSpec((1, cache_len, ATT_WIDTH), lambda n: (n, 0, 0)),
            pl.BlockSpec((1, cache_len, ATT_WIDTH), lambda n: (n, 0, 0)),
            pl.BlockSpec((rows, cache_len), lambda n: (0, 0)),
            pl.BlockSpec((rows, LANES), lambda n: (0, 0)),
        ],
        out_specs=pl.BlockSpec((steps, ATT_WIDTH), lambda n: (n, 0)),
        compiler_params=_cparams(("arbitrary",)),
        name="sample_attn",
    )(qa, k_new, v_new, cache_k, cache_v, mc, mn)


def _outproj_kernel(att_ref, gla_ref, x_ref, gate_ref, ng_ref, w_ref, h_ref):
    att = jnp.concatenate([att_ref[hp].astype(F32) for hp in range(HEAD_PAIRS)], axis=1)
    rs = lax.rsqrt(jnp.mean(att * att, axis=-1, keepdims=True) + EPS)
    att_y = (att * rs * ng_ref[...]).astype(BF16)
    mix = _dot(att_y, w_ref[:ATT_WIDTH, :]) + _dot(gla_ref[...], w_ref[ATT_WIDTH:, :])
    h_ref[...] = x_ref[...] + gate_ref[...] * mix


def _outproj(att, gla_y, x, gate, att_norm_g, w_out, *, tm):
    t = x.shape[0]
    mod_rows = gate.shape[0]
    mod_map = (lambda i: (0, 0)) if mod_rows == 1 else (lambda i: (i, 0))
    mod_block = (1, D_MODEL) if mod_rows == 1 else (tm, D_MODEL)
    return pl.pallas_call(
        _outproj_kernel,
        out_shape=jax.ShapeDtypeStruct((t, D_MODEL), F32),
        grid=(t // tm,),
        in_specs=[
            pl.BlockSpec((HEAD_PAIRS, tm, LANES), lambda i: (0, i, 0)),
            pl.BlockSpec((tm, GLA_WIDTH), lambda i: (i, 0)),
            pl.BlockSpec((tm, D_MODEL), lambda i: (i, 0)),
            pl.BlockSpec(mod_block, mod_map),
            pl.BlockSpec((1, ATT_WIDTH), lambda i: (0, 0)),
            pl.BlockSpec((ATT_WIDTH + GLA_WIDTH, D_MODEL), lambda i: (0, 0)),
        ],
        out_specs=pl.BlockSpec((tm, D_MODEL), lambda i: (i, 0)),
        compiler_params=_cparams(("arbitrary",)),
        name="outproj",
    )(att, gla_y, x, gate, att_norm_g, w_out)


def _ffn_kernel(h_ref, g_ref, sc_ref, sh_ref, gate_ref, fg_ref, wu_ref, wd_ref, y_ref,
                hn_ref, acc_ref):
    f = pl.program_id(1)

    @pl.when(f == 0)
    def _():
        h = h_ref[...]
        rs = lax.rsqrt(jnp.mean(h * h, axis=-1, keepdims=True) + EPS)
        hn_ref[...] = (h * rs * g_ref[...] * (1.0 + sc_ref[...]) + sh_ref[...]).astype(BF16)
        acc_ref[...] = jnp.zeros_like(acc_ref)

    u = jnp.maximum(_dot(hn_ref[...], wu_ref[...]), 0.0)
    acc_ref[...] += _dot((u * u).astype(BF16), wd_ref[...])

    @pl.when(f == pl.num_programs(1) - 1)
    def _():
        h2 = h_ref[...] + gate_ref[...] * acc_ref[...]
        rs = lax.rsqrt(jnp.mean(h2 * h2, axis=-1, keepdims=True) + EPS)
        y_ref[...] = h2 * rs * fg_ref[...]


def _ffn(h, g, sc, sh, gate, final_g, w_up, w_down, *, tm, tf):
    t = h.shape[0]
    mod_rows = sc.shape[0]
    mod_map = (lambda i, f: (0, 0)) if mod_rows == 1 else (lambda i, f: (i, 0))
    mod_block = (1, D_MODEL) if mod_rows == 1 else (tm, D_MODEL)
    const = pl.BlockSpec((1, D_MODEL), lambda i, f: (0, 0))
    mod = pl.BlockSpec(mod_block, mod_map)
    return pl.pallas_call(
        _ffn_kernel,
        out_shape=jax.ShapeDtypeStruct((t, D_MODEL), F32),
        grid=(t // tm, D_FF // tf),
        in_specs=[
            pl.BlockSpec((tm, D_MODEL), lambda i, f: (i, 0)),
            const, mod, mod, mod, const,
            pl.BlockSpec((D_MODEL, tf), lambda i, f: (0, f)),
            pl.BlockSpec((tf, D_MODEL), lambda i, f: (f, 0)),
        ],
        out_specs=pl.BlockSpec((tm, D_MODEL), lambda i, f: (i, 0)),
        scratch_shapes=[pltpu.VMEM((tm, D_MODEL), BF16), pltpu.VMEM((tm, D_MODEL), F32)],
        compiler_params=_cparams(("arbitrary", "arbitrary")),
        name="ffn",
    )(h, g, sc, sh, gate, final_g, w_up, w_down)


def _split_ada(ada):
    return [ada[:, D_MODEL * i:D_MODEL * (i + 1)] for i in range(6)]


def kernel(x_prompt, x_sample, c_prompt, c_sample, cache_win_k, cache_win_v, state_gla, w_in, gla_gate_w, gla_gate_b, att_norm_g, gla_norm_g, w_out, norm1_g, norm2_g, w_ada, b_ada, w_up, w_down, final_g):
    assert w_in.shape[0] == 1, "single-layer model"
    n_p, t_p, _ = x_prompt.shape
    n_s, t_s, _ = x_sample.shape
    assert n_p == 1
    rows_s = n_s * t_s
    wb_p = min(MAX_WINDOW, t_p)

    n_main = 6 * 1024
    w_main = w_in[0, :, :n_main].astype(BF16)
    w_ag = jnp.pad(w_in[0, :, n_main:], ((0, 0), (0, LANES - GLA_GATE_RANK))).astype(BF16)
    gate_w = jnp.pad(gla_gate_w[0], ((0, LANES - GLA_GATE_RANK), (0, 0))).astype(BF16)
    gate_b = gla_gate_b[0][None, :]
    w_out_b = w_out[0].astype(BF16)
    w_up_b = w_up[0].astype(BF16)
    w_down_b = w_down[0].astype(BF16)
    n1g, n2g = norm1_g[0][None, :], norm2_g[0][None, :]
    ang, gng = att_norm_g[0][None, :], gla_norm_g[0][None, :]
    fg = final_g[None, :]

    c_rows = n_p + n_s
    c_pad = -c_rows % 8
    c_all = jnp.concatenate([c_prompt, c_sample, jnp.zeros((c_pad, D_MODEL), F32)], axis=0)
    ada = _ada(c_all, w_ada[0], b_ada[0][None, :])
    sh1_p, sc1_p, g1_p, sh2_p, sc2_p, g2_p = _split_ada(ada[:n_p])
    sh1_s, sc1_s, g1_s, sh2_s, sc2_s, g2_s = _split_ada(
        jnp.repeat(ada[n_p:c_rows], t_s, axis=0))

    xp = x_prompt[0]
    cs_p, sa_p, sb_p = _rope_tables(jnp.arange(t_p))
    qa, ka, va, gla_in, log_a, win_k, win_v = _inproj(
        xp, n1g, sc1_p, sh1_p, w_main, w_ag, gate_w, gate_b, cs_p, sa_p, sb_p,
        tm=512, tail_rows=wb_p)
    att = _prompt_attention(qa, ka, va)
    gla_y, st_p = _gla_prompt(gla_in, log_a, gng, tb=512, chunk=64, sub=16)
    h_p = _outproj(att, gla_y, xp, g1_p, ang, w_out_b, tm=512)
    y_p = _ffn(h_p, n2g, sc2_p, sh2_p, g2_p, fg, w_up_b, w_down_b, tm=512, tf=512)

    xs = x_sample.reshape(rows_s, D_MODEL)
    cs_s, sa_s, sb_s = _rope_tables(jnp.tile(PAST_LEN + jnp.arange(t_s), n_s))
    qa_s, _, _, gla_in_s, log_a_s, new_k, new_v = _inproj(
        xs, n1g, sc1_s, sh1_s, w_main, w_ag, gate_w, gate_b, cs_s, sa_s, sb_s,
        tm=rows_s, tail_rows=rows_s)
    wc = cache_win_k.shape[2]
    q_tok = jnp.transpose(qa_s, (1, 0, 2)).reshape(rows_s, ATT_WIDTH).astype(F32)
    att_tok = _sample_attention(
        q_tok, new_k, new_v, cache_win_k[0].reshape(n_s, wc, ATT_WIDTH),
        cache_win_v[0].reshape(n_s, wc, ATT_WIDTH), steps=t_s)
    att_s = jnp.transpose(att_tok.reshape(rows_s, HEAD_PAIRS, LANES), (1, 0, 2)).astype(BF16)
    gla_y_s, st_s = _gla_sample(gla_in_s.astype(F32), log_a_s, gng, state_gla[0], steps=t_s)
    h_s = _outproj(att_s, gla_y_s.astype(BF16), xs, g1_s, ang, w_out_b, tm=rows_s)
    y_s = _ffn(h_s, n2g, sc2_s, sh2_s, g2_s, fg, w_up_b, w_down_b, tm=rows_s, tf=512)

    return (
        y_p[None],
        y_s.reshape(n_s, t_s, D_MODEL),
        win_k.reshape(1, n_p, wb_p, ATT_HEADS, ATT_HEAD_DIM),
        win_v.reshape(1, n_p, wb_p, ATT_HEADS, ATT_HEAD_DIM),
        st_p[None, None],
        new_k.reshape(1, n_s, t_s, ATT_HEADS, ATT_HEAD_DIM),
        new_v.reshape(1, n_s, t_s, ATT_HEADS, ATT_HEAD_DIM),
        st_s[None],
    )
```

```python
import functools

import jax
import jax.numpy as jnp
from jax import lax
from jax.experimental import pallas as pl
from jax.experimental.pallas import tpu as pltpu

F32 = jnp.float32
BF16 = jnp.bfloat16

D_MODEL = 2048
ATT_WIDTH = 1024
ATT_HEADS = 16
ATT_HEAD_DIM = 64
ROT_DIM = 16
ROPE_THETA = 500000.0
DILATIONS = (1, 4, 16)
BAND = 128
MAX_WINDOW = 2048
GLA_HEADS = 4
GLA_DK = 128
GLA_DV = 256
GLA_WIDTH = 1024
GLA_GATE_RANK = 16
GLA_TAU = 16.0
D_FF = 8192
EPS = 1e-6
PAST_LEN = 16384
NEG = -1e30

LANES = 128
HEAD_PAIRS = ATT_WIDTH // LANES
VMEM_LIMIT = 56 * 1024 * 1024


def _cparams(sem):
    return pltpu.CompilerParams(dimension_semantics=sem, vmem_limit_bytes=VMEM_LIMIT)


def _dot(a, b):
    return jnp.dot(a, b, preferred_element_type=F32)


def _dot_nt(a, b):
    return lax.dot_general(a, b, (((1,), (1,)), ((), ())), preferred_element_type=F32)


def _dot_tn(a, b):
    return lax.dot_general(a, b, (((0,), (0,)), ((), ())), preferred_element_type=F32)


def _silu(x):
    return x / (1.0 + jnp.exp(-x))


def _ada_kernel(c_ref, w_ref, b_ref, o_ref):
    s = _silu(c_ref[...]).astype(BF16)
    o_ref[...] = _dot(s, w_ref[...].astype(BF16)) + b_ref[...]


def _ada(c, w_ada, b_ada):
    rows = c.shape[0]
    n = w_ada.shape[1]
    tn = 1024
    return pl.pallas_call(
        _ada_kernel,
        out_shape=jax.ShapeDtypeStruct((rows, n), F32),
        grid=(n // tn,),
        in_specs=[
            pl.BlockSpec((rows, D_MODEL), lambda j: (0, 0)),
            pl.BlockSpec((D_MODEL, tn), lambda j: (0, j)),
            pl.BlockSpec((1, tn), lambda j: (0, j)),
        ],
        out_specs=pl.BlockSpec((rows, tn), lambda j: (0, j)),
        compiler_params=_cparams(("arbitrary",)),
        name="ada",
    )(c, w_ada, b_ada)


def _rope(slab, cs, sn):
    d = lax.broadcasted_iota(jnp.int32, (1, LANES), 1) % ATT_HEAD_DIM
    half = ROT_DIM // 2
    from_hi = jnp.where(d < half, -1.0, 0.0)
    from_lo = jnp.where((d >= half) & (d < ROT_DIM), 1.0, 0.0)
    return (slab * cs + pltpu.roll(slab, LANES - half, 1) * (sn * from_hi)
            + pltpu.roll(slab, half, 1) * (sn * from_lo))


def _norm_mod(x, g, sc, sh):
    rs = lax.rsqrt(jnp.mean(x * x, axis=-1, keepdims=True) + EPS)
    return x * rs * g * (1.0 + sc) + sh


def _inproj_kernel(*refs, emit_att):
    (x_ref, g_ref, sc_ref, sh_ref, w_ref, wag_ref, gw_ref, gb_ref, cs_ref, sn_ref), refs = (
        refs[:10], refs[10:])
    if emit_att:
        att_refs, refs = refs[:9], refs[9:]
    gla_ref, la_ref, xn_ref = refs[:3]
    j = pl.program_id(1)
    col = j if emit_att else j + 3

    @pl.when(j == 0)
    def _():
        xnb = _norm_mod(x_ref[...], g_ref[...], sc_ref[...], sh_ref[...]).astype(BF16)
        xn_ref[...] = xnb
        ag = _dot(xnb, wag_ref[...])
        z = _dot(ag.astype(BF16), gw_ref[...]) + gb_ref[...]
        log_sig = jnp.minimum(z, 0.0) - jnp.log(1.0 + jnp.exp(-jnp.abs(z)))
        la_ref[...] = log_sig / GLA_TAU

    acc = _dot(xn_ref[...], w_ref[...])

    if emit_att:
        s0_ref, s1_ref = refs[3:5]
        tm = acc.shape[0]
        n4, n16 = tm // 4, tm // 16

        def emit(out1, out4, out16, slab_fn):
            for hp in range(HEAD_PAIRS):
                slab = slab_fn(acc[:, LANES * hp:LANES * (hp + 1)])
                out1[hp] = slab.astype(BF16)
                s0_ref[hp] = slab
                for r in range(4):
                    part = s0_ref[hp, pl.ds(r, n4, stride=4), :]
                    out4[hp, :, LANES * r:LANES * (r + 1)] = part.astype(BF16)
                    s1_ref[hp, n4 * r:n4 * (r + 1), :] = part
                for r in range(16):
                    part = s1_ref[hp, pl.ds(n4 * (r % 4) + r // 4, n16, stride=4), :]
                    out16[hp, :, LANES * r:LANES * (r + 1)] = part.astype(BF16)

        @pl.when(col == 0)
        def _():
            emit(*att_refs[0:3],
                 lambda s: _rope(s, cs_ref[...], sn_ref[...]) * (ATT_HEAD_DIM ** -0.5))

        @pl.when(col == 1)
        def _():
            emit(*att_refs[3:6], lambda s: _rope(s, cs_ref[...], sn_ref[...]))

        @pl.when(col == 2)
        def _():
            emit(*att_refs[6:9], lambda s: s)

    @pl.when(col == 3)
    def _():
        half = GLA_HEADS * GLA_DK
        gla_ref[:, :half] = (acc[:, :half] * (GLA_DK ** -0.5)).astype(BF16)
        gla_ref[:, half:] = acc[:, half:].astype(BF16)

    @pl.when(col >= 4)
    def _():
        gla_ref[...] = acc.astype(BF16)


def _mod_spec(mod_rows, tm):
    if mod_rows == 1:
        return pl.BlockSpec((1, D_MODEL), lambda i, j: (0, 0))
    return pl.BlockSpec((tm, D_MODEL), lambda i, j: (i, 0))


def _inproj(x, g, sc, sh, w_main, w_ag, gate_w, gate_b, cs, sn, *, tm, emit_att):
    t = x.shape[0]
    col0 = 0 if emit_att else 3
    mod = _mod_spec(sc.shape[0], tm)
    out_shape, out_specs, scratch = [], [], [pltpu.VMEM((tm, D_MODEL), BF16)]
    if emit_att:
        for _ in range(3):
            for d in DILATIONS:
                out_shape.append(jax.ShapeDtypeStruct((HEAD_PAIRS, t // d, d * LANES), BF16))
                out_specs.append(pl.BlockSpec((HEAD_PAIRS, tm // d, d * LANES),
                                              lambda i, j: (0, i, 0)))
        scratch += [pltpu.VMEM((HEAD_PAIRS, tm, LANES), F32) for _ in range(2)]
    out_shape += [jax.ShapeDtypeStruct((t, 3 * 1024), BF16),
                  jax.ShapeDtypeStruct((t, GLA_HEADS * GLA_DK), F32)]
    out_specs += [pl.BlockSpec((tm, 1024), lambda i, j: (i, jnp.maximum(j + col0 - 3, 0))),
                  pl.BlockSpec((tm, GLA_HEADS * GLA_DK), lambda i, j: (i, 0))]
    return pl.pallas_call(
        functools.partial(_inproj_kernel, emit_att=emit_att),
        out_shape=tuple(out_shape),
        grid=(t // tm, 6 - col0),
        in_specs=[
            pl.BlockSpec((tm, D_MODEL), lambda i, j: (i, 0)),
            pl.BlockSpec((1, D_MODEL), lambda i, j: (0, 0)),
            mod, mod,
            pl.BlockSpec((D_MODEL, 1024), lambda i, j: (0, j + col0)),
            pl.BlockSpec((D_MODEL, LANES), lambda i, j: (0, 0)),
            pl.BlockSpec((LANES, GLA_HEADS * GLA_DK), lambda i, j: (0, 0)),
            pl.BlockSpec((1, GLA_HEADS * GLA_DK), lambda i, j: (0, 0)),
            pl.BlockSpec((tm, LANES), lambda i, j: (i, 0)),
            pl.BlockSpec((tm, LANES), lambda i, j: (i, 0)),
        ],
        out_specs=tuple(out_specs),
        scratch_shapes=scratch,
        compiler_params=_cparams(("arbitrary", "arbitrary")),
        name="inproj_att" if emit_att else "inproj_gla",
    )(x, g, sc, sh, w_main, w_ag, gate_w, gate_b, cs, sn)


def _qkv_rows_kernel(x_ref, g_ref, sc_ref, sh_ref, w_ref, cs_ref, sn_ref, o_ref, xn_ref, *, col0):
    j = pl.program_id(1)

    @pl.when(j == 0)
    def _():
        xn_ref[...] = _norm_mod(x_ref[...], g_ref[...], sc_ref[...], sh_ref[...]).astype(BF16)

    acc = _dot(xn_ref[...], w_ref[...])
    col = j + col0

    @pl.when(col <= 1)
    def _():
        scale = jnp.where(col == 0, ATT_HEAD_DIM ** -0.5, 1.0)
        for hp in range(HEAD_PAIRS):
            lanes = slice(LANES * hp, LANES * (hp + 1))
            o_ref[0, :, lanes] = _rope(acc[:, lanes], cs_ref[...], sn_ref[...]) * scale

    @pl.when(col == 2)
    def _():
        o_ref[0] = acc


def _qkv_rows(x, g, sc, sh, w_main, cs, sn, *, tm, first_row, rows, col0):
    first = first_row // tm
    assert sc.shape[0] == 1 or first == 0
    ncols = 3 - col0
    mod = _mod_spec(sc.shape[0], tm)
    return pl.pallas_call(
        functools.partial(_qkv_rows_kernel, col0=col0),
        out_shape=jax.ShapeDtypeStruct((ncols, rows, ATT_WIDTH), F32),
        grid=(rows // tm, ncols),
        in_specs=[
            pl.BlockSpec((tm, D_MODEL), lambda i, j: (i + first, 0)),
            pl.BlockSpec((1, D_MODEL), lambda i, j: (0, 0)),
            mod, mod,
            pl.BlockSpec((D_MODEL, 1024), lambda i, j: (0, j + col0)),
            pl.BlockSpec((tm, LANES), lambda i, j: (i + first, 0)),
            pl.BlockSpec((tm, LANES), lambda i, j: (i + first, 0)),
        ],
        out_specs=pl.BlockSpec((1, tm, ATT_WIDTH), lambda i, j: (j, i, 0)),
        scratch_shapes=[pltpu.VMEM((tm, D_MODEL), BF16)],
        compiler_params=_cparams(("arbitrary", "arbitrary")),
        name="qkv_rows",
    )(x, g, sc, sh, w_main, cs, sn)


def _rope_tables(pos):
    half = ROT_DIM // 2
    inv_freq = ROPE_THETA ** (-jnp.arange(half, dtype=F32) / half)
    per_head = jnp.concatenate([inv_freq, inv_freq, jnp.zeros((ATT_HEAD_DIM - ROT_DIM,), F32)])
    ang = pos.astype(F32)[:, None] * jnp.tile(per_head, LANES // ATT_HEAD_DIM)[None, :]
    return jnp.cos(ang), jnp.sin(ang)


def _attn_kernel(q1, k1c, v1c, k1p, v1p, q4, k4c, v4c, k4p, v4p, q16, k16c, v16c, k16p, v16p,
                 o_ref, a1, m1, l1, a4, m4, l4, t16, s_scr, p_scr, bias_ref, o_scr):
    first_block = pl.program_id(1) == 0
    lane = lax.broadcasted_iota(jnp.int32, (1, LANES), 1)
    first = lane < ATT_HEAD_DIM
    head0 = jnp.where(first, 1.0, 0.0).astype(BF16)
    head1 = jnp.where(first, 0.0, 1.0).astype(BF16)
    ones = jnp.ones((2 * BAND, LANES), BF16)

    qi = lax.broadcasted_iota(jnp.int32, (2 * BAND, 2 * BAND), 0) % BAND
    kk = lax.broadcasted_iota(jnp.int32, (2 * BAND, 2 * BAND), 1)
    full = jnp.where(jnp.where(kk < BAND, kk - qi, qi - (kk - BAND)) >= 0, 0.0, NEG)
    bias_ref[0] = full
    bias_ref[1] = jnp.where(kk < BAND, full + jnp.where(first_block, NEG, 0.0), full)

    n1 = q1.shape[1] // BAND
    rows4 = q4.shape[1]
    n4 = rows4 // BAND

    def window(cur, prev, i, cols):
        if i == 0:
            return jnp.concatenate([prev[0, :, cols], cur[0, :BAND, cols]], axis=0)
        return cur[0, BAND * (i - 1):BAND * (i + 1), cols]

    jobs = []
    everything = slice(None)
    for i in range(n1):
        jobs.append(dict(q=(q1, i, everything), k=(k1c, k1p), v=(v1c, v1p), bias=int(i == 0),
                         dst=(a1, m1, l1, BAND * i)))
    for r in range(4):
        cols = slice(LANES * r, LANES * (r + 1))
        for i in range(n4):
            jobs.append(dict(q=(q4, i, cols), k=(k4c, k4p), v=(v4c, v4p), bias=int(i == 0),
                             dst=(a4, m4, l4, rows4 * r + BAND * i)))
    for r in range(16):
        cols = slice(LANES * r, LANES * (r + 1))
        jobs.append(dict(q=(q16, 0, cols), k=(k16c, k16p), v=(v16c, v16p), bias=1, merge=r))

    def scores(job, slot):
        ref, i, cols = job["q"]
        q = ref[0, BAND * i:BAND * (i + 1), cols]
        q2 = jnp.concatenate([q * head0, q * head1], axis=0)
        s_scr[slot] = _dot_nt(q2, window(*job["k"], i, cols))

    def softmax(job, slot):
        s = s_scr[slot] + bias_ref[job["bias"]]
        m = jnp.max(s, axis=-1, keepdims=True)
        p_scr[slot] = jnp.exp(s - m).astype(BF16)
        m = jnp.where(first, m[:BAND], m[BAND:])
        if "merge" in job:
            t16[slot] = m
        else:
            job["dst"][1][pl.ds(job["dst"][3], BAND), :] = m

    def values(job, slot):
        _, i, cols = job["q"]
        vw = jnp.concatenate([window(*job["v"], i, cols), ones], axis=1)
        o2 = _dot(p_scr[slot], vw)
        acc = jnp.where(first, o2[:BAND, :LANES], o2[BAND:, :LANES])
        den = jnp.where(first, o2[:BAND, LANES:], o2[BAND:, LANES:])
        if "merge" not in job:
            a_ref, _, l_ref, row = job["dst"]
            a_ref[pl.ds(row, BAND), :] = acc
            l_ref[pl.ds(row, BAND), :] = den
            return
        r = job["merge"]
        sl1 = pl.ds(r, BAND, stride=16)
        sl4 = pl.ds((r % 4) * rows4 + r // 4, BAND, stride=4)
        ma, mb, mc = m1[sl1, :], m4[sl4, :], t16[slot]
        m = jnp.maximum(jnp.maximum(ma, mb), mc)
        wa, wb, wc = jnp.exp(ma - m), jnp.exp(mb - m), jnp.exp(mc - m)
        num = wa * a1[sl1, :] + wb * a4[sl4, :] + wc * acc
        dsum = wa * l1[sl1, :] + wb * l4[sl4, :] + wc * den
        o_scr[sl1, :] = num / dsum

    slots = s_scr.shape[0]
    for step in range(len(jobs) + 2):
        if step < len(jobs):
            scores(jobs[step], step % slots)
        if 1 <= step <= len(jobs):
            softmax(jobs[step - 1], (step - 1) % slots)
        if step >= 2:
            values(jobs[step - 2], (step - 2) % slots)
    o_ref[0] = o_scr[...].astype(o_ref.dtype)


def _prompt_attention(views):
    hp, t, _ = views[0].shape
    sb = 16 * BAND
    in_specs, args = [], []
    for di, d in enumerate(DILATIONS):
        rows, width = sb // d, d * LANES
        cur = pl.BlockSpec((1, rows, width), lambda h, s: (h, s, 0))
        nb = rows // BAND
        prev = pl.BlockSpec((1, BAND, width),
                            lambda h, s, nb=nb: (h, jnp.maximum(s * nb - 1, 0), 0))
        q, k, v = views[di], views[3 + di], views[6 + di]
        in_specs += [cur, cur, cur, prev, prev]
        args += [q, k, v, k, v]
    tile = (2 * BAND, 2 * BAND)
    return pl.pallas_call(
        _attn_kernel,
        out_shape=jax.ShapeDtypeStruct((hp, t, LANES), BF16),
        grid=(hp, t // sb),
        in_specs=in_specs,
        out_specs=pl.BlockSpec((1, sb, LANES), lambda h, s: (h, s, 0)),
        scratch_shapes=(
            [pltpu.VMEM((sb, LANES), F32) for _ in range(6)]
            + [pltpu.VMEM((3, BAND, LANES), F32),
               pltpu.VMEM((3,) + tile, F32),
               pltpu.VMEM((3,) + tile, BF16),
               pltpu.VMEM((2,) + tile, F32),
               pltpu.VMEM((sb, LANES), F32)]),
        compiler_params=_cparams(("arbitrary", "arbitrary")),
        name="prompt_attn",
    )(*args)


def _gla_chunk(q, k, v, la, st_ref, h, tri, sub):
    c = q.shape[0]
    la_hi = la.astype(BF16)
    la_lo = (la - la_hi.astype(F32)).astype(BF16)
    b = _dot(tri, la_hi) + _dot(tri, la_lo)
    st = st_ref[h]
    o = _dot_nt((q * jnp.exp(b)).astype(BF16), st.astype(BF16))
    srow = lax.broadcasted_iota(jnp.int32, (c, GLA_DK), 0)
    att_rows = []
    for i in range(c // sub):
        lo, hi = i * sub, (i + 1) * sub
        ref = b[lo:lo + 1, :]
        qh = (q[lo:hi] * jnp.exp(b[lo:hi] - ref)).astype(BF16)
        kh = (k * jnp.exp(jnp.where(srow < hi, ref - b, 0.0))).astype(BF16)
        a = _dot_nt(qh, kh)
        ti = lax.broadcasted_iota(jnp.int32, (sub, c), 0) + lo
        si = lax.broadcasted_iota(jnp.int32, (sub, c), 1)
        att_rows.append(jnp.where(si <= ti, a, 0.0))
    att = att_rows[0] if len(att_rows) == 1 else jnp.concatenate(att_rows, axis=0)
    o = o + _dot(att.astype(BF16), v)
    b_last = b[c - 1:c, :]
    kbar = (k * jnp.exp(b_last - b)).astype(BF16)
    st_ref[h] = st * jnp.exp(b_last) + _dot_tn(v, kbar)
    return o


def _gla_out(o, g, gg):
    rs = lax.rsqrt(jnp.mean(o * o, axis=-1, keepdims=True) + EPS)
    return o * rs * g * _silu(gg)


def _tri(c):
    r = lax.broadcasted_iota(jnp.int32, (c, c), 0)
    s = lax.broadcasted_iota(jnp.int32, (c, c), 1)
    return jnp.where(s <= r, 1.0, 0.0).astype(BF16)


def _gla_prompt_kernel(qk_ref, v_ref, gg_ref, la_ref, g_ref, y_ref, s_ref, st_ref, *, chunk, sub):
    i = pl.program_id(0)

    @pl.when(i == 0)
    def _():
        st_ref[...] = jnp.zeros_like(st_ref)

    tri = _tri(chunk)
    nchunks = qk_ref.shape[0] // chunk
    half = GLA_HEADS * GLA_DK

    for ci in range(nchunks):
        rows = slice(ci * chunk, (ci + 1) * chunk)
        for h in range(GLA_HEADS):
            kc = slice(GLA_DK * h, GLA_DK * (h + 1))
            vc = slice(GLA_DV * h, GLA_DV * (h + 1))
            q = qk_ref[rows, kc].astype(F32)
            k = qk_ref[rows, half + GLA_DK * h:half + GLA_DK * (h + 1)].astype(F32)
            o = _gla_chunk(q, k, v_ref[rows, vc], la_ref[rows, kc], st_ref, h, tri, sub)
            y_ref[rows, vc] = _gla_out(o, g_ref[:, vc], gg_ref[rows, vc].astype(F32)).astype(y_ref.dtype)

    @pl.when(i == pl.num_programs(0) - 1)
    def _():
        for h in range(GLA_HEADS):
            s_ref[h] = st_ref[h].T


def _gla_prompt(gla_in, log_a, norm_g, *, tb, chunk, sub):
    t = gla_in.shape[0]
    return pl.pallas_call(
        functools.partial(_gla_prompt_kernel, chunk=chunk, sub=sub),
        out_shape=(
            jax.ShapeDtypeStruct((t, GLA_WIDTH), BF16),
            jax.ShapeDtypeStruct((GLA_HEADS, GLA_DK, GLA_DV), F32),
        ),
        grid=(t // tb,),
        in_specs=[
            pl.BlockSpec((tb, 1024), lambda i: (i, 0)),
            pl.BlockSpec((tb, 1024), lambda i: (i, 1)),
            pl.BlockSpec((tb, 1024), lambda i: (i, 2)),
            pl.BlockSpec((tb, GLA_HEADS * GLA_DK), lambda i: (i, 0)),
            pl.BlockSpec((1, GLA_WIDTH), lambda i: (0, 0)),
        ],
        out_specs=(
            pl.BlockSpec((tb, GLA_WIDTH), lambda i: (i, 0)),
            pl.BlockSpec((GLA_HEADS, GLA_DK, GLA_DV), lambda i: (0, 0, 0)),
        ),
        scratch_shapes=[pltpu.VMEM((GLA_HEADS, GLA_DV, GLA_DK), F32)],
        compiler_params=_cparams(("arbitrary",)),
        name="gla_prompt",
    )(gla_in, gla_in, gla_in, log_a, norm_g)


def _gla_sample_kernel(qk_ref, v_ref, gg_ref, la_ref, g_ref, s0_ref, y_ref, s_ref, st_ref):
    steps = qk_ref.shape[0]
    c = 2 * steps
    tri = _tri(c)
    half = GLA_HEADS * GLA_DK
    pad = lambda a: jnp.concatenate([a, jnp.zeros_like(a)], axis=0)
    for h in range(GLA_HEADS):
        st_ref[h] = s0_ref[0, h].T
    for h in range(GLA_HEADS):
        kc = slice(GLA_DK * h, GLA_DK * (h + 1))
        vc = slice(GLA_DV * h, GLA_DV * (h + 1))
        q = pad(qk_ref[:, kc])
        k = pad(qk_ref[:, half + GLA_DK * h:half + GLA_DK * (h + 1)])
        v = pad(v_ref[:, vc]).astype(BF16)
        o = _gla_chunk(q, k, v, pad(la_ref[:, kc]), st_ref, h, tri, c)
        y_ref[:, vc] = _gla_out(o[:steps], g_ref[:, vc], gg_ref[:, vc])
    for h in range(GLA_HEADS):
        s_ref[0, h] = st_ref[h].T


def _gla_sample(gla_in, log_a, norm_g, state, *, steps):
    t = gla_in.shape[0]
    nb = t // steps
    return pl.pallas_call(
        _gla_sample_kernel,
        out_shape=(
            jax.ShapeDtypeStruct((t, GLA_WIDTH), F32),
            jax.ShapeDtypeStruct(state.shape, F32),
        ),
        grid=(nb,),
        in_specs=[
            pl.BlockSpec((steps, 1024), lambda n: (n, 0)),
            pl.BlockSpec((steps, 1024), lambda n: (n, 1)),
            pl.BlockSpec((steps, 1024), lambda n: (n, 2)),
            pl.BlockSpec((steps, GLA_HEADS * GLA_DK), lambda n: (n, 0)),
            pl.BlockSpec((1, GLA_WIDTH), lambda n: (0, 0)),
            pl.BlockSpec((1, GLA_HEADS, GLA_DK, GLA_DV), lambda n: (n, 0, 0, 0)),
        ],
        out_specs=(
            pl.BlockSpec((steps, GLA_WIDTH), lambda n: (n, 0)),
            pl.BlockSpec((1, GLA_HEADS, GLA_DK, GLA_DV), lambda n: (n, 0, 0, 0)),
        ),
        scratch_shapes=[pltpu.VMEM((GLA_HEADS, GLA_DV, GLA_DK), F32)],
        compiler_params=_cparams(("arbitrary",)),
        name="gla_sample",
    )(gla_in, gla_in, gla_in, log_a, norm_g, state)


def _sample_attn_kernel(q_ref, kn_ref, vn_ref, ck_ref, cv_ref, mc_ref, mn_ref, o_ref):
    steps = q_ref.shape[0]
    rows = ATT_HEADS * steps
    qt = jnp.concatenate([q_ref[...]] * ATT_HEADS, axis=0)
    rh = lax.broadcasted_iota(jnp.int32, (rows, ATT_WIDTH), 0) // steps
    ch = lax.broadcasted_iota(jnp.int32, (rows, ATT_WIDTH), 1) // ATT_HEAD_DIM
    own = rh == ch
    qb = jnp.where(own, qt, 0.0).astype(BF16)
    pad = jnp.zeros((LANES - steps, ATT_WIDTH), F32)
    kn = jnp.concatenate([kn_ref[...], pad], axis=0).astype(BF16)
    vn = jnp.concatenate([vn_ref[...], pad], axis=0).astype(BF16)
    sc = _dot_nt(qb, ck_ref[0].astype(BF16))
    sn = _dot_nt(qb, kn)
    mc, mn = mc_ref[...], mn_ref[...]
    sc = jnp.where(mc > 0.0, sc, NEG)
    sn = jnp.where(mn > 0.0, sn, NEG)
    m = jnp.maximum(jnp.max(sc, axis=-1, keepdims=True), jnp.max(sn, axis=-1, keepdims=True))
    pc = mc * jnp.exp(sc - m)
    pn = mn * jnp.exp(sn - m)
    den = jnp.sum(pc, axis=-1, keepdims=True) + jnp.sum(pn, axis=-1, keepdims=True)
    o = _dot(pc.astype(BF16), cv_ref[0].astype(BF16)) + _dot(pn.astype(BF16), vn)
    o = jnp.where(own, o / den, 0.0)
    o_ref[...] = jnp.sum(o.reshape(ATT_HEADS, steps, ATT_WIDTH), axis=0)


def _sample_multiplicity(steps, cache_len):
    t = jnp.arange(steps)[:, None]
    row_c = jnp.arange(cache_len)[None, :]
    row_n = cache_len + jnp.arange(LANES)[None, :]

    def mult(rows, in_range):
        delta = cache_len + t - rows
        total = jnp.zeros(delta.shape, F32)
        for d in DILATIONS:
            hit = (delta >= 0) & (delta <= BAND * d) & (delta % d == 0)
            total = total + hit.astype(F32)
        return jnp.where(in_range, total, 0.0)

    mc = mult(row_c, True)
    mn = mult(row_n, jnp.arange(LANES)[None, :] < steps)
    tile = lambda a: jnp.tile(a, (ATT_HEADS, 1))
    return tile(mc), tile(mn)


def _sample_attention(qa, k_new, v_new, cache_k, cache_v, *, steps):
    nb, cache_len, _ = cache_k.shape
    mc, mn = _sample_multiplicity(steps, cache_len)
    rows = ATT_HEADS * steps
    return pl.pallas_call(
        _sample_attn_kernel,
        out_shape=jax.ShapeDtypeStruct((nb * steps, ATT_WIDTH), F32),
        grid=(nb,),
        in_specs=[
            pl.BlockSpec((steps, ATT_WIDTH), lambda n: (n, 0)),
            pl.BlockSpec((steps, ATT_WIDTH), lambda n: (n, 0)),
            pl.BlockSpec((steps, ATT_WIDTH), lambda n: (n, 0)),
            pl.BlockSpec((1, cache_len, ATT_WIDTH), lambda n: (n, 0, 0)),
            pl.BlockSpec((1, cache_len, ATT_WIDTH), lambda n: (n, 0, 0)),
            pl.BlockSpec((rows, cache_len), lambda n: (0, 0)),
            pl.BlockSpec((rows, LANES), lambda n: (0, 0)),
        ],
        out_specs=pl.BlockSpec((steps, ATT_WIDTH), lambda n: (n, 0)),
        compiler_params=_cparams(("arbitrary",)),
        name="sample_attn",
    )(qa, k_new, v_new, cache_k, cache_v, mc, mn)


def _outproj_kernel(att_ref, gla_ref, x_ref, gate_ref, ng_ref, w_ref, h_ref):
    att = jnp.concatenate([att_ref[hp].astype(F32) for hp in range(HEAD_PAIRS)], axis=1)
    rs = lax.rsqrt(jnp.mean(att * att, axis=-1, keepdims=True) + EPS)
    att_y = (att * rs * ng_ref[...]).astype(BF16)
    mix = _dot(att_y, w_ref[:ATT_WIDTH, :]) + _dot(gla_ref[...], w_ref[ATT_WIDTH:, :])
    h_ref[...] = x_ref[...] + gate_ref[...] * mix


def _outproj(att, gla_y, x, gate, att_norm_g, w_out, *, tm):
    t = x.shape[0]
    mod_rows = gate.shape[0]
    mod_map = (lambda i: (0, 0)) if mod_rows == 1 else (lambda i: (i, 0))
    mod_block = (1, D_MODEL) if mod_rows == 1 else (tm, D_MODEL)
    return pl.pallas_call(
        _outproj_kernel,
        out_shape=jax.ShapeDtypeStruct((t, D_MODEL), F32),
        grid=(t // tm,),
        in_specs=[
            pl.BlockSpec((HEAD_PAIRS, tm, LANES), lambda i: (0, i, 0)),
            pl.BlockSpec((tm, GLA_WIDTH), lambda i: (i, 0)),
            pl.BlockSpec((tm, D_MODEL), lambda i: (i, 0)),
            pl.BlockSpec(mod_block, mod_map),
            pl.BlockSpec((1, ATT_WIDTH), lambda i: (0, 0)),
            pl.BlockSpec((ATT_WIDTH + GLA_WIDTH, D_MODEL), lambda i: (0, 0)),
        ],
        out_specs=pl.BlockSpec((tm, D_MODEL), lambda i: (i, 0)),
        compiler_params=_cparams(("arbitrary",)),
        name="outproj",
    )(att, gla_y, x, gate, att_norm_g, w_out)


def _ffn_kernel(h_ref, g_ref, sc_ref, sh_ref, gate_ref, fg_ref, wu_ref, wd_ref, y_ref, hn_ref):
    f = pl.program_id(1)

    @pl.when(f == 0)
    def _():
        hn_ref[...] = _norm_mod(h_ref[...], g_ref[...], sc_ref[...], sh_ref[...]).astype(BF16)
        y_ref[...] = jnp.zeros_like(y_ref)

    u = jnp.maximum(_dot(hn_ref[...], wu_ref[...]), 0.0)
    y_ref[...] += _dot((u * u).astype(BF16), wd_ref[...])

    @pl.when(f == pl.num_programs(1) - 1)
    def _():
        h2 = h_ref[...] + gate_ref[...] * y_ref[...]
        rs = lax.rsqrt(jnp.mean(h2 * h2, axis=-1, keepdims=True) + EPS)
        y_ref[...] = h2 * rs * fg_ref[...]


def _ffn(h, g, sc, sh, gate, final_g, w_up, w_down, *, tm, tf):
    t = h.shape[0]
    const = pl.BlockSpec((1, D_MODEL), lambda i, f: (0, 0))
    mod = _mod_spec(sc.shape[0], tm)
    return pl.pallas_call(
        _ffn_kernel,
        out_shape=jax.ShapeDtypeStruct((t, D_MODEL), F32),
        grid=(t // tm, D_FF // tf),
        in_specs=[
            pl.BlockSpec((tm, D_MODEL), lambda i, f: (i, 0)),
            const, mod, mod, mod, const,
            pl.BlockSpec((D_MODEL, tf), lambda i, f: (0, f)),
            pl.BlockSpec((tf, D_MODEL), lambda i, f: (f, 0)),
        ],
        out_specs=pl.BlockSpec((tm, D_MODEL), lambda i, f: (i, 0)),
        scratch_shapes=[pltpu.VMEM((tm, D_MODEL), BF16)],
        compiler_params=_cparams(("arbitrary", "arbitrary")),
        name="ffn",
    )(h, g, sc, sh, gate, final_g, w_up, w_down)


def _split_ada(ada):
    return [ada[:, D_MODEL * i:D_MODEL * (i + 1)] for i in range(6)]


def kernel(x_prompt, x_sample, c_prompt, c_sample, cache_win_k, cache_win_v, state_gla, w_in, gla_gate_w, gla_gate_b, att_norm_g, gla_norm_g, w_out, norm1_g, norm2_g, w_ada, b_ada, w_up, w_down, final_g):
    assert w_in.shape[0] == 1, "single-layer model"
    n_p, t_p, _ = x_prompt.shape
    n_s, t_s, _ = x_sample.shape
    assert n_p == 1
    rows_s = n_s * t_s
    wb_p = min(MAX_WINDOW, t_p)

    n_main = 6 * 1024
    w_main = w_in[0, :, :n_main].astype(BF16)
    w_ag = jnp.pad(w_in[0, :, n_main:], ((0, 0), (0, LANES - GLA_GATE_RANK))).astype(BF16)
    gate_w = jnp.pad(gla_gate_w[0], ((0, LANES - GLA_GATE_RANK), (0, 0))).astype(BF16)
    gate_b = gla_gate_b[0][None, :]
    w_out_b = w_out[0].astype(BF16)
    w_up_b = w_up[0].astype(BF16)
    w_down_b = w_down[0].astype(BF16)
    n1g, n2g = norm1_g[0][None, :], norm2_g[0][None, :]
    ang, gng = att_norm_g[0][None, :], gla_norm_g[0][None, :]
    fg = final_g[None, :]

    c_rows = n_p + n_s
    c_pad = -c_rows % 8
    c_all = jnp.concatenate([c_prompt, c_sample, jnp.zeros((c_pad, D_MODEL), F32)], axis=0)
    ada = _ada(c_all, w_ada[0], b_ada[0][None, :])
    sh1_p, sc1_p, g1_p, sh2_p, sc2_p, g2_p = _split_ada(ada[:n_p])
    sh1_s, sc1_s, g1_s, sh2_s, sc2_s, g2_s = _split_ada(
        jnp.repeat(ada[n_p:c_rows], t_s, axis=0))

    xp = x_prompt[0]
    cs_p, sn_p = _rope_tables(jnp.arange(t_p))
    outs = _inproj(xp, n1g, sc1_p, sh1_p, w_main, w_ag, gate_w, gate_b, cs_p, sn_p,
                   tm=512, emit_att=True)
    att = _prompt_attention(outs[:9])
    gla_in, log_a = outs[9], outs[10]
    win = _qkv_rows(xp, n1g, sc1_p, sh1_p, w_main, cs_p, sn_p,
                    tm=512, first_row=t_p - wb_p, rows=wb_p, col0=1)
    win_k, win_v = win[0], win[1]
    gla_y, st_p = _gla_prompt(gla_in, log_a, gng, tb=512, chunk=64, sub=16)
    h_p = _outproj(att, gla_y, xp, g1_p, ang, w_out_b, tm=512)
    y_p = _ffn(h_p, n2g, sc2_p, sh2_p, g2_p, fg, w_up_b, w_down_b, tm=1024, tf=512)

    xs = x_sample.reshape(rows_s, D_MODEL)
    cs_s, sn_s = _rope_tables(jnp.tile(PAST_LEN + jnp.arange(t_s), n_s))
    gla_in_s, log_a_s = _inproj(xs, n1g, sc1_s, sh1_s, w_main, w_ag, gate_w, gate_b, cs_s, sn_s,
                                tm=rows_s, emit_att=False)
    qkv_s = _qkv_rows(xs, n1g, sc1_s, sh1_s, w_main, cs_s, sn_s,
                      tm=rows_s, first_row=0, rows=rows_s, col0=0)
    new_k, new_v = qkv_s[1], qkv_s[2]
    wc = cache_win_k.shape[2]
    att_tok = _sample_attention(
        qkv_s[0], new_k, new_v, cache_win_k[0].reshape(n_s, wc, ATT_WIDTH),
        cache_win_v[0].reshape(n_s, wc, ATT_WIDTH), steps=t_s)
    att_s = jnp.transpose(att_tok.reshape(rows_s, HEAD_PAIRS, LANES), (1, 0, 2)).astype(BF16)
    gla_y_s, st_s = _gla_sample(gla_in_s.astype(F32), log_a_s, gng, state_gla[0], steps=t_s)
    h_s = _outproj(att_s, gla_y_s.astype(BF16), xs, g1_s, ang, w_out_b, tm=rows_s)
    y_s = _ffn(h_s, n2g, sc2_s, sh2_s, g2_s, fg, w_up_b, w_down_b, tm=rows_s, tf=512)

    return (
        y_p[None],
        y_s.reshape(n_s, t_s, D_MODEL),
        win_k.reshape(1, n_p, wb_p, ATT_HEADS, ATT_HEAD_DIM),
        win_v.reshape(1, n_p, wb_p, ATT_HEADS, ATT_HEAD_DIM),
        st_p[None, None],
        new_k.reshape(1, n_s, t_s, ATT_HEADS, ATT_HEAD_DIM),
        new_v.reshape(1, n_s, t_s, ATT_HEADS, ATT_HEAD_DIM),
        st_s[None],
    )
```

```python
import functools

import jax
import jax.numpy as jnp
from jax import lax
from jax.experimental import pallas as pl
from jax.experimental.pallas import tpu as pltpu

F32 = jnp.float32
BF16 = jnp.bfloat16

D_MODEL = 2048
ATT_WIDTH = 1024
ATT_HEADS = 16
ATT_HEAD_DIM = 64
ROT_DIM = 16
ROPE_THETA = 500000.0
DILATIONS = (1, 4, 16)
BAND = 128
MAX_WINDOW = 2048
GLA_HEADS = 4
GLA_DK = 128
GLA_DV = 256
GLA_WIDTH = 1024
GLA_GATE_RANK = 16
GLA_TAU = 16.0
D_FF = 8192
EPS = 1e-6
PAST_LEN = 16384
NEG = -1e30

LANES = 128
HEAD_PAIRS = ATT_WIDTH // LANES
VMEM_LIMIT = 56 * 1024 * 1024


def _cparams(sem):
    return pltpu.CompilerParams(dimension_semantics=sem, vmem_limit_bytes=VMEM_LIMIT)


def _dot(a, b):
    return jnp.dot(a, b, preferred_element_type=F32)


def _dot_nt(a, b):
    return lax.dot_general(a, b, (((1,), (1,)), ((), ())), preferred_element_type=F32)


def _dot_tn(a, b):
    return lax.dot_general(a, b, (((0,), (0,)), ((), ())), preferred_element_type=F32)


def _silu(x):
    return x / (1.0 + jnp.exp(-x))


def _ada_kernel(c_ref, w_ref, b_ref, o_ref):
    s = _silu(c_ref[...]).astype(BF16)
    o_ref[...] = _dot(s, w_ref[...].astype(BF16)) + b_ref[...]


def _ada(c, w_ada, b_ada):
    rows = c.shape[0]
    n = w_ada.shape[1]
    tn = 1024
    return pl.pallas_call(
        _ada_kernel,
        out_shape=jax.ShapeDtypeStruct((rows, n), F32),
        grid=(n // tn,),
        in_specs=[
            pl.BlockSpec((rows, D_MODEL), lambda j: (0, 0)),
            pl.BlockSpec((D_MODEL, tn), lambda j: (0, j)),
            pl.BlockSpec((1, tn), lambda j: (0, j)),
        ],
        out_specs=pl.BlockSpec((rows, tn), lambda j: (0, j)),
        compiler_params=_cparams(("arbitrary",)),
        name="ada",
    )(c, w_ada, b_ada)


def _rope(slab, cs, sn):
    d = lax.broadcasted_iota(jnp.int32, (1, LANES), 1) % ATT_HEAD_DIM
    half = ROT_DIM // 2
    from_hi = jnp.where(d < half, -1.0, 0.0)
    from_lo = jnp.where((d >= half) & (d < ROT_DIM), 1.0, 0.0)
    return (slab * cs + pltpu.roll(slab, LANES - half, 1) * (sn * from_hi)
            + pltpu.roll(slab, half, 1) * (sn * from_lo))


def _norm_mod(x, g, sc, sh):
    rs = lax.rsqrt(jnp.mean(x * x, axis=-1, keepdims=True) + EPS)
    return x * rs * g * (1.0 + sc) + sh


def _inproj_kernel(*refs, emit_att):
    (x_ref, g_ref, sc_ref, sh_ref, w_ref, wag_ref, gw_ref, gb_ref, cs_ref, sn_ref), refs = (
        refs[:10], refs[10:])
    if emit_att:
        att_refs, refs = refs[:9], refs[9:]
    gla_ref, la_ref, xn_ref = refs[:3]
    j = pl.program_id(1)
    col = j if emit_att else j + 3

    @pl.when(j == 0)
    def _():
        xnb = _norm_mod(x_ref[...], g_ref[...], sc_ref[...], sh_ref[...]).astype(BF16)
        xn_ref[...] = xnb
        ag = _dot(xnb, wag_ref[...])
        z = _dot(ag.astype(BF16), gw_ref[...]) + gb_ref[...]
        log_sig = jnp.minimum(z, 0.0) - jnp.log(1.0 + jnp.exp(-jnp.abs(z)))
        la_ref[...] = log_sig / GLA_TAU

    if emit_att:
        s0_ref, s1_ref = refs[3:5]
        tm = x_ref.shape[0]
        n4, n16 = tm // 4, tm // 16
        group = 2 * LANES

        def emit(out1, out4, out16, slab_fn):
            for hp in range(HEAD_PAIRS):
                if hp % 2 == 0:
                    acc = _dot(xn_ref[...], w_ref[:, LANES * hp:LANES * hp + group])
                slab = slab_fn(acc[:, LANES * (hp % 2):LANES * (hp % 2 + 1)])
                out1[hp] = slab.astype(BF16)
                s0_ref[hp] = slab
                for r in range(4):
                    part = s0_ref[hp, pl.ds(r, n4, stride=4), :]
                    out4[hp, :, LANES * r:LANES * (r + 1)] = part.astype(BF16)
                    s1_ref[hp, n4 * r:n4 * (r + 1), :] = part
                for r in range(16):
                    part = s1_ref[hp, pl.ds(n4 * (r % 4) + r // 4, n16, stride=4), :]
                    out16[hp, :, LANES * r:LANES * (r + 1)] = part.astype(BF16)

        @pl.when(col == 0)
        def _():
            emit(*att_refs[0:3],
                 lambda s: _rope(s, cs_ref[...], sn_ref[...]) * (ATT_HEAD_DIM ** -0.5))

        @pl.when(col == 1)
        def _():
            emit(*att_refs[3:6], lambda s: _rope(s, cs_ref[...], sn_ref[...]))

        @pl.when(col == 2)
        def _():
            emit(*att_refs[6:9], lambda s: s)

    @pl.when(col == 3)
    def _():
        acc = _dot(xn_ref[...], w_ref[...])
        half = GLA_HEADS * GLA_DK
        gla_ref[:, :half] = (acc[:, :half] * (GLA_DK ** -0.5)).astype(BF16)
        gla_ref[:, half:] = acc[:, half:].astype(BF16)

    @pl.when(col >= 4)
    def _():
        gla_ref[...] = _dot(xn_ref[...], w_ref[...]).astype(BF16)


def _mod_spec(mod_rows, tm):
    if mod_rows == 1:
        return pl.BlockSpec((1, D_MODEL), lambda i, j: (0, 0))
    return pl.BlockSpec((tm, D_MODEL), lambda i, j: (i, 0))


def _inproj(x, g, sc, sh, w_main, w_ag, gate_w, gate_b, cs, sn, *, tm, emit_att):
    t = x.shape[0]
    col0 = 0 if emit_att else 3
    mod = _mod_spec(sc.shape[0], tm)
    out_shape, out_specs, scratch = [], [], [pltpu.VMEM((tm, D_MODEL), BF16)]
    if emit_att:
        for _ in range(3):
            for d in DILATIONS:
                out_shape.append(jax.ShapeDtypeStruct((HEAD_PAIRS, t // d, d * LANES), BF16))
                out_specs.append(pl.BlockSpec((HEAD_PAIRS, tm // d, d * LANES),
                                              lambda i, j: (0, i, 0)))
        scratch += [pltpu.VMEM((HEAD_PAIRS, tm, LANES), F32) for _ in range(2)]
    out_shape += [jax.ShapeDtypeStruct((t, 3 * 1024), BF16),
                  jax.ShapeDtypeStruct((t, GLA_HEADS * GLA_DK), F32)]
    out_specs += [pl.BlockSpec((tm, 1024), lambda i, j: (i, jnp.maximum(j + col0 - 3, 0))),
                  pl.BlockSpec((tm, GLA_HEADS * GLA_DK), lambda i, j: (i, 0))]
    return pl.pallas_call(
        functools.partial(_inproj_kernel, emit_att=emit_att),
        out_shape=tuple(out_shape),
        grid=(t // tm, 6 - col0),
        in_specs=[
            pl.BlockSpec((tm, D_MODEL), lambda i, j: (i, 0)),
            pl.BlockSpec((1, D_MODEL), lambda i, j: (0, 0)),
            mod, mod,
            pl.BlockSpec((D_MODEL, 1024), lambda i, j: (0, j + col0)),
            pl.BlockSpec((D_MODEL, LANES), lambda i, j: (0, 0)),
            pl.BlockSpec((LANES, GLA_HEADS * GLA_DK), lambda i, j: (0, 0)),
            pl.BlockSpec((1, GLA_HEADS * GLA_DK), lambda i, j: (0, 0)),
            pl.BlockSpec((tm, LANES), lambda i, j: (i, 0)),
            pl.BlockSpec((tm, LANES), lambda i, j: (i, 0)),
        ],
        out_specs=tuple(out_specs),
        scratch_shapes=scratch,
        compiler_params=_cparams(("arbitrary", "arbitrary")),
        name="inproj_att" if emit_att else "inproj_gla",
    )(x, g, sc, sh, w_main, w_ag, gate_w, gate_b, cs, sn)


def _qkv_rows_kernel(x_ref, g_ref, sc_ref, sh_ref, w_ref, cs_ref, sn_ref, o_ref, xn_ref, *, col0):
    j = pl.program_id(1)

    @pl.when(j == 0)
    def _():
        xn_ref[...] = _norm_mod(x_ref[...], g_ref[...], sc_ref[...], sh_ref[...]).astype(BF16)

    acc = _dot(xn_ref[...], w_ref[...])
    col = j + col0

    @pl.when(col <= 1)
    def _():
        scale = jnp.where(col == 0, ATT_HEAD_DIM ** -0.5, 1.0)
        for hp in range(HEAD_PAIRS):
            lanes = slice(LANES * hp, LANES * (hp + 1))
            o_ref[0, :, lanes] = _rope(acc[:, lanes], cs_ref[...], sn_ref[...]) * scale

    @pl.when(col == 2)
    def _():
        o_ref[0] = acc


def _qkv_rows(x, g, sc, sh, w_main, cs, sn, *, tm, first_row, rows, col0):
    first = first_row // tm
    assert sc.shape[0] == 1 or first == 0
    ncols = 3 - col0
    mod = _mod_spec(sc.shape[0], tm)
    return pl.pallas_call(
        functools.partial(_qkv_rows_kernel, col0=col0),
        out_shape=jax.ShapeDtypeStruct((ncols, rows, ATT_WIDTH), F32),
        grid=(rows // tm, ncols),
        in_specs=[
            pl.BlockSpec((tm, D_MODEL), lambda i, j: (i + first, 0)),
            pl.BlockSpec((1, D_MODEL), lambda i, j: (0, 0)),
            mod, mod,
            pl.BlockSpec((D_MODEL, 1024), lambda i, j: (0, j + col0)),
            pl.BlockSpec((tm, LANES), lambda i, j: (i + first, 0)),
            pl.BlockSpec((tm, LANES), lambda i, j: (i + first, 0)),
        ],
        out_specs=pl.BlockSpec((1, tm, ATT_WIDTH), lambda i, j: (j, i, 0)),
        scratch_shapes=[pltpu.VMEM((tm, D_MODEL), BF16)],
        compiler_params=_cparams(("arbitrary", "arbitrary")),
        name="qkv_rows",
    )(x, g, sc, sh, w_main, cs, sn)


def _rope_tables(pos):
    half = ROT_DIM // 2
    inv_freq = ROPE_THETA ** (-jnp.arange(half, dtype=F32) / half)
    per_head = jnp.concatenate([inv_freq, inv_freq, jnp.zeros((ATT_HEAD_DIM - ROT_DIM,), F32)])
    ang = pos.astype(F32)[:, None] * jnp.tile(per_head, LANES // ATT_HEAD_DIM)[None, :]
    return jnp.cos(ang), jnp.sin(ang)


def _attn_kernel(q1, k1c, v1c, k1p, v1p, q4, k4c, v4c, k4p, v4p, q16, k16c, v16c, k16p, v16p,
                 o_ref, a1, m1, l1, a4, m4, l4, t16, s_scr, p_scr, bias_ref, o_scr):
    first_block = pl.program_id(1) == 0
    lane = lax.broadcasted_iota(jnp.int32, (1, LANES), 1)
    first = lane < ATT_HEAD_DIM
    head0 = jnp.where(first, 1.0, 0.0).astype(BF16)
    head1 = jnp.where(first, 0.0, 1.0).astype(BF16)
    ones = jnp.ones((2 * BAND, LANES), BF16)

    qi = lax.broadcasted_iota(jnp.int32, (2 * BAND, 2 * BAND), 0) % BAND
    kk = lax.broadcasted_iota(jnp.int32, (2 * BAND, 2 * BAND), 1)
    full = jnp.where(jnp.where(kk < BAND, kk - qi, qi - (kk - BAND)) >= 0, 0.0, NEG)
    bias_ref[0] = full
    bias_ref[1] = jnp.where(kk < BAND, full + jnp.where(first_block, NEG, 0.0), full)

    n1 = q1.shape[1] // BAND
    rows4 = q4.shape[1]
    n4 = rows4 // BAND

    def window(cur, prev, i, cols):
        if i == 0:
            return jnp.concatenate([prev[0, :, cols], cur[0, :BAND, cols]], axis=0)
        return cur[0, BAND * (i - 1):BAND * (i + 1), cols]

    jobs = []
    everything = slice(None)
    for i in range(n1):
        jobs.append(dict(q=(q1, i, everything), k=(k1c, k1p), v=(v1c, v1p), bias=int(i == 0),
                         dst=(a1, m1, l1, BAND * i)))
    for r in range(4):
        cols = slice(LANES * r, LANES * (r + 1))
        for i in range(n4):
            jobs.append(dict(q=(q4, i, cols), k=(k4c, k4p), v=(v4c, v4p), bias=int(i == 0),
                             dst=(a4, m4, l4, rows4 * r + BAND * i)))
    for r in range(16):
        cols = slice(LANES * r, LANES * (r + 1))
        jobs.append(dict(q=(q16, 0, cols), k=(k16c, k16p), v=(v16c, v16p), bias=1, merge=r))

    def scores(job, slot):
        ref, i, cols = job["q"]
        q = ref[0, BAND * i:BAND * (i + 1), cols]
        q2 = jnp.concatenate([q * head0, q * head1], axis=0)
        s_scr[slot] = _dot_nt(q2, window(*job["k"], i, cols))

    def softmax(job, slot):
        s = s_scr[slot] + bias_ref[job["bias"]]
        m = jnp.max(s, axis=-1, keepdims=True)
        p_scr[slot] = jnp.exp(s - m).astype(BF16)
        m = jnp.where(first, m[:BAND], m[BAND:])
        if "merge" in job:
            t16[slot] = m
        else:
            job["dst"][1][pl.ds(job["dst"][3], BAND), :] = m

    def values(job, slot):
        _, i, cols = job["q"]
        vw = jnp.concatenate([window(*job["v"], i, cols), ones], axis=1)
        o2 = _dot(p_scr[slot], vw)
        acc = jnp.where(first, o2[:BAND, :LANES], o2[BAND:, :LANES])
        den = jnp.where(first, o2[:BAND, LANES:], o2[BAND:, LANES:])
        if "merge" not in job:
            a_ref, _, l_ref, row = job["dst"]
            a_ref[pl.ds(row, BAND), :] = acc
            l_ref[pl.ds(row, BAND), :] = den
            return
        r = job["merge"]
        sl1 = pl.ds(r, BAND, stride=16)
        sl4 = pl.ds((r % 4) * rows4 + r // 4, BAND, stride=4)
        ma, mb, mc = m1[sl1, :], m4[sl4, :], t16[slot]
        m = jnp.maximum(jnp.maximum(ma, mb), mc)
        wa, wb, wc = jnp.exp(ma - m), jnp.exp(mb - m), jnp.exp(mc - m)
        num = wa * a1[sl1, :] + wb * a4[sl4, :] + wc * acc
        dsum = wa * l1[sl1, :] + wb * l4[sl4, :] + wc * den
        o_scr[sl1, :] = num / dsum

    slots = s_scr.shape[0]
    for step in range(len(jobs) + 2):
        if step < len(jobs):
            scores(jobs[step], step % slots)
        if 1 <= step <= len(jobs):
            softmax(jobs[step - 1], (step - 1) % slots)
        if step >= 2:
            values(jobs[step - 2], (step - 2) % slots)
    o_ref[0] = o_scr[...].astype(o_ref.dtype)


def _prompt_attention(views):
    hp, t, _ = views[0].shape
    sb = 16 * BAND
    in_specs, args = [], []
    for di, d in enumerate(DILATIONS):
        rows, width = sb // d, d * LANES
        cur = pl.BlockSpec((1, rows, width), lambda h, s: (h, s, 0))
        nb = rows // BAND
        prev = pl.BlockSpec((1, BAND, width),
                            lambda h, s, nb=nb: (h, jnp.maximum(s * nb - 1, 0), 0))
        q, k, v = views[di], views[3 + di], views[6 + di]
        in_specs += [cur, cur, cur, prev, prev]
        args += [q, k, v, k, v]
    tile = (2 * BAND, 2 * BAND)
    return pl.pallas_call(
        _attn_kernel,
        out_shape=jax.ShapeDtypeStruct((hp, t, LANES), BF16),
        grid=(hp, t // sb),
        in_specs=in_specs,
        out_specs=pl.BlockSpec((1, sb, LANES), lambda h, s: (h, s, 0)),
        scratch_shapes=(
            [pltpu.VMEM((sb, LANES), F32) for _ in range(6)]
            + [pltpu.VMEM((3, BAND, LANES), F32),
               pltpu.VMEM((3,) + tile, F32),
               pltpu.VMEM((3,) + tile, BF16),
               pltpu.VMEM((2,) + tile, F32),
               pltpu.VMEM((sb, LANES), F32)]),
        compiler_params=_cparams(("arbitrary", "arbitrary")),
        name="prompt_attn",
    )(*args)


def _gla_chunk(q, k, v, la, st_ref, h, tri, sub):
    c = q.shape[0]
    la_hi = la.astype(BF16)
    la_lo = (la - la_hi.astype(F32)).astype(BF16)
    b = _dot(tri, la_hi) + _dot(tri, la_lo)
    st = st_ref[h]
    o = _dot_nt((q * jnp.exp(b)).astype(BF16), st.astype(BF16))
    srow = lax.broadcasted_iota(jnp.int32, (c, GLA_DK), 0)
    att_rows = []
    for i in range(c // sub):
        lo, hi = i * sub, (i + 1) * sub
        ref = b[lo:lo + 1, :]
        qh = (q[lo:hi] * jnp.exp(b[lo:hi] - ref)).astype(BF16)
        kh = (k * jnp.exp(jnp.where(srow < hi, ref - b, 0.0))).astype(BF16)
        a = _dot_nt(qh, kh)
        ti = lax.broadcasted_iota(jnp.int32, (sub, c), 0) + lo
        si = lax.broadcasted_iota(jnp.int32, (sub, c), 1)
        att_rows.append(jnp.where(si <= ti, a, 0.0))
    att = att_rows[0] if len(att_rows) == 1 else jnp.concatenate(att_rows, axis=0)
    o = o + _dot(att.astype(BF16), v)
    b_last = b[c - 1:c, :]
    kbar = (k * jnp.exp(b_last - b)).astype(BF16)
    st_ref[h] = st * jnp.exp(b_last) + _dot_tn(v, kbar)
    return o


def _gla_out(o, g, gg):
    rs = lax.rsqrt(jnp.mean(o * o, axis=-1, keepdims=True) + EPS)
    return o * rs * g * _silu(gg)


def _tri(c):
    r = lax.broadcasted_iota(jnp.int32, (c, c), 0)
    s = lax.broadcasted_iota(jnp.int32, (c, c), 1)
    return jnp.where(s <= r, 1.0, 0.0).astype(BF16)


def _gla_prompt_kernel(qk_ref, v_ref, gg_ref, la_ref, g_ref, y_ref, s_ref, st_ref, *, chunk, sub):
    i = pl.program_id(0)

    @pl.when(i == 0)
    def _():
        st_ref[...] = jnp.zeros_like(st_ref)

    tri = _tri(chunk)
    nchunks = qk_ref.shape[0] // chunk
    half = GLA_HEADS * GLA_DK

    for ci in range(nchunks):
        rows = slice(ci * chunk, (ci + 1) * chunk)
        for h in range(GLA_HEADS):
            kc = slice(GLA_DK * h, GLA_DK * (h + 1))
            vc = slice(GLA_DV * h, GLA_DV * (h + 1))
            q = qk_ref[rows, kc].astype(F32)
            k = qk_ref[rows, half + GLA_DK * h:half + GLA_DK * (h + 1)].astype(F32)
            o = _gla_chunk(q, k, v_ref[rows, vc], la_ref[rows, kc], st_ref, h, tri, sub)
            y_ref[rows, vc] = _gla_out(o, g_ref[:, vc], gg_ref[rows, vc].astype(F32)).astype(y_ref.dtype)

    @pl.when(i == pl.num_programs(0) - 1)
    def _():
        for h in range(GLA_HEADS):
            s_ref[h] = st_ref[h].T


def _gla_prompt(gla_in, log_a, norm_g, *, tb, chunk, sub):
    t = gla_in.shape[0]
    return pl.pallas_call(
        functools.partial(_gla_prompt_kernel, chunk=chunk, sub=sub),
        out_shape=(
            jax.ShapeDtypeStruct((t, GLA_WIDTH), BF16),
            jax.ShapeDtypeStruct((GLA_HEADS, GLA_DK, GLA_DV), F32),
        ),
        grid=(t // tb,),
        in_specs=[
            pl.BlockSpec((tb, 1024), lambda i: (i, 0)),
            pl.BlockSpec((tb, 1024), lambda i: (i, 1)),
            pl.BlockSpec((tb, 1024), lambda i: (i, 2)),
            pl.BlockSpec((tb, GLA_HEADS * GLA_DK), lambda i: (i, 0)),
            pl.BlockSpec((1, GLA_WIDTH), lambda i: (0, 0)),
        ],
        out_specs=(
            pl.BlockSpec((tb, GLA_WIDTH), lambda i: (i, 0)),
            pl.BlockSpec((GLA_HEADS, GLA_DK, GLA_DV), lambda i: (0, 0, 0)),
        ),
        scratch_shapes=[pltpu.VMEM((GLA_HEADS, GLA_DV, GLA_DK), F32)],
        compiler_params=_cparams(("arbitrary",)),
        name="gla_prompt",
    )(gla_in, gla_in, gla_in, log_a, norm_g)


def _gla_sample_kernel(qk_ref, v_ref, gg_ref, la_ref, g_ref, s0_ref, y_ref, s_ref, st_ref):
    steps = qk_ref.shape[0]
    c = 2 * steps
    tri = _tri(c)
    half = GLA_HEADS * GLA_DK
    pad = lambda a: jnp.concatenate([a, jnp.zeros_like(a)], axis=0)
    for h in range(GLA_HEADS):
        st_ref[h] = s0_ref[0, h].T
    for h in range(GLA_HEADS):
        kc = slice(GLA_DK * h, GLA_DK * (h + 1))
        vc = slice(GLA_DV * h, GLA_DV * (h + 1))
        q = pad(qk_ref[:, kc])
        k = pad(qk_ref[:, half + GLA_DK * h:half + GLA_DK * (h + 1)])
        v = pad(v_ref[:, vc]).astype(BF16)
        o = _gla_chunk(q, k, v, pad(la_ref[:, kc]), st_ref, h, tri, c)
        y_ref[:, vc] = _gla_out(o[:steps], g_ref[:, vc], gg_ref[:, vc])
    for h in range(GLA_HEADS):
        s_ref[0, h] = st_ref[h].T


def _gla_sample(gla_in, log_a, norm_g, state, *, steps):
    t = gla_in.shape[0]
    nb = t // steps
    return pl.pallas_call(
        _gla_sample_kernel,
        out_shape=(
            jax.ShapeDtypeStruct((t, GLA_WIDTH), F32),
            jax.ShapeDtypeStruct(state.shape, F32),
        ),
        grid=(nb,),
        in_specs=[
            pl.BlockSpec((steps, 1024), lambda n: (n, 0)),
            pl.BlockSpec((steps, 1024), lambda n: (n, 1)),
            pl.BlockSpec((steps, 1024), lambda n: (n, 2)),
            pl.BlockSpec((steps, GLA_HEADS * GLA_DK), lambda n: (n, 0)),
            pl.BlockSpec((1, GLA_WIDTH), lambda n: (0, 0)),
            pl.BlockSpec((1, GLA_HEADS, GLA_DK, GLA_DV), lambda n: (n, 0, 0, 0)),
        ],
        out_specs=(
            pl.BlockSpec((steps, GLA_WIDTH), lambda n: (n, 0)),
            pl.BlockSpec((1, GLA_HEADS, GLA_DK, GLA_DV), lambda n: (n, 0, 0, 0)),
        ),
        scratch_shapes=[pltpu.VMEM((GLA_HEADS, GLA_DV, GLA_DK), F32)],
        compiler_params=_cparams(("arbitrary",)),
        name="gla_sample",
    )(gla_in, gla_in, gla_in, log_a, norm_g, state)


def _multiplicity(delta):
    total = jnp.zeros(delta.shape, F32)
    for d in DILATIONS:
        assert d & (d - 1) == 0
        hit = jnp.where(jnp.bitwise_and(delta, d - 1) == 0, 1.0, 0.0)
        total = total + jnp.where(delta <= BAND * d, hit, 0.0)
    return jnp.where(delta >= 0, total, 0.0)


def _sample_attn_kernel(q_ref, kn_ref, vn_ref, ckt_ref, cvt_ref, o_ref):
    steps = q_ref.shape[0]
    rows = ATT_HEADS * steps
    cache_len = ckt_ref.shape[2]
    qt = jnp.concatenate([q_ref[...]] * ATT_HEADS, axis=0)
    rh = lax.broadcasted_iota(jnp.int32, (rows, ATT_WIDTH), 0) // steps
    ch = lax.broadcasted_iota(jnp.int32, (rows, ATT_WIDTH), 1) // ATT_HEAD_DIM
    own = rh == ch
    qb = jnp.where(own, qt, 0.0).astype(BF16)
    pad = jnp.zeros((LANES - steps, ATT_WIDTH), F32)
    kn = jnp.concatenate([kn_ref[...], pad], axis=0).astype(BF16)
    vn = jnp.concatenate([vn_ref[...], pad], axis=0).astype(BF16)
    assert steps & (steps - 1) == 0
    t_c = jnp.bitwise_and(lax.broadcasted_iota(jnp.int32, (rows, cache_len), 0), steps - 1)
    mc = _multiplicity(cache_len + t_c - lax.broadcasted_iota(jnp.int32, (rows, cache_len), 1))
    t_n = jnp.bitwise_and(lax.broadcasted_iota(jnp.int32, (rows, LANES), 0), steps - 1)
    new = lax.broadcasted_iota(jnp.int32, (rows, LANES), 1)
    mn = jnp.where(new < steps, _multiplicity(t_n - new), 0.0)
    sc = jnp.where(mc > 0.0, _dot(qb, ckt_ref[0].astype(BF16)), NEG)
    sn = jnp.where(mn > 0.0, _dot_nt(qb, kn), NEG)
    m = jnp.maximum(jnp.max(sc, axis=-1, keepdims=True), jnp.max(sn, axis=-1, keepdims=True))
    pc = mc * jnp.exp(sc - m)
    pn = mn * jnp.exp(sn - m)
    den = jnp.sum(pc, axis=-1, keepdims=True) + jnp.sum(pn, axis=-1, keepdims=True)
    o = _dot_nt(pc.astype(BF16), cvt_ref[0].astype(BF16)) + _dot(pn.astype(BF16), vn)
    o = jnp.where(own, o / den, 0.0)
    o_ref[...] = jnp.sum(o.reshape(ATT_HEADS, steps, ATT_WIDTH), axis=0)


def _sample_attention(q, k_new, v_new, cache_kt, cache_vt, *, steps):
    nb, _, cache_len = cache_kt.shape
    tok = pl.BlockSpec((steps, ATT_WIDTH), lambda n: (n, 0))
    cache = pl.BlockSpec((1, ATT_WIDTH, cache_len), lambda n: (n, 0, 0))
    return pl.pallas_call(
        _sample_attn_kernel,
        out_shape=jax.ShapeDtypeStruct((nb * steps, ATT_WIDTH), F32),
        grid=(nb,),
        in_specs=[tok, tok, tok, cache, cache],
        out_specs=tok,
        compiler_params=_cparams(("arbitrary",)),
        name="sample_attn",
    )(q, k_new, v_new, cache_kt, cache_vt)


def _outproj_kernel(att_ref, gla_ref, x_ref, gate_ref, ng_ref, w_ref, h_ref):
    att = jnp.concatenate([att_ref[hp].astype(F32) for hp in range(HEAD_PAIRS)], axis=1)
    rs = lax.rsqrt(jnp.mean(att * att, axis=-1, keepdims=True) + EPS)
    att_y = (att * rs * ng_ref[...]).astype(BF16)
    mix = _dot(att_y, w_ref[:ATT_WIDTH, :]) + _dot(gla_ref[...], w_ref[ATT_WIDTH:, :])
    h_ref[...] = x_ref[...] + gate_ref[...] * mix


def _outproj(att, gla_y, x, gate, att_norm_g, w_out, *, tm):
    t = x.shape[0]
    mod_rows = gate.shape[0]
    mod_map = (lambda i: (0, 0)) if mod_rows == 1 else (lambda i: (i, 0))
    mod_block = (1, D_MODEL) if mod_rows == 1 else (tm, D_MODEL)
    return pl.pallas_call(
        _outproj_kernel,
        out_shape=jax.ShapeDtypeStruct((t, D_MODEL), F32),
        grid=(t // tm,),
        in_specs=[
            pl.BlockSpec((HEAD_PAIRS, tm, LANES), lambda i: (0, i, 0)),
            pl.BlockSpec((tm, GLA_WIDTH), lambda i: (i, 0)),
            pl.BlockSpec((tm, D_MODEL), lambda i: (i, 0)),
            pl.BlockSpec(mod_block, mod_map),
            pl.BlockSpec((1, ATT_WIDTH), lambda i: (0, 0)),
            pl.BlockSpec((ATT_WIDTH + GLA_WIDTH, D_MODEL), lambda i: (0, 0)),
        ],
        out_specs=pl.BlockSpec((tm, D_MODEL), lambda i: (i, 0)),
        compiler_params=_cparams(("arbitrary",)),
        name="outproj",
    )(att, gla_y, x, gate, att_norm_g, w_out)


def _ffn_kernel(h_ref, g_ref, sc_ref, sh_ref, gate_ref, fg_ref, wu_ref, wd_ref, y_ref, hn_ref):
    f = pl.program_id(1)

    @pl.when(f == 0)
    def _():
        hn_ref[...] = _norm_mod(h_ref[...], g_ref[...], sc_ref[...], sh_ref[...]).astype(BF16)
        y_ref[...] = jnp.zeros_like(y_ref)

    u = jnp.maximum(_dot(hn_ref[...], wu_ref[...]), 0.0)
    y_ref[...] += _dot((u * u).astype(BF16), wd_ref[...])

    @pl.when(f == pl.num_programs(1) - 1)
    def _():
        h2 = h_ref[...] + gate_ref[...] * y_ref[...]
        rs = lax.rsqrt(jnp.mean(h2 * h2, axis=-1, keepdims=True) + EPS)
        y_ref[...] = h2 * rs * fg_ref[...]


def _ffn(h, g, sc, sh, gate, final_g, w_up, w_down, *, tm, tf):
    t = h.shape[0]
    const = pl.BlockSpec((1, D_MODEL), lambda i, f: (0, 0))
    mod = _mod_spec(sc.shape[0], tm)
    return pl.pallas_call(
        _ffn_kernel,
        out_shape=jax.ShapeDtypeStruct((t, D_MODEL), F32),
        grid=(t // tm, D_FF // tf),
        in_specs=[
            pl.BlockSpec((tm, D_MODEL), lambda i, f: (i, 0)),
            const, mod, mod, mod, const,
            pl.BlockSpec((D_MODEL, tf), lambda i, f: (0, f)),
            pl.BlockSpec((tf, D_MODEL), lambda i, f: (f, 0)),
        ],
        out_specs=pl.BlockSpec((tm, D_MODEL), lambda i, f: (i, 0)),
        scratch_shapes=[pltpu.VMEM((tm, D_MODEL), BF16)],
        compiler_params=_cparams(("arbitrary", "arbitrary")),
        name="ffn",
    )(h, g, sc, sh, gate, final_g, w_up, w_down)


def _split_ada(ada):
    return [ada[:, D_MODEL * i:D_MODEL * (i + 1)] for i in range(6)]


def kernel(x_prompt, x_sample, c_prompt, c_sample, cache_win_k, cache_win_v, state_gla, w_in, gla_gate_w, gla_gate_b, att_norm_g, gla_norm_g, w_out, norm1_g, norm2_g, w_ada, b_ada, w_up, w_down, final_g):
    assert w_in.shape[0] == 1, "single-layer model"
    n_p, t_p, _ = x_prompt.shape
    n_s, t_s, _ = x_sample.shape
    assert n_p == 1
    rows_s = n_s * t_s
    wb_p = min(MAX_WINDOW, t_p)

    n_main = 6 * 1024
    w_main = w_in[0, :, :n_main].astype(BF16)
    w_ag = jnp.pad(w_in[0, :, n_main:], ((0, 0), (0, LANES - GLA_GATE_RANK))).astype(BF16)
    gate_w = jnp.pad(gla_gate_w[0], ((0, LANES - GLA_GATE_RANK), (0, 0))).astype(BF16)
    gate_b = gla_gate_b[0][None, :]
    w_out_b = w_out[0].astype(BF16)
    w_up_b = w_up[0].astype(BF16)
    w_down_b = w_down[0].astype(BF16)
    n1g, n2g = norm1_g[0][None, :], norm2_g[0][None, :]
    ang, gng = att_norm_g[0][None, :], gla_norm_g[0][None, :]
    fg = final_g[None, :]

    c_rows = n_p + n_s
    c_pad = -c_rows % 8
    c_all = jnp.concatenate([c_prompt, c_sample, jnp.zeros((c_pad, D_MODEL), F32)], axis=0)
    ada = _ada(c_all, w_ada[0], b_ada[0][None, :])
    sh1_p, sc1_p, g1_p, sh2_p, sc2_p, g2_p = _split_ada(ada[:n_p])
    sh1_s, sc1_s, g1_s, sh2_s, sc2_s, g2_s = _split_ada(
        jnp.repeat(ada[n_p:c_rows], t_s, axis=0))

    xp = x_prompt[0]
    cs_p, sn_p = _rope_tables(jnp.arange(t_p))
    outs = _inproj(xp, n1g, sc1_p, sh1_p, w_main, w_ag, gate_w, gate_b, cs_p, sn_p,
                   tm=512, emit_att=True)
    att = _prompt_attention(outs[:9])
    gla_in, log_a = outs[9], outs[10]
    win = _qkv_rows(xp, n1g, sc1_p, sh1_p, w_main, cs_p, sn_p,
                    tm=512, first_row=t_p - wb_p, rows=wb_p, col0=1)
    win_k, win_v = win[0], win[1]
    gla_y, st_p = _gla_prompt(gla_in, log_a, gng, tb=512, chunk=64, sub=16)
    h_p = _outproj(att, gla_y, xp, g1_p, ang, w_out_b, tm=512)
    y_p = _ffn(h_p, n2g, sc2_p, sh2_p, g2_p, fg, w_up_b, w_down_b, tm=1024, tf=512)

    xs = x_sample.reshape(rows_s, D_MODEL)
    cs_s, sn_s = _rope_tables(jnp.tile(PAST_LEN + jnp.arange(t_s), n_s))
    gla_in_s, log_a_s = _inproj(xs, n1g, sc1_s, sh1_s, w_main, w_ag, gate_w, gate_b, cs_s, sn_s,
                                tm=rows_s, emit_att=False)
    qkv_s = _qkv_rows(xs, n1g, sc1_s, sh1_s, w_main, cs_s, sn_s,
                      tm=rows_s, first_row=0, rows=rows_s, col0=0)
    new_k, new_v = qkv_s[1], qkv_s[2]
    wc = cache_win_k.shape[2]
    cache_t = lambda c: jnp.transpose(c[0], (0, 2, 3, 1)).reshape(n_s, ATT_WIDTH, wc)
    att_tok = _sample_attention(qkv_s[0], new_k, new_v, cache_t(cache_win_k), cache_t(cache_win_v),
                                steps=t_s)
    att_s = jnp.transpose(att_tok.reshape(rows_s, HEAD_PAIRS, LANES), (1, 0, 2)).astype(BF16)
    gla_y_s, st_s = _gla_sample(gla_in_s.astype(F32), log_a_s, gng, state_gla[0], steps=t_s)
    h_s = _outproj(att_s, gla_y_s.astype(BF16), xs, g1_s, ang, w_out_b, tm=rows_s)
    y_s = _ffn(h_s, n2g, sc2_s, sh2_s, g2_s, fg, w_up_b, w_down_b, tm=rows_s, tf=512)

    return (
        y_p[None],
        y_s.reshape(n_s, t_s, D_MODEL),
        win_k.reshape(1, n_p, wb_p, ATT_HEADS, ATT_HEAD_DIM),
        win_v.reshape(1, n_p, wb_p, ATT_HEADS, ATT_HEAD_DIM),
        st_p[None, None],
        new_k.reshape(1, n_s, t_s, ATT_HEADS, ATT_HEAD_DIM),
        new_v.reshape(1, n_s, t_s, ATT_HEADS, ATT_HEAD_DIM),
        st_s[None],
    )
```

```python
import functools

import jax
import jax.numpy as jnp
from jax import lax
from jax.experimental import pallas as pl
from jax.experimental.pallas import tpu as pltpu

F32 = jnp.float32
BF16 = jnp.bfloat16

D_MODEL = 2048
ATT_WIDTH = 1024
ATT_HEADS = 16
ATT_HEAD_DIM = 64
ROT_DIM = 16
ROPE_THETA = 500000.0
DILATIONS = (1, 4, 16)
BAND = 128
MAX_WINDOW = 2048
GLA_HEADS = 4
GLA_DK = 128
GLA_DV = 256
GLA_WIDTH = 1024
GLA_GATE_RANK = 16
GLA_TAU = 16.0
D_FF = 8192
EPS = 1e-6
PAST_LEN = 16384
NEG = -1e30

LANES = 128
HEAD_PAIRS = ATT_WIDTH // LANES
VMEM_LIMIT = 56 * 1024 * 1024


def _cparams(sem):
    return pltpu.CompilerParams(dimension_semantics=sem, vmem_limit_bytes=VMEM_LIMIT)


def _dot(a, b):
    return jnp.dot(a, b, preferred_element_type=F32)


def _dot_nt(a, b):
    return lax.dot_general(a, b, (((1,), (1,)), ((), ())), preferred_element_type=F32)


def _dot_tn(a, b):
    return lax.dot_general(a, b, (((0,), (0,)), ((), ())), preferred_element_type=F32)


def _silu(x):
    return x / (1.0 + jnp.exp(-x))


def _ada_kernel(c_ref, w_ref, b_ref, o_ref):
    s = _silu(c_ref[...]).astype(BF16)
    o_ref[...] = _dot(s, w_ref[...].astype(BF16)) + b_ref[...]


def _ada(c, w_ada, b_ada):
    rows = c.shape[0]
    n = w_ada.shape[1]
    tn = 1024
    return pl.pallas_call(
        _ada_kernel,
        out_shape=jax.ShapeDtypeStruct((rows, n), F32),
        grid=(n // tn,),
        in_specs=[
            pl.BlockSpec((rows, D_MODEL), lambda j: (0, 0)),
            pl.BlockSpec((D_MODEL, tn), lambda j: (0, j)),
            pl.BlockSpec((1, tn), lambda j: (0, j)),
        ],
        out_specs=pl.BlockSpec((rows, tn), lambda j: (0, j)),
        compiler_params=_cparams(("arbitrary",)),
        name="ada",
    )(c, w_ada, b_ada)


def _rope(slab, cs, sn):
    d = lax.broadcasted_iota(jnp.int32, (1, LANES), 1) % ATT_HEAD_DIM
    half = ROT_DIM // 2
    from_hi = jnp.where(d < half, -1.0, 0.0)
    from_lo = jnp.where((d >= half) & (d < ROT_DIM), 1.0, 0.0)
    return (slab * cs + pltpu.roll(slab, LANES - half, 1) * (sn * from_hi)
            + pltpu.roll(slab, half, 1) * (sn * from_lo))


def _norm_mod(x, g, sc, sh):
    rs = lax.rsqrt(jnp.mean(x * x, axis=-1, keepdims=True) + EPS)
    return x * rs * g * (1.0 + sc) + sh


def _inproj_kernel(*refs, emit_att):
    (x_ref, g_ref, sc_ref, sh_ref, w_ref, wag_ref, gw_ref, gb_ref, cs_ref, sn_ref), refs = (
        refs[:10], refs[10:])
    if emit_att:
        att_refs, refs = refs[:9], refs[9:]
    gla_ref, la_ref, xn_ref = refs[:3]
    j = pl.program_id(1)
    col = j if emit_att else j + 3

    @pl.when(j == 0)
    def _():
        xnb = _norm_mod(x_ref[...], g_ref[...], sc_ref[...], sh_ref[...]).astype(BF16)
        xn_ref[...] = xnb
        ag = _dot_nt(xnb, wag_ref[...])
        z = _dot(ag.astype(BF16), gw_ref[...]) + gb_ref[...]
        log_sig = jnp.minimum(z, 0.0) - jnp.log(1.0 + jnp.exp(-jnp.abs(z)))
        la_ref[...] = log_sig / GLA_TAU

    if emit_att:
        s0_ref, s1_ref = refs[3:5]
        tm = x_ref.shape[0]
        n4, n16 = tm // 4, tm // 16
        group = 2 * LANES

        def emit(out1, out4, out16, slab_fn):
            for hp in range(HEAD_PAIRS):
                if hp % 2 == 0:
                    acc = _dot_nt(xn_ref[...], w_ref[LANES * hp:LANES * hp + group, :])
                slab = slab_fn(acc[:, LANES * (hp % 2):LANES * (hp % 2 + 1)])
                out1[hp] = slab.astype(BF16)
                s0_ref[hp] = slab
                for r in range(4):
                    part = s0_ref[hp, pl.ds(r, n4, stride=4), :]
                    out4[hp, :, LANES * r:LANES * (r + 1)] = part.astype(BF16)
                    s1_ref[hp, n4 * r:n4 * (r + 1), :] = part
                for r in range(16):
                    part = s1_ref[hp, pl.ds(n4 * (r % 4) + r // 4, n16, stride=4), :]
                    out16[hp, :, LANES * r:LANES * (r + 1)] = part.astype(BF16)

        @pl.when(col == 0)
        def _():
            emit(*att_refs[0:3],
                 lambda s: _rope(s, cs_ref[...], sn_ref[...]) * (ATT_HEAD_DIM ** -0.5))

        @pl.when(col == 1)
        def _():
            emit(*att_refs[3:6], lambda s: _rope(s, cs_ref[...], sn_ref[...]))

        @pl.when(col == 2)
        def _():
            emit(*att_refs[6:9], lambda s: s)

    @pl.when(col == 3)
    def _():
        acc = _dot_nt(xn_ref[...], w_ref[...])
        half = GLA_HEADS * GLA_DK
        gla_ref[:, :half] = (acc[:, :half] * (GLA_DK ** -0.5)).astype(BF16)
        gla_ref[:, half:] = acc[:, half:].astype(BF16)

    @pl.when(col >= 4)
    def _():
        gla_ref[...] = _dot_nt(xn_ref[...], w_ref[...]).astype(BF16)


def _mod_spec(mod_rows, tm):
    if mod_rows == 1:
        return pl.BlockSpec((1, D_MODEL), lambda i, j: (0, 0))
    return pl.BlockSpec((tm, D_MODEL), lambda i, j: (i, 0))


def _inproj(x, g, sc, sh, w_main, w_ag, gate_w, gate_b, cs, sn, *, tm, emit_att):
    t = x.shape[0]
    col0 = 0 if emit_att else 3
    mod = _mod_spec(sc.shape[0], tm)
    out_shape, out_specs, scratch = [], [], [pltpu.VMEM((tm, D_MODEL), BF16)]
    if emit_att:
        for _ in range(3):
            for d in DILATIONS:
                out_shape.append(jax.ShapeDtypeStruct((HEAD_PAIRS, t // d, d * LANES), BF16))
                out_specs.append(pl.BlockSpec((HEAD_PAIRS, tm // d, d * LANES),
                                              lambda i, j: (0, i, 0)))
        scratch += [pltpu.VMEM((HEAD_PAIRS, tm, LANES), F32) for _ in range(2)]
    out_shape += [jax.ShapeDtypeStruct((t, 3 * 1024), BF16),
                  jax.ShapeDtypeStruct((t, GLA_HEADS * GLA_DK), F32)]
    out_specs += [pl.BlockSpec((tm, 1024), lambda i, j: (i, jnp.maximum(j + col0 - 3, 0))),
                  pl.BlockSpec((tm, GLA_HEADS * GLA_DK), lambda i, j: (i, 0))]
    return pl.pallas_call(
        functools.partial(_inproj_kernel, emit_att=emit_att),
        out_shape=tuple(out_shape),
        grid=(t // tm, 6 - col0),
        in_specs=[
            pl.BlockSpec((tm, D_MODEL), lambda i, j: (i, 0)),
            pl.BlockSpec((1, D_MODEL), lambda i, j: (0, 0)),
            mod, mod,
            pl.BlockSpec((1024, D_MODEL), lambda i, j: (j + col0, 0)),
            pl.BlockSpec((LANES, D_MODEL), lambda i, j: (0, 0)),
            pl.BlockSpec((LANES, GLA_HEADS * GLA_DK), lambda i, j: (0, 0)),
            pl.BlockSpec((1, GLA_HEADS * GLA_DK), lambda i, j: (0, 0)),
            pl.BlockSpec((tm, LANES), lambda i, j: (i, 0)),
            pl.BlockSpec((tm, LANES), lambda i, j: (i, 0)),
        ],
        out_specs=tuple(out_specs),
        scratch_shapes=scratch,
        compiler_params=_cparams(("arbitrary", "arbitrary")),
        name="inproj_att" if emit_att else "inproj_gla",
    )(x, g, sc, sh, w_main, w_ag, gate_w, gate_b, cs, sn)


def _qkv_rows_kernel(x_ref, g_ref, sc_ref, sh_ref, w_ref, cs_ref, sn_ref, o_ref, xn_ref, *, col0):
    j = pl.program_id(1)

    @pl.when(j == 0)
    def _():
        xn_ref[...] = _norm_mod(x_ref[...], g_ref[...], sc_ref[...], sh_ref[...]).astype(BF16)

    acc = _dot_nt(xn_ref[...], w_ref[...])
    col = j + col0

    @pl.when(col <= 1)
    def _():
        scale = jnp.where(col == 0, ATT_HEAD_DIM ** -0.5, 1.0)
        for hp in range(HEAD_PAIRS):
            lanes = slice(LANES * hp, LANES * (hp + 1))
            o_ref[0, :, lanes] = _rope(acc[:, lanes], cs_ref[...], sn_ref[...]) * scale

    @pl.when(col == 2)
    def _():
        o_ref[0] = acc


def _qkv_rows(x, g, sc, sh, w_main, cs, sn, *, tm, first_row, rows, col0):
    first = first_row // tm
    assert sc.shape[0] == 1 or first == 0
    ncols = 3 - col0
    mod = _mod_spec(sc.shape[0], tm)
    return pl.pallas_call(
        functools.partial(_qkv_rows_kernel, col0=col0),
        out_shape=jax.ShapeDtypeStruct((ncols, rows, ATT_WIDTH), F32),
        grid=(rows // tm, ncols),
        in_specs=[
            pl.BlockSpec((tm, D_MODEL), lambda i, j: (i + first, 0)),
            pl.BlockSpec((1, D_MODEL), lambda i, j: (0, 0)),
            mod, mod,
            pl.BlockSpec((1024, D_MODEL), lambda i, j: (j + col0, 0)),
            pl.BlockSpec((tm, LANES), lambda i, j: (i + first, 0)),
            pl.BlockSpec((tm, LANES), lambda i, j: (i + first, 0)),
        ],
        out_specs=pl.BlockSpec((1, tm, ATT_WIDTH), lambda i, j: (j, i, 0)),
        scratch_shapes=[pltpu.VMEM((tm, D_MODEL), BF16)],
        compiler_params=_cparams(("arbitrary", "arbitrary")),
        name="qkv_rows",
    )(x, g, sc, sh, w_main, cs, sn)


def _rope_tables(pos):
    half = ROT_DIM // 2
    inv_freq = ROPE_THETA ** (-jnp.arange(half, dtype=F32) / half)
    per_head = jnp.concatenate([inv_freq, inv_freq, jnp.zeros((ATT_HEAD_DIM - ROT_DIM,), F32)])
    ang = pos.astype(F32)[:, None] * jnp.tile(per_head, LANES // ATT_HEAD_DIM)[None, :]
    return jnp.cos(ang), jnp.sin(ang)


def _attn_kernel(q1, k1c, v1c, k1p, v1p, q4, k4c, v4c, k4p, v4p, q16, k16c, v16c, k16p, v16p,
                 o_ref, a1, m1, l1, a4, m4, l4, t16, s_scr, p_scr, bias_ref, o_scr):
    first_block = pl.program_id(1) == 0
    lane = lax.broadcasted_iota(jnp.int32, (1, LANES), 1)
    first = lane < ATT_HEAD_DIM
    head0 = jnp.where(first, 1.0, 0.0).astype(BF16)
    head1 = jnp.where(first, 0.0, 1.0).astype(BF16)
    ones = jnp.ones((2 * BAND, LANES), BF16)

    qi = lax.broadcasted_iota(jnp.int32, (2 * BAND, 2 * BAND), 0) % BAND
    kk = lax.broadcasted_iota(jnp.int32, (2 * BAND, 2 * BAND), 1)
    full = jnp.where(jnp.where(kk < BAND, kk - qi, qi - (kk - BAND)) >= 0, 0.0, NEG)
    bias_ref[0] = full
    bias_ref[1] = jnp.where(kk < BAND, full + jnp.where(first_block, NEG, 0.0), full)

    n1 = q1.shape[1] // BAND
    rows4 = q4.shape[1]
    n4 = rows4 // BAND

    def window(cur, prev, i, cols):
        if i == 0:
            return jnp.concatenate([prev[0, :, cols], cur[0, :BAND, cols]], axis=0)
        return cur[0, BAND * (i - 1):BAND * (i + 1), cols]

    jobs = []
    everything = slice(None)
    for i in range(n1):
        jobs.append(dict(q=(q1, i, everything), k=(k1c, k1p), v=(v1c, v1p), bias=int(i == 0),
                         dst=(a1, m1, l1, BAND * i)))
    for r in range(4):
        cols = slice(LANES * r, LANES * (r + 1))
        for i in range(n4):
            jobs.append(dict(q=(q4, i, cols), k=(k4c, k4p), v=(v4c, v4p), bias=int(i == 0),
                             dst=(a4, m4, l4, rows4 * r + BAND * i)))
    for r in range(16):
        cols = slice(LANES * r, LANES * (r + 1))
        jobs.append(dict(q=(q16, 0, cols), k=(k16c, k16p), v=(v16c, v16p), bias=1, merge=r))

    def scores(job, slot):
        ref, i, cols = job["q"]
        q = ref[0, BAND * i:BAND * (i + 1), cols]
        q2 = jnp.concatenate([q * head0, q * head1], axis=0)
        s_scr[slot] = _dot_nt(q2, window(*job["k"], i, cols))

    def softmax(job, slot):
        s = s_scr[slot] + bias_ref[job["bias"]]
        m = jnp.max(s, axis=-1, keepdims=True)
        p_scr[slot] = jnp.exp(s - m).astype(BF16)
        m = jnp.where(first, m[:BAND], m[BAND:])
        if "merge" in job:
            t16[slot] = m
        else:
            job["dst"][1][pl.ds(job["dst"][3], BAND), :] = m

    def values(job, slot):
        _, i, cols = job["q"]
        vw = jnp.concatenate([window(*job["v"], i, cols), ones], axis=1)
        o2 = _dot(p_scr[slot], vw)
        acc = jnp.where(first, o2[:BAND, :LANES], o2[BAND:, :LANES])
        den = jnp.where(first, o2[:BAND, LANES:], o2[BAND:, LANES:])
        if "merge" not in job:
            a_ref, _, l_ref, row = job["dst"]
            a_ref[pl.ds(row, BAND), :] = acc
            l_ref[pl.ds(row, BAND), :] = den
            return
        r = job["merge"]
        sl1 = pl.ds(r, BAND, stride=16)
        sl4 = pl.ds((r % 4) * rows4 + r // 4, BAND, stride=4)
        ma, mb, mc = m1[sl1, :], m4[sl4, :], t16[slot]
        m = jnp.maximum(jnp.maximum(ma, mb), mc)
        wa, wb, wc = jnp.exp(ma - m), jnp.exp(mb - m), jnp.exp(mc - m)
        num = wa * a1[sl1, :] + wb * a4[sl4, :] + wc * acc
        dsum = wa * l1[sl1, :] + wb * l4[sl4, :] + wc * den
        o_scr[sl1, :] = num / dsum

    slots = s_scr.shape[0]
    for step in range(len(jobs) + 2):
        if step < len(jobs):
            scores(jobs[step], step % slots)
        if 1 <= step <= len(jobs):
            softmax(jobs[step - 1], (step - 1) % slots)
        if step >= 2:
            values(jobs[step - 2], (step - 2) % slots)
    o_ref[0] = o_scr[...].astype(o_ref.dtype)


def _prompt_attention(views):
    hp, t, _ = views[0].shape
    sb = 16 * BAND
    in_specs, args = [], []
    for di, d in enumerate(DILATIONS):
        rows, width = sb // d, d * LANES
        cur = pl.BlockSpec((1, rows, width), lambda h, s: (h, s, 0))
        nb = rows // BAND
        prev = pl.BlockSpec((1, BAND, width),
                            lambda h, s, nb=nb: (h, jnp.maximum(s * nb - 1, 0), 0))
        q, k, v = views[di], views[3 + di], views[6 + di]
        in_specs += [cur, cur, cur, prev, prev]
        args += [q, k, v, k, v]
    tile = (2 * BAND, 2 * BAND)
    return pl.pallas_call(
        _attn_kernel,
        out_shape=jax.ShapeDtypeStruct((hp, t, LANES), BF16),
        grid=(hp, t // sb),
        in_specs=in_specs,
        out_specs=pl.BlockSpec((1, sb, LANES), lambda h, s: (h, s, 0)),
        scratch_shapes=(
            [pltpu.VMEM((sb, LANES), F32) for _ in range(6)]
            + [pltpu.VMEM((3, BAND, LANES), F32),
               pltpu.VMEM((3,) + tile, F32),
               pltpu.VMEM((3,) + tile, BF16),
               pltpu.VMEM((2,) + tile, F32),
               pltpu.VMEM((sb, LANES), F32)]),
        compiler_params=_cparams(("arbitrary", "arbitrary")),
        name="prompt_attn",
    )(*args)


def _gla_chunk(q, k, v, la, st_ref, h, tri, sub):
    c = q.shape[0]
    la_hi = la.astype(BF16)
    la_lo = (la - la_hi.astype(F32)).astype(BF16)
    b = _dot(tri, la_hi) + _dot(tri, la_lo)
    st = st_ref[h]
    o = _dot_nt((q * jnp.exp(b)).astype(BF16), st.astype(BF16))
    srow = lax.broadcasted_iota(jnp.int32, (c, GLA_DK), 0)
    att_rows = []
    for i in range(c // sub):
        lo, hi = i * sub, (i + 1) * sub
        ref = b[lo:lo + 1, :]
        qh = (q[lo:hi] * jnp.exp(b[lo:hi] - ref)).astype(BF16)
        kh = (k * jnp.exp(jnp.where(srow < hi, ref - b, 0.0))).astype(BF16)
        a = _dot_nt(qh, kh)
        ti = lax.broadcasted_iota(jnp.int32, (sub, c), 0) + lo
        si = lax.broadcasted_iota(jnp.int32, (sub, c), 1)
        att_rows.append(jnp.where(si <= ti, a, 0.0))
    att = att_rows[0] if len(att_rows) == 1 else jnp.concatenate(att_rows, axis=0)
    o = o + _dot(att.astype(BF16), v)
    b_last = b[c - 1:c, :]
    kbar = (k * jnp.exp(b_last - b)).astype(BF16)
    st_ref[h] = st * jnp.exp(b_last) + _dot_tn(v, kbar)
    return o


def _gla_out(o, g, gg):
    rs = lax.rsqrt(jnp.mean(o * o, axis=-1, keepdims=True) + EPS)
    return o * rs * g * _silu(gg)


def _tri(c):
    r = lax.broadcasted_iota(jnp.int32, (c, c), 0)
    s = lax.broadcasted_iota(jnp.int32, (c, c), 1)
    return jnp.where(s <= r, 1.0, 0.0).astype(BF16)


def _gla_prompt_kernel(qk_ref, v_ref, gg_ref, la_ref, g_ref, y_ref, s_ref, st_ref,
                       qd_ref, kl_ref, qs_ref, ks_ref, dec_ref, att_ref, o_ref, *, chunk, sub):
    i = pl.program_id(0)

    @pl.when(i == 0)
    def _():
        st_ref[...] = jnp.zeros_like(st_ref)

    tb = qk_ref.shape[0]
    nchunks, nsub = tb // chunk, chunk // sub
    half = GLA_HEADS * GLA_DK

    r = lax.broadcasted_iota(jnp.int32, (tb, tb), 0)
    c = lax.broadcasted_iota(jnp.int32, (tb, tb), 1)
    start = r - jnp.bitwise_and(r, chunk - 1)
    tri = jnp.where(c <= r, jnp.where(c >= start, 1.0, 0.0), 0.0).astype(BF16)
    la = la_ref[...]
    la_hi = la.astype(BF16)
    la_lo = (la - la_hi.astype(F32)).astype(BF16)
    b = _dot(tri, la_hi) + _dot(tri, la_lo)
    q = qk_ref[:, :half].astype(F32)
    k = qk_ref[:, half:].astype(F32)
    qd_ref[...] = (q * jnp.exp(b)).astype(BF16)
    srow = lax.broadcasted_iota(jnp.int32, (chunk, half), 0)
    for ci in range(nchunks):
        lo = ci * chunk
        bc, kc_ = b[lo:lo + chunk], k[lo:lo + chunk]
        b_last = bc[chunk - 1:chunk]
        dec_ref[ci:ci + 1, :] = jnp.exp(b_last)
        kl_ref[lo:lo + chunk, :] = (kc_ * jnp.exp(b_last - bc)).astype(BF16)
        for si in range(nsub):
            s_lo, s_hi = si * sub, (si + 1) * sub
            ref = bc[s_lo:s_lo + 1]
            qs_ref[lo + s_lo:lo + s_hi, :] = (
                q[lo + s_lo:lo + s_hi] * jnp.exp(bc[s_lo:s_hi] - ref)).astype(BF16)
            ks_ref[si, lo:lo + chunk, :] = (
                kc_ * jnp.exp(jnp.where(srow < s_hi, ref - bc, 0.0))).astype(BF16)

    ti = lax.broadcasted_iota(jnp.int32, (chunk, chunk), 0)
    si_ = lax.broadcasted_iota(jnp.int32, (chunk, chunk), 1)
    causal = si_ <= ti
    jobs = [(ci, h) for ci in range(nchunks) for h in range(GLA_HEADS)]

    def intra(job, slot):
        ci, h = job
        lo, kc = ci * chunk, slice(GLA_DK * h, GLA_DK * (h + 1))
        parts = [_dot_nt(qs_ref[lo + s * sub:lo + (s + 1) * sub, kc], ks_ref[s, lo:lo + chunk, kc])
                 for s in range(nsub)]
        att_ref[slot] = jnp.where(causal, jnp.concatenate(parts, axis=0), 0.0).astype(BF16)

    def state(job, slot):
        ci, h = job
        rows = slice(ci * chunk, (ci + 1) * chunk)
        kc, vc = slice(GLA_DK * h, GLA_DK * (h + 1)), slice(GLA_DV * h, GLA_DV * (h + 1))
        st = st_ref[h]
        v = v_ref[rows, vc]
        o_ref[slot] = _dot_nt(qd_ref[rows, kc], st.astype(BF16)) + _dot(att_ref[slot], v)
        st_ref[h] = st * dec_ref[ci:ci + 1, kc] + _dot_tn(v, kl_ref[rows, kc])

    def finish(job, slot):
        ci, h = job
        rows, vc = slice(ci * chunk, (ci + 1) * chunk), slice(GLA_DV * h, GLA_DV * (h + 1))
        y_ref[rows, vc] = _gla_out(o_ref[slot], g_ref[:, vc],
                                   gg_ref[rows, vc].astype(F32)).astype(y_ref.dtype)

    slots = att_ref.shape[0]
    for step in range(len(jobs) + 2):
        if step < len(jobs):
            intra(jobs[step], step % slots)
        if 1 <= step <= len(jobs):
            state(jobs[step - 1], (step - 1) % slots)
        if step >= 2:
            finish(jobs[step - 2], (step - 2) % slots)

    @pl.when(i == pl.num_programs(0) - 1)
    def _():
        for h in range(GLA_HEADS):
            s_ref[h] = st_ref[h].T


def _gla_prompt(gla_in, log_a, norm_g, *, tb, chunk, sub):
    t = gla_in.shape[0]
    return pl.pallas_call(
        functools.partial(_gla_prompt_kernel, chunk=chunk, sub=sub),
        out_shape=(
            jax.ShapeDtypeStruct((t, GLA_WIDTH), BF16),
            jax.ShapeDtypeStruct((GLA_HEADS, GLA_DK, GLA_DV), F32),
        ),
        grid=(t // tb,),
        in_specs=[
            pl.BlockSpec((tb, 1024), lambda i: (i, 0)),
            pl.BlockSpec((tb, 1024), lambda i: (i, 1)),
            pl.BlockSpec((tb, 1024), lambda i: (i, 2)),
            pl.BlockSpec((tb, GLA_HEADS * GLA_DK), lambda i: (i, 0)),
            pl.BlockSpec((1, GLA_WIDTH), lambda i: (0, 0)),
        ],
        out_specs=(
            pl.BlockSpec((tb, GLA_WIDTH), lambda i: (i, 0)),
            pl.BlockSpec((GLA_HEADS, GLA_DK, GLA_DV), lambda i: (0, 0, 0)),
        ),
        scratch_shapes=[
            pltpu.VMEM((GLA_HEADS, GLA_DV, GLA_DK), F32),
            pltpu.VMEM((tb, GLA_HEADS * GLA_DK), BF16),
            pltpu.VMEM((tb, GLA_HEADS * GLA_DK), BF16),
            pltpu.VMEM((tb, GLA_HEADS * GLA_DK), BF16),
            pltpu.VMEM((chunk // sub, tb, GLA_HEADS * GLA_DK), BF16),
            pltpu.VMEM((tb // chunk, GLA_HEADS * GLA_DK), F32),
            pltpu.VMEM((3, chunk, chunk), BF16),
            pltpu.VMEM((3, chunk, GLA_DV), F32),
        ],
        compiler_params=_cparams(("arbitrary",)),
        name="gla_prompt",
    )(gla_in, gla_in, gla_in, log_a, norm_g)


def _gla_sample_kernel(qk_ref, v_ref, gg_ref, la_ref, g_ref, s0_ref, y_ref, s_ref, st_ref, *, steps):
    group = s0_ref.shape[0]
    c = 2 * steps
    tri = _tri(c)
    half = GLA_HEADS * GLA_DK
    pad = lambda a: jnp.concatenate([a, jnp.zeros_like(a)], axis=0)
    for n in range(group):
        rows = slice(n * steps, (n + 1) * steps)
        for h in range(GLA_HEADS):
            idx = n * GLA_HEADS + h
            kc = slice(GLA_DK * h, GLA_DK * (h + 1))
            vc = slice(GLA_DV * h, GLA_DV * (h + 1))
            st_ref[idx] = s0_ref[n, h].T
            q = pad(qk_ref[rows, kc])
            k = pad(qk_ref[rows, half + GLA_DK * h:half + GLA_DK * (h + 1)])
            v = pad(v_ref[rows, vc]).astype(BF16)
            o = _gla_chunk(q, k, v, pad(la_ref[rows, kc]), st_ref, idx, tri, c)
            y_ref[rows, vc] = _gla_out(o[:steps], g_ref[:, vc], gg_ref[rows, vc])
            s_ref[n, h] = st_ref[idx].T


def _gla_sample(gla_in, log_a, norm_g, state, *, steps, group):
    t = gla_in.shape[0]
    rows = steps * group
    return pl.pallas_call(
        functools.partial(_gla_sample_kernel, steps=steps),
        out_shape=(
            jax.ShapeDtypeStruct((t, GLA_WIDTH), F32),
            jax.ShapeDtypeStruct(state.shape, F32),
        ),
        grid=(t // rows,),
        in_specs=[
            pl.BlockSpec((rows, 1024), lambda n: (n, 0)),
            pl.BlockSpec((rows, 1024), lambda n: (n, 1)),
            pl.BlockSpec((rows, 1024), lambda n: (n, 2)),
            pl.BlockSpec((rows, GLA_HEADS * GLA_DK), lambda n: (n, 0)),
            pl.BlockSpec((1, GLA_WIDTH), lambda n: (0, 0)),
            pl.BlockSpec((group, GLA_HEADS, GLA_DK, GLA_DV), lambda n: (n, 0, 0, 0)),
        ],
        out_specs=(
            pl.BlockSpec((rows, GLA_WIDTH), lambda n: (n, 0)),
            pl.BlockSpec((group, GLA_HEADS, GLA_DK, GLA_DV), lambda n: (n, 0, 0, 0)),
        ),
        scratch_shapes=[pltpu.VMEM((group * GLA_HEADS, GLA_DV, GLA_DK), F32)],
        compiler_params=_cparams(("arbitrary",)),
        name="gla_sample",
    )(gla_in, gla_in, gla_in, log_a, norm_g, state)


def _multiplicity(delta):
    total = jnp.zeros(delta.shape, F32)
    for d in DILATIONS:
        assert d & (d - 1) == 0
        hit = jnp.where(jnp.bitwise_and(delta, d - 1) == 0, 1.0, 0.0)
        total = total + jnp.where(delta <= BAND * d, hit, 0.0)
    return jnp.where(delta >= 0, total, 0.0)


def _sample_attn_kernel(q_ref, kn_ref, vn_ref, ckt_ref, cvt_ref, o_ref):
    steps = q_ref.shape[0]
    rows = ATT_HEADS * steps
    cache_len = ckt_ref.shape[2]
    qt = jnp.concatenate([q_ref[...]] * ATT_HEADS, axis=0)
    rh = lax.broadcasted_iota(jnp.int32, (rows, ATT_WIDTH), 0) // steps
    ch = lax.broadcasted_iota(jnp.int32, (rows, ATT_WIDTH), 1) // ATT_HEAD_DIM
    own = rh == ch
    qb = jnp.where(own, qt, 0.0).astype(BF16)
    pad = jnp.zeros((LANES - steps, ATT_WIDTH), F32)
    kn = jnp.concatenate([kn_ref[...], pad], axis=0).astype(BF16)
    vn = jnp.concatenate([vn_ref[...], pad], axis=0).astype(BF16)
    assert steps & (steps - 1) == 0
    t_c = jnp.bitwise_and(lax.broadcasted_iota(jnp.int32, (rows, cache_len), 0), steps - 1)
    mc = _multiplicity(cache_len + t_c - lax.broadcasted_iota(jnp.int32, (rows, cache_len), 1))
    t_n = jnp.bitwise_and(lax.broadcasted_iota(jnp.int32, (rows, LANES), 0), steps - 1)
    new = lax.broadcasted_iota(jnp.int32, (rows, LANES), 1)
    mn = jnp.where(new < steps, _multiplicity(t_n - new), 0.0)
    sc = jnp.where(mc > 0.0, _dot(qb, ckt_ref[0].astype(BF16)), NEG)
    sn = jnp.where(mn > 0.0, _dot_nt(qb, kn), NEG)
    m = jnp.maximum(jnp.max(sc, axis=-1, keepdims=True), jnp.max(sn, axis=-1, keepdims=True))
    pc = mc * jnp.exp(sc - m)
    pn = mn * jnp.exp(sn - m)
    den = jnp.sum(pc, axis=-1, keepdims=True) + jnp.sum(pn, axis=-1, keepdims=True)
    o = _dot_nt(pc.astype(BF16), cvt_ref[0].astype(BF16)) + _dot(pn.astype(BF16), vn)
    o = jnp.where(own, o / den, 0.0)
    o_ref[...] = jnp.sum(o.reshape(ATT_HEADS, steps, ATT_WIDTH), axis=0)


def _sample_attention(q, k_new, v_new, cache_kt, cache_vt, *, steps):
    nb, _, cache_len = cache_kt.shape
    tok = pl.BlockSpec((steps, ATT_WIDTH), lambda n: (n, 0))
    cache = pl.BlockSpec((1, ATT_WIDTH, cache_len), lambda n: (n, 0, 0))
    return pl.pallas_call(
        _sample_attn_kernel,
        out_shape=jax.ShapeDtypeStruct((nb * steps, ATT_WIDTH), F32),
        grid=(nb,),
        in_specs=[tok, tok, tok, cache, cache],
        out_specs=tok,
        compiler_params=_cparams(("arbitrary",)),
        name="sample_attn",
    )(q, k_new, v_new, cache_kt, cache_vt)


def _outproj_kernel(att_ref, gla_ref, x_ref, gate_ref, ng_ref, w_ref, h_ref):
    att = jnp.concatenate([att_ref[hp].astype(F32) for hp in range(HEAD_PAIRS)], axis=1)
    rs = lax.rsqrt(jnp.mean(att * att, axis=-1, keepdims=True) + EPS)
    att_y = (att * rs * ng_ref[...]).astype(BF16)
    mix = _dot(att_y, w_ref[:ATT_WIDTH, :]) + _dot(gla_ref[...], w_ref[ATT_WIDTH:, :])
    h_ref[...] = x_ref[...] + gate_ref[...] * mix


def _outproj(att, gla_y, x, gate, att_norm_g, w_out, *, tm):
    t = x.shape[0]
    mod_rows = gate.shape[0]
    mod_map = (lambda i: (0, 0)) if mod_rows == 1 else (lambda i: (i, 0))
    mod_block = (1, D_MODEL) if mod_rows == 1 else (tm, D_MODEL)
    return pl.pallas_call(
        _outproj_kernel,
        out_shape=jax.ShapeDtypeStruct((t, D_MODEL), F32),
        grid=(t // tm,),
        in_specs=[
            pl.BlockSpec((HEAD_PAIRS, tm, LANES), lambda i: (0, i, 0)),
            pl.BlockSpec((tm, GLA_WIDTH), lambda i: (i, 0)),
            pl.BlockSpec((tm, D_MODEL), lambda i: (i, 0)),
            pl.BlockSpec(mod_block, mod_map),
            pl.BlockSpec((1, ATT_WIDTH), lambda i: (0, 0)),
            pl.BlockSpec((ATT_WIDTH + GLA_WIDTH, D_MODEL), lambda i: (0, 0)),
        ],
        out_specs=pl.BlockSpec((tm, D_MODEL), lambda i: (i, 0)),
        compiler_params=_cparams(("arbitrary",)),
        name="outproj",
    )(att, gla_y, x, gate, att_norm_g, w_out)


def _ffn_kernel(h_ref, g_ref, sc_ref, sh_ref, gate_ref, fg_ref, wu_ref, wd_ref, y_ref, hn_ref):
    f = pl.program_id(1)

    @pl.when(f == 0)
    def _():
        hn_ref[...] = _norm_mod(h_ref[...], g_ref[...], sc_ref[...], sh_ref[...]).astype(BF16)
        y_ref[...] = jnp.zeros_like(y_ref)

    u = jnp.maximum(_dot(hn_ref[...], wu_ref[...]), 0.0)
    y_ref[...] += _dot((u * u).astype(BF16), wd_ref[...])

    @pl.when(f == pl.num_programs(1) - 1)
    def _():
        h2 = h_ref[...] + gate_ref[...] * y_ref[...]
        rs = lax.rsqrt(jnp.mean(h2 * h2, axis=-1, keepdims=True) + EPS)
        y_ref[...] = h2 * rs * fg_ref[...]


def _ffn(h, g, sc, sh, gate, final_g, w_up, w_down, *, tm, tf):
    t = h.shape[0]
    const = pl.BlockSpec((1, D_MODEL), lambda i, f: (0, 0))
    mod = _mod_spec(sc.shape[0], tm)
    return pl.pallas_call(
        _ffn_kernel,
        out_shape=jax.ShapeDtypeStruct((t, D_MODEL), F32),
        grid=(t // tm, D_FF // tf),
        in_specs=[
            pl.BlockSpec((tm, D_MODEL), lambda i, f: (i, 0)),
            const, mod, mod, mod, const,
            pl.BlockSpec((D_MODEL, tf), lambda i, f: (0, f)),
            pl.BlockSpec((tf, D_MODEL), lambda i, f: (f, 0)),
        ],
        out_specs=pl.BlockSpec((tm, D_MODEL), lambda i, f: (i, 0)),
        scratch_shapes=[pltpu.VMEM((tm, D_MODEL), BF16)],
        compiler_params=_cparams(("arbitrary", "arbitrary")),
        name="ffn",
    )(h, g, sc, sh, gate, final_g, w_up, w_down)


def _split_ada(ada):
    return [ada[:, D_MODEL * i:D_MODEL * (i + 1)] for i in range(6)]


def kernel(x_prompt, x_sample, c_prompt, c_sample, cache_win_k, cache_win_v, state_gla, w_in, gla_gate_w, gla_gate_b, att_norm_g, gla_norm_g, w_out, norm1_g, norm2_g, w_ada, b_ada, w_up, w_down, final_g):
    assert w_in.shape[0] == 1, "single-layer model"
    n_p, t_p, _ = x_prompt.shape
    n_s, t_s, _ = x_sample.shape
    assert n_p == 1
    rows_s = n_s * t_s
    wb_p = min(MAX_WINDOW, t_p)

    n_main = 6 * 1024
    w_in_t = jnp.transpose(w_in[0])
    w_main = w_in_t[:n_main].astype(BF16)
    w_ag = jnp.pad(w_in_t[n_main:], ((0, LANES - GLA_GATE_RANK), (0, 0))).astype(BF16)
    gate_w = jnp.pad(gla_gate_w[0], ((0, LANES - GLA_GATE_RANK), (0, 0))).astype(BF16)
    gate_b = gla_gate_b[0][None, :]
    w_out_b = w_out[0].astype(BF16)
    w_up_b = w_up[0].astype(BF16)
    w_down_b = w_down[0].astype(BF16)
    n1g, n2g = norm1_g[0][None, :], norm2_g[0][None, :]
    ang, gng = att_norm_g[0][None, :], gla_norm_g[0][None, :]
    fg = final_g[None, :]

    c_rows = n_p + n_s
    c_pad = -c_rows % 8
    c_all = jnp.concatenate([c_prompt, c_sample, jnp.zeros((c_pad, D_MODEL), F32)], axis=0)
    ada = _ada(c_all, w_ada[0], b_ada[0][None, :])
    sh1_p, sc1_p, g1_p, sh2_p, sc2_p, g2_p = _split_ada(ada[:n_p])
    sh1_s, sc1_s, g1_s, sh2_s, sc2_s, g2_s = _split_ada(
        jnp.repeat(ada[n_p:c_rows], t_s, axis=0))

    xp = x_prompt[0]
    cs_p, sn_p = _rope_tables(jnp.arange(t_p))
    outs = _inproj(xp, n1g, sc1_p, sh1_p, w_main, w_ag, gate_w, gate_b, cs_p, sn_p,
                   tm=512, emit_att=True)
    att = _prompt_attention(outs[:9])
    gla_in, log_a = outs[9], outs[10]
    win = _qkv_rows(xp, n1g, sc1_p, sh1_p, w_main, cs_p, sn_p,
                    tm=512, first_row=t_p - wb_p, rows=wb_p, col0=1)
    win_k, win_v = win[0], win[1]
    gla_y, st_p = _gla_prompt(gla_in, log_a, gng, tb=512, chunk=64, sub=16)
    h_p = _outproj(att, gla_y, xp, g1_p, ang, w_out_b, tm=512)
    y_p = _ffn(h_p, n2g, sc2_p, sh2_p, g2_p, fg, w_up_b, w_down_b, tm=1024, tf=512)

    xs = x_sample.reshape(rows_s, D_MODEL)
    cs_s, sn_s = _rope_tables(jnp.tile(PAST_LEN + jnp.arange(t_s), n_s))
    gla_in_s, log_a_s = _inproj(xs, n1g, sc1_s, sh1_s, w_main, w_ag, gate_w, gate_b, cs_s, sn_s,
                                tm=rows_s, emit_att=False)
    qkv_s = _qkv_rows(xs, n1g, sc1_s, sh1_s, w_main, cs_s, sn_s,
                      tm=rows_s, first_row=0, rows=rows_s, col0=0)
    new_k, new_v = qkv_s[1], qkv_s[2]
    wc = cache_win_k.shape[2]
    cache_t = lambda c: jnp.transpose(c[0], (0, 2, 3, 1)).reshape(n_s, ATT_WIDTH, wc)
    att_tok = _sample_attention(qkv_s[0], new_k, new_v, cache_t(cache_win_k), cache_t(cache_win_v),
                                steps=t_s)
    att_s = jnp.transpose(att_tok.reshape(rows_s, HEAD_PAIRS, LANES), (1, 0, 2)).astype(BF16)
    gla_y_s, st_s = _gla_sample(gla_in_s.astype(F32), log_a_s, gng, state_gla[0], steps=t_s,
                                group=4 if n_s % 4 == 0 else 1)
    h_s = _outproj(att_s, gla_y_s.astype(BF16), xs, g1_s, ang, w_out_b, tm=rows_s)
    y_s = _ffn(h_s, n2g, sc2_s, sh2_s, g2_s, fg, w_up_b, w_down_b, tm=rows_s, tf=2048)

    return (
        y_p[None],
        y_s.reshape(n_s, t_s, D_MODEL),
        win_k.reshape(1, n_p, wb_p, ATT_HEADS, ATT_HEAD_DIM),
        win_v.reshape(1, n_p, wb_p, ATT_HEADS, ATT_HEAD_DIM),
        st_p[None, None],
        new_k.reshape(1, n_s, t_s, ATT_HEADS, ATT_HEAD_DIM),
        new_v.reshape(1, n_s, t_s, ATT_HEADS, ATT_HEAD_DIM),
        st_s[None],
    )
```

```python
import functools

import jax
import jax.numpy as jnp
from jax import lax
from jax.experimental import pallas as pl
from jax.experimental.pallas import tpu as pltpu

F32 = jnp.float32
BF16 = jnp.bfloat16

D_MODEL = 2048
ATT_WIDTH = 1024
ATT_HEADS = 16
ATT_HEAD_DIM = 64
ROT_DIM = 16
ROPE_THETA = 500000.0
DILATIONS = (1, 4, 16)
BAND = 128
MAX_WINDOW = 2048
GLA_HEADS = 4
GLA_DK = 128
GLA_DV = 256
GLA_WIDTH = 1024
GLA_GATE_RANK = 16
GLA_TAU = 16.0
D_FF = 8192
EPS = 1e-6
PAST_LEN = 16384
NEG = -1e30

LANES = 128
HEAD_PAIRS = ATT_WIDTH // LANES
VMEM_LIMIT = 60000 * 1024


def _cparams(sem):
    return pltpu.CompilerParams(dimension_semantics=sem, vmem_limit_bytes=VMEM_LIMIT)


def _dot(a, b):
    return jnp.dot(a, b, preferred_element_type=F32)


def _dot_nt(a, b):
    return lax.dot_general(a, b, (((1,), (1,)), ((), ())), preferred_element_type=F32)


def _dot_tn(a, b):
    return lax.dot_general(a, b, (((0,), (0,)), ((), ())), preferred_element_type=F32)


def _silu(x):
    return x / (1.0 + jnp.exp(-x))


def _ada_kernel(c_ref, w_ref, b_ref, o_ref):
    s = _silu(c_ref[...]).astype(BF16)
    o_ref[...] = _dot(s, w_ref[...].astype(BF16)) + b_ref[...]


def _ada(c, w_ada, b_ada):
    rows = c.shape[0]
    n = w_ada.shape[1]
    tn = 1024
    return pl.pallas_call(
        _ada_kernel,
        out_shape=jax.ShapeDtypeStruct((rows, n), F32),
        grid=(n // tn,),
        in_specs=[
            pl.BlockSpec((rows, D_MODEL), lambda j: (0, 0)),
            pl.BlockSpec((D_MODEL, tn), lambda j: (0, j)),
            pl.BlockSpec((1, tn), lambda j: (0, j)),
        ],
        out_specs=pl.BlockSpec((rows, tn), lambda j: (0, j)),
        compiler_params=_cparams(("arbitrary",)),
        name="ada",
    )(c, w_ada, b_ada)


def _rope(slab, cs, sn):
    d = lax.broadcasted_iota(jnp.int32, (1, LANES), 1) % ATT_HEAD_DIM
    half = ROT_DIM // 2
    from_hi = jnp.where(d < half, -1.0, 0.0)
    from_lo = jnp.where((d >= half) & (d < ROT_DIM), 1.0, 0.0)
    return (slab * cs + pltpu.roll(slab, LANES - half, 1) * (sn * from_hi)
            + pltpu.roll(slab, half, 1) * (sn * from_lo))


def _norm_mod(x, g, sc, sh):
    rs = lax.rsqrt(jnp.mean(x * x, axis=-1, keepdims=True) + EPS)
    return x * rs * g * (1.0 + sc) + sh


def _inproj_kernel(*refs, emit_att):
    (x_ref, g_ref, sc_ref, sh_ref, w_ref, wag_ref, gw_ref, gb_ref, cs_ref, sn_ref), refs = (
        refs[:10], refs[10:])
    if emit_att:
        att_refs, refs = refs[:9], refs[9:]
    gla_ref, la_ref, xn_ref = refs[:3]
    j = pl.program_id(1)
    col = j if emit_att else j + 3

    @pl.when(j == 0)
    def _():
        xnb = _norm_mod(x_ref[...], g_ref[...], sc_ref[...], sh_ref[...]).astype(BF16)
        xn_ref[...] = xnb
        ag = _dot_nt(xnb, wag_ref[...])
        z = _dot(ag.astype(BF16), gw_ref[...]) + gb_ref[...]
        log_sig = jnp.minimum(z, 0.0) - jnp.log(1.0 + jnp.exp(-jnp.abs(z)))
        la_ref[...] = log_sig / GLA_TAU

    if emit_att:
        s0_ref, s1_ref = refs[3:5]
        tm = x_ref.shape[0]
        n4, n16 = tm // 4, tm // 16
        group = 2 * LANES

        def emit(out1, out4, out16, slab_fn):
            for hp in range(HEAD_PAIRS):
                if hp % 2 == 0:
                    acc = _dot_nt(xn_ref[...], w_ref[LANES * hp:LANES * hp + group, :])
                slab = slab_fn(acc[:, LANES * (hp % 2):LANES * (hp % 2 + 1)])
                out1[hp] = slab.astype(BF16)
                s0_ref[hp] = slab
                for r in range(4):
                    part = s0_ref[hp, pl.ds(r, n4, stride=4), :]
                    out4[hp, :, LANES * r:LANES * (r + 1)] = part.astype(BF16)
                    s1_ref[hp, n4 * r:n4 * (r + 1), :] = part
                for r in range(16):
                    part = s1_ref[hp, pl.ds(n4 * (r % 4) + r // 4, n16, stride=4), :]
                    out16[hp, :, LANES * r:LANES * (r + 1)] = part.astype(BF16)

        @pl.when(col == 0)
        def _():
            emit(*att_refs[0:3],
                 lambda s: _rope(s, cs_ref[...], sn_ref[...]) * (ATT_HEAD_DIM ** -0.5))

        @pl.when(col == 1)
        def _():
            emit(*att_refs[3:6], lambda s: _rope(s, cs_ref[...], sn_ref[...]))

        @pl.when(col == 2)
        def _():
            emit(*att_refs[6:9], lambda s: s)

    @pl.when(col == 3)
    def _():
        acc = _dot_nt(xn_ref[...], w_ref[...])
        half = GLA_HEADS * GLA_DK
        gla_ref[:, :half] = (acc[:, :half] * (GLA_DK ** -0.5)).astype(BF16)
        gla_ref[:, half:] = acc[:, half:].astype(BF16)

    @pl.when(col >= 4)
    def _():
        gla_ref[...] = _dot_nt(xn_ref[...], w_ref[...]).astype(BF16)


def _mod_spec(mod_rows, tm):
    if mod_rows == 1:
        return pl.BlockSpec((1, D_MODEL), lambda i, j: (0, 0))
    return pl.BlockSpec((tm, D_MODEL), lambda i, j: (i, 0))


def _inproj(x, g, sc, sh, w_main, w_ag, gate_w, gate_b, cs, sn, *, tm, emit_att):
    t = x.shape[0]
    col0 = 0 if emit_att else 3
    mod = _mod_spec(sc.shape[0], tm)
    out_shape, out_specs, scratch = [], [], [pltpu.VMEM((tm, D_MODEL), BF16)]
    if emit_att:
        for _ in range(3):
            for d in DILATIONS:
                out_shape.append(jax.ShapeDtypeStruct((HEAD_PAIRS, t // d, d * LANES), BF16))
                out_specs.append(pl.BlockSpec((HEAD_PAIRS, tm // d, d * LANES),
                                              lambda i, j: (0, i, 0)))
        scratch += [pltpu.VMEM((HEAD_PAIRS, tm, LANES), F32) for _ in range(2)]
    out_shape += [jax.ShapeDtypeStruct((t, 3 * 1024), BF16),
                  jax.ShapeDtypeStruct((t, GLA_HEADS * GLA_DK), F32)]
    out_specs += [pl.BlockSpec((tm, 1024), lambda i, j: (i, jnp.maximum(j + col0 - 3, 0))),
                  pl.BlockSpec((tm, GLA_HEADS * GLA_DK), lambda i, j: (i, 0))]
    return pl.pallas_call(
        functools.partial(_inproj_kernel, emit_att=emit_att),
        out_shape=tuple(out_shape),
        grid=(t // tm, 6 - col0),
        in_specs=[
            pl.BlockSpec((tm, D_MODEL), lambda i, j: (i, 0)),
            pl.BlockSpec((1, D_MODEL), lambda i, j: (0, 0)),
            mod, mod,
            pl.BlockSpec((1024, D_MODEL), lambda i, j: (j + col0, 0)),
            pl.BlockSpec((LANES, D_MODEL), lambda i, j: (0, 0)),
            pl.BlockSpec((LANES, GLA_HEADS * GLA_DK), lambda i, j: (0, 0)),
            pl.BlockSpec((1, GLA_HEADS * GLA_DK), lambda i, j: (0, 0)),
            pl.BlockSpec((tm, LANES), lambda i, j: (i, 0)),
            pl.BlockSpec((tm, LANES), lambda i, j: (i, 0)),
        ],
        out_specs=tuple(out_specs),
        scratch_shapes=scratch,
        compiler_params=_cparams(("arbitrary", "arbitrary")),
        name="inproj_att" if emit_att else "inproj_gla",
    )(x, g, sc, sh, w_main, w_ag, gate_w, gate_b, cs, sn)


def _qkv_rows_kernel(x_ref, g_ref, sc_ref, sh_ref, w_ref, cs_ref, sn_ref, o_ref, xn_ref, *, col0):
    j = pl.program_id(1)

    @pl.when(j == 0)
    def _():
        xn_ref[...] = _norm_mod(x_ref[...], g_ref[...], sc_ref[...], sh_ref[...]).astype(BF16)

    acc = _dot_nt(xn_ref[...], w_ref[...])
    col = j + col0

    @pl.when(col <= 1)
    def _():
        scale = jnp.where(col == 0, ATT_HEAD_DIM ** -0.5, 1.0)
        for hp in range(HEAD_PAIRS):
            lanes = slice(LANES * hp, LANES * (hp + 1))
            o_ref[0, :, lanes] = _rope(acc[:, lanes], cs_ref[...], sn_ref[...]) * scale

    @pl.when(col == 2)
    def _():
        o_ref[0] = acc


def _qkv_rows(x, g, sc, sh, w_main, cs, sn, *, tm, first_row, rows, col0):
    first = first_row // tm
    assert sc.shape[0] == 1 or first == 0
    ncols = 3 - col0
    mod = _mod_spec(sc.shape[0], tm)
    return pl.pallas_call(
        functools.partial(_qkv_rows_kernel, col0=col0),
        out_shape=jax.ShapeDtypeStruct((ncols, rows, ATT_WIDTH), F32),
        grid=(rows // tm, ncols),
        in_specs=[
            pl.BlockSpec((tm, D_MODEL), lambda i, j: (i + first, 0)),
            pl.BlockSpec((1, D_MODEL), lambda i, j: (0, 0)),
            mod, mod,
            pl.BlockSpec((1024, D_MODEL), lambda i, j: (j + col0, 0)),
            pl.BlockSpec((tm, LANES), lambda i, j: (i + first, 0)),
            pl.BlockSpec((tm, LANES), lambda i, j: (i + first, 0)),
        ],
        out_specs=pl.BlockSpec((1, tm, ATT_WIDTH), lambda i, j: (j, i, 0)),
        scratch_shapes=[pltpu.VMEM((tm, D_MODEL), BF16)],
        compiler_params=_cparams(("arbitrary", "arbitrary")),
        name="qkv_rows",
    )(x, g, sc, sh, w_main, cs, sn)


def _rope_tables(pos):
    half = ROT_DIM // 2
    inv_freq = ROPE_THETA ** (-jnp.arange(half, dtype=F32) / half)
    per_head = jnp.concatenate([inv_freq, inv_freq, jnp.zeros((ATT_HEAD_DIM - ROT_DIM,), F32)])
    ang = pos.astype(F32)[:, None] * jnp.tile(per_head, LANES // ATT_HEAD_DIM)[None, :]
    return jnp.cos(ang), jnp.sin(ang)


def _attn_kernel(*refs, n_cast):
    (q1, k1c, v1c, k1p, v1p, q4, k4c, v4c, k4p, v4p, q16, k16c, v16c, k16p, v16p), refs = (
        refs[:15], refs[15:])
    cast_in, o_ref, cast_out = refs[:n_cast], refs[n_cast], refs[n_cast + 1:2 * n_cast + 1]
    a1, m1, l1, a4, m4, l4, t16, s_scr, p_scr, bias_ref, o_scr = refs[2 * n_cast + 1:]
    for src, dst in zip(cast_in, cast_out):
        dst[...] = src[...].astype(dst.dtype)

    first_block = pl.program_id(1) == 0
    lane = lax.broadcasted_iota(jnp.int32, (1, LANES), 1)
    first = lane < ATT_HEAD_DIM
    head0 = jnp.where(first, 1.0, 0.0).astype(BF16)
    head1 = jnp.where(first, 0.0, 1.0).astype(BF16)
    ones = jnp.ones((2 * BAND, LANES), BF16)

    qi = lax.broadcasted_iota(jnp.int32, (2 * BAND, 2 * BAND), 0) % BAND
    kk = lax.broadcasted_iota(jnp.int32, (2 * BAND, 2 * BAND), 1)
    full = jnp.where(jnp.where(kk < BAND, kk - qi, qi - (kk - BAND)) >= 0, 0.0, NEG)
    bias_ref[0] = full
    bias_ref[1] = jnp.where(kk < BAND, full + jnp.where(first_block, NEG, 0.0), full)

    n1 = q1.shape[1] // BAND
    rows4 = q4.shape[1]
    n4 = rows4 // BAND

    def window(cur, prev, i, cols):
        if i == 0:
            return jnp.concatenate([prev[0, :, cols], cur[0, :BAND, cols]], axis=0)
        return cur[0, BAND * (i - 1):BAND * (i + 1), cols]

    jobs = []
    everything = slice(None)
    for i in range(n1):
        jobs.append(dict(q=(q1, i, everything), k=(k1c, k1p), v=(v1c, v1p), bias=int(i == 0),
                         dst=(a1, m1, l1, BAND * i)))
    for r in range(4):
        cols = slice(LANES * r, LANES * (r + 1))
        for i in range(n4):
            jobs.append(dict(q=(q4, i, cols), k=(k4c, k4p), v=(v4c, v4p), bias=int(i == 0),
                             dst=(a4, m4, l4, rows4 * r + BAND * i)))
    for r in range(16):
        cols = slice(LANES * r, LANES * (r + 1))
        jobs.append(dict(q=(q16, 0, cols), k=(k16c, k16p), v=(v16c, v16p), bias=1, merge=r))

    def scores(job, slot):
        ref, i, cols = job["q"]
        q = ref[0, BAND * i:BAND * (i + 1), cols]
        q2 = jnp.concatenate([q * head0, q * head1], axis=0)
        s_scr[slot] = _dot_nt(q2, window(*job["k"], i, cols))

    def softmax(job, slot):
        s = s_scr[slot] + bias_ref[job["bias"]]
        m = jnp.max(s, axis=-1, keepdims=True)
        p_scr[slot] = jnp.exp(s - m).astype(BF16)
        m = jnp.where(first, m[:BAND], m[BAND:])
        if "merge" in job:
            t16[slot] = m
        else:
            job["dst"][1][pl.ds(job["dst"][3], BAND), :] = m

    def values(job, slot):
        _, i, cols = job["q"]
        vw = jnp.concatenate([window(*job["v"], i, cols), ones], axis=1)
        o2 = _dot(p_scr[slot], vw)
        acc = jnp.where(first, o2[:BAND, :LANES], o2[BAND:, :LANES])
        den = jnp.where(first, o2[:BAND, LANES:], o2[BAND:, LANES:])
        if "merge" not in job:
            a_ref, _, l_ref, row = job["dst"]
            a_ref[pl.ds(row, BAND), :] = acc
            l_ref[pl.ds(row, BAND), :] = den
            return
        r = job["merge"]
        sl1 = pl.ds(r, BAND, stride=16)
        sl4 = pl.ds((r % 4) * rows4 + r // 4, BAND, stride=4)
        ma, mb, mc = m1[sl1, :], m4[sl4, :], t16[slot]
        m = jnp.maximum(jnp.maximum(ma, mb), mc)
        wa, wb, wc = jnp.exp(ma - m), jnp.exp(mb - m), jnp.exp(mc - m)
        num = wa * a1[sl1, :] + wb * a4[sl4, :] + wc * acc
        dsum = wa * l1[sl1, :] + wb * l4[sl4, :] + wc * den
        o_scr[sl1, :] = num / dsum

    slots = s_scr.shape[0]
    for step in range(len(jobs) + 2):
        if step < len(jobs):
            scores(jobs[step], step % slots)
        if 1 <= step <= len(jobs):
            softmax(jobs[step - 1], (step - 1) % slots)
        if step >= 2:
            values(jobs[step - 2], (step - 2) % slots)
    o_ref[0] = o_scr[...].astype(o_ref.dtype)


def _prompt_attention(views, weights):
    hp, t, _ = views[0].shape
    sb = 16 * BAND
    ns = t // sb
    in_specs, args = [], []
    for di, d in enumerate(DILATIONS):
        rows, width = sb // d, d * LANES
        cur = pl.BlockSpec((1, rows, width), lambda h, s: (h, s, 0))
        nb = rows // BAND
        prev = pl.BlockSpec((1, BAND, width),
                            lambda h, s, nb=nb: (h, jnp.maximum(s * nb - 1, 0), 0))
        q, k, v = views[di], views[3 + di], views[6 + di]
        in_specs += [cur, cur, cur, prev, prev]
        args += [q, k, v, k, v]
    cast_specs = []
    for w in weights:
        slab = w.shape[0] // (hp * ns)
        assert slab * hp * ns == w.shape[0] and slab % 16 == 0
        cast_specs.append(pl.BlockSpec((slab, w.shape[1]), lambda h, s: (h * ns + s, 0)))
    tile = (2 * BAND, 2 * BAND)
    return pl.pallas_call(
        functools.partial(_attn_kernel, n_cast=len(weights)),
        out_shape=tuple([jax.ShapeDtypeStruct((hp, t, LANES), BF16)]
                        + [jax.ShapeDtypeStruct(w.shape, BF16) for w in weights]),
        grid=(hp, ns),
        in_specs=in_specs + cast_specs,
        out_specs=tuple([pl.BlockSpec((1, sb, LANES), lambda h, s: (h, s, 0))] + cast_specs),
        scratch_shapes=(
            [pltpu.VMEM((sb, LANES), F32) for _ in range(6)]
            + [pltpu.VMEM((3, BAND, LANES), F32),
               pltpu.VMEM((3,) + tile, F32),
               pltpu.VMEM((3,) + tile, BF16),
               pltpu.VMEM((2,) + tile, F32),
               pltpu.VMEM((sb, LANES), F32)]),
        compiler_params=_cparams(("arbitrary", "arbitrary")),
        name="prompt_attn",
    )(*args, *weights)


def _gla_chunk(q, k, v, la, st_ref, h, tri, sub):
    c = q.shape[0]
    la_hi = la.astype(BF16)
    la_lo = (la - la_hi.astype(F32)).astype(BF16)
    b = _dot(tri, la_hi) + _dot(tri, la_lo)
    st = st_ref[h]
    o = _dot_nt((q * jnp.exp(b)).astype(BF16), st.astype(BF16))
    srow = lax.broadcasted_iota(jnp.int32, (c, GLA_DK), 0)
    att_rows = []
    for i in range(c // sub):
        lo, hi = i * sub, (i + 1) * sub
        ref = b[lo:lo + 1, :]
        qh = (q[lo:hi] * jnp.exp(b[lo:hi] - ref)).astype(BF16)
        kh = (k * jnp.exp(jnp.where(srow < hi, ref - b, 0.0))).astype(BF16)
        a = _dot_nt(qh, kh)
        ti = lax.broadcasted_iota(jnp.int32, (sub, c), 0) + lo
        si = lax.broadcasted_iota(jnp.int32, (sub, c), 1)
        att_rows.append(jnp.where(si <= ti, a, 0.0))
    att = att_rows[0] if len(att_rows) == 1 else jnp.concatenate(att_rows, axis=0)
    o = o + _dot(att.astype(BF16), v)
    b_last = b[c - 1:c, :]
    kbar = (k * jnp.exp(b_last - b)).astype(BF16)
    st_ref[h] = st * jnp.exp(b_last) + _dot_tn(v, kbar)
    return o


def _gla_out(o, g, gg):
    rs = lax.rsqrt(jnp.mean(o * o, axis=-1, keepdims=True) + EPS)
    return o * rs * g * _silu(gg)


def _tri(c):
    r = lax.broadcasted_iota(jnp.int32, (c, c), 0)
    s = lax.broadcasted_iota(jnp.int32, (c, c), 1)
    return jnp.where(s <= r, 1.0, 0.0).astype(BF16)


def _gla_prompt_kernel(qk_ref, v_ref, gg_ref, la_ref, g_ref, y_ref, s_ref, st_ref,
                       qd_ref, kl_ref, qs_ref, ks_ref, dec_ref, att_ref, o_ref, *, chunk, sub):
    i = pl.program_id(0)

    @pl.when(i == 0)
    def _():
        st_ref[...] = jnp.zeros_like(st_ref)

    tb = qk_ref.shape[0]
    nchunks, nsub = tb // chunk, chunk // sub
    half = GLA_HEADS * GLA_DK

    r = lax.broadcasted_iota(jnp.int32, (tb, tb), 0)
    c = lax.broadcasted_iota(jnp.int32, (tb, tb), 1)
    start = r - jnp.bitwise_and(r, chunk - 1)
    tri = jnp.where(c <= r, jnp.where(c >= start, 1.0, 0.0), 0.0).astype(BF16)
    la = la_ref[...]
    la_hi = la.astype(BF16)
    la_lo = (la - la_hi.astype(F32)).astype(BF16)
    b = _dot(tri, la_hi) + _dot(tri, la_lo)
    q = qk_ref[:, :half].astype(F32)
    k = qk_ref[:, half:].astype(F32)
    qd_ref[...] = (q * jnp.exp(b)).astype(BF16)
    srow = lax.broadcasted_iota(jnp.int32, (chunk, half), 0)
    for ci in range(nchunks):
        lo = ci * chunk
        bc, kc_ = b[lo:lo + chunk], k[lo:lo + chunk]
        b_last = bc[chunk - 1:chunk]
        dec_ref[ci:ci + 1, :] = jnp.exp(b_last)
        kl_ref[lo:lo + chunk, :] = (kc_ * jnp.exp(b_last - bc)).astype(BF16)
        for si in range(nsub):
            s_lo, s_hi = si * sub, (si + 1) * sub
            ref = bc[s_lo:s_lo + 1]
            qs_ref[lo + s_lo:lo + s_hi, :] = (
                q[lo + s_lo:lo + s_hi] * jnp.exp(bc[s_lo:s_hi] - ref)).astype(BF16)
            ks_ref[si, lo:lo + chunk, :] = (
                kc_ * jnp.exp(jnp.where(srow < s_hi, ref - bc, 0.0))).astype(BF16)

    ti = lax.broadcasted_iota(jnp.int32, (chunk, chunk), 0)
    si_ = lax.broadcasted_iota(jnp.int32, (chunk, chunk), 1)
    causal = si_ <= ti
    jobs = [(ci, h) for ci in range(nchunks) for h in range(GLA_HEADS)]

    def intra(job, slot):
        ci, h = job
        lo, kc = ci * chunk, slice(GLA_DK * h, GLA_DK * (h + 1))
        parts = [_dot_nt(qs_ref[lo + s * sub:lo + (s + 1) * sub, kc], ks_ref[s, lo:lo + chunk, kc])
                 for s in range(nsub)]
        att_ref[slot] = jnp.where(causal, jnp.concatenate(parts, axis=0), 0.0).astype(BF16)

    def state(job, slot):
        ci, h = job
        rows = slice(ci * chunk, (ci + 1) * chunk)
        kc, vc = slice(GLA_DK * h, GLA_DK * (h + 1)), slice(GLA_DV * h, GLA_DV * (h + 1))
        st = st_ref[h]
        v = v_ref[rows, vc]
        o_ref[slot] = _dot_nt(qd_ref[rows, kc], st.astype(BF16)) + _dot(att_ref[slot], v)
        st_ref[h] = st * dec_ref[ci:ci + 1, kc] + _dot_tn(v, kl_ref[rows, kc])

    def finish(job, slot):
        ci, h = job
        rows, vc = slice(ci * chunk, (ci + 1) * chunk), slice(GLA_DV * h, GLA_DV * (h + 1))
        y_ref[rows, vc] = _gla_out(o_ref[slot], g_ref[:, vc],
                                   gg_ref[rows, vc].astype(F32)).astype(y_ref.dtype)

    slots = att_ref.shape[0]
    for step in range(len(jobs) + 2):
        if step < len(jobs):
            intra(jobs[step], step % slots)
        if 1 <= step <= len(jobs):
            state(jobs[step - 1], (step - 1) % slots)
        if step >= 2:
            finish(jobs[step - 2], (step - 2) % slots)

    @pl.when(i == pl.num_programs(0) - 1)
    def _():
        for h in range(GLA_HEADS):
            s_ref[h] = st_ref[h].T


def _gla_prompt(gla_in, log_a, norm_g, *, tb, chunk, sub):
    t = gla_in.shape[0]
    return pl.pallas_call(
        functools.partial(_gla_prompt_kernel, chunk=chunk, sub=sub),
        out_shape=(
            jax.ShapeDtypeStruct((t, GLA_WIDTH), BF16),
            jax.ShapeDtypeStruct((GLA_HEADS, GLA_DK, GLA_DV), F32),
        ),
        grid=(t // tb,),
        in_specs=[
            pl.BlockSpec((tb, 1024), lambda i: (i, 0)),
            pl.BlockSpec((tb, 1024), lambda i: (i, 1)),
            pl.BlockSpec((tb, 1024), lambda i: (i, 2)),
            pl.BlockSpec((tb, GLA_HEADS * GLA_DK), lambda i: (i, 0)),
            pl.BlockSpec((1, GLA_WIDTH), lambda i: (0, 0)),
        ],
        out_specs=(
            pl.BlockSpec((tb, GLA_WIDTH), lambda i: (i, 0)),
            pl.BlockSpec((GLA_HEADS, GLA_DK, GLA_DV), lambda i: (0, 0, 0)),
        ),
        scratch_shapes=[
            pltpu.VMEM((GLA_HEADS, GLA_DV, GLA_DK), F32),
            pltpu.VMEM((tb, GLA_HEADS * GLA_DK), BF16),
            pltpu.VMEM((tb, GLA_HEADS * GLA_DK), BF16),
            pltpu.VMEM((tb, GLA_HEADS * GLA_DK), BF16),
            pltpu.VMEM((chunk // sub, tb, GLA_HEADS * GLA_DK), BF16),
            pltpu.VMEM((tb // chunk, GLA_HEADS * GLA_DK), F32),
            pltpu.VMEM((3, chunk, chunk), BF16),
            pltpu.VMEM((3, chunk, GLA_DV), F32),
        ],
        compiler_params=_cparams(("arbitrary",)),
        name="gla_prompt",
    )(gla_in, gla_in, gla_in, log_a, norm_g)


def _gla_sample_kernel(qk_ref, v_ref, gg_ref, la_ref, g_ref, s0_ref, y_ref, s_ref, st_ref, *, steps):
    group = s0_ref.shape[0]
    c = 2 * steps
    tri = _tri(c)
    half = GLA_HEADS * GLA_DK
    pad = lambda a: jnp.concatenate([a, jnp.zeros_like(a)], axis=0)
    for n in range(group):
        rows = slice(n * steps, (n + 1) * steps)
        for h in range(GLA_HEADS):
            idx = n * GLA_HEADS + h
            kc = slice(GLA_DK * h, GLA_DK * (h + 1))
            vc = slice(GLA_DV * h, GLA_DV * (h + 1))
            st_ref[idx] = s0_ref[n, h].T
            q = pad(qk_ref[rows, kc])
            k = pad(qk_ref[rows, half + GLA_DK * h:half + GLA_DK * (h + 1)])
            v = pad(v_ref[rows, vc]).astype(BF16)
            o = _gla_chunk(q, k, v, pad(la_ref[rows, kc]), st_ref, idx, tri, c)
            y_ref[rows, vc] = _gla_out(o[:steps], g_ref[:, vc], gg_ref[rows, vc])
            s_ref[n, h] = st_ref[idx].T


def _gla_sample(gla_in, log_a, norm_g, state, *, steps, group):
    t = gla_in.shape[0]
    rows = steps * group
    return pl.pallas_call(
        functools.partial(_gla_sample_kernel, steps=steps),
        out_shape=(
            jax.ShapeDtypeStruct((t, GLA_WIDTH), F32),
            jax.ShapeDtypeStruct(state.shape, F32),
        ),
        grid=(t // rows,),
        in_specs=[
            pl.BlockSpec((rows, 1024), lambda n: (n, 0)),
            pl.BlockSpec((rows, 1024), lambda n: (n, 1)),
            pl.BlockSpec((rows, 1024), lambda n: (n, 2)),
            pl.BlockSpec((rows, GLA_HEADS * GLA_DK), lambda n: (n, 0)),
            pl.BlockSpec((1, GLA_WIDTH), lambda n: (0, 0)),
            pl.BlockSpec((group, GLA_HEADS, GLA_DK, GLA_DV), lambda n: (n, 0, 0, 0)),
        ],
        out_specs=(
            pl.BlockSpec((rows, GLA_WIDTH), lambda n: (n, 0)),
            pl.BlockSpec((group, GLA_HEADS, GLA_DK, GLA_DV), lambda n: (n, 0, 0, 0)),
        ),
        scratch_shapes=[pltpu.VMEM((group * GLA_HEADS, GLA_DV, GLA_DK), F32)],
        compiler_params=_cparams(("arbitrary",)),
        name="gla_sample",
    )(gla_in, gla_in, gla_in, log_a, norm_g, state)


def _multiplicity(delta):
    total = jnp.zeros(delta.shape, F32)
    for d in DILATIONS:
        assert d & (d - 1) == 0
        hit = jnp.where(jnp.bitwise_and(delta, d - 1) == 0, 1.0, 0.0)
        total = total + jnp.where(delta <= BAND * d, hit, 0.0)
    return jnp.where(delta >= 0, total, 0.0)


def _sample_attn_kernel(q_ref, kn_ref, vn_ref, ckt_ref, cvt_ref, o_ref):
    steps = q_ref.shape[0]
    rows = ATT_HEADS * steps
    cache_len = ckt_ref.shape[2]
    qt = jnp.concatenate([q_ref[...]] * ATT_HEADS, axis=0)
    rh = lax.broadcasted_iota(jnp.int32, (rows, ATT_WIDTH), 0) // steps
    ch = lax.broadcasted_iota(jnp.int32, (rows, ATT_WIDTH), 1) // ATT_HEAD_DIM
    own = rh == ch
    qb = jnp.where(own, qt, 0.0).astype(BF16)
    pad = jnp.zeros((LANES - steps, ATT_WIDTH), F32)
    kn = jnp.concatenate([kn_ref[...], pad], axis=0).astype(BF16)
    vn = jnp.concatenate([vn_ref[...], pad], axis=0).astype(BF16)
    assert steps & (steps - 1) == 0
    t_c = jnp.bitwise_and(lax.broadcasted_iota(jnp.int32, (rows, cache_len), 0), steps - 1)
    mc = _multiplicity(cache_len + t_c - lax.broadcasted_iota(jnp.int32, (rows, cache_len), 1))
    t_n = jnp.bitwise_and(lax.broadcasted_iota(jnp.int32, (rows, LANES), 0), steps - 1)
    new = lax.broadcasted_iota(jnp.int32, (rows, LANES), 1)
    mn = jnp.where(new < steps, _multiplicity(t_n - new), 0.0)
    sc = jnp.where(mc > 0.0, _dot(qb, ckt_ref[0].astype(BF16)), NEG)
    sn = jnp.where(mn > 0.0, _dot_nt(qb, kn), NEG)
    m = jnp.maximum(jnp.max(sc, axis=-1, keepdims=True), jnp.max(sn, axis=-1, keepdims=True))
    pc = mc * jnp.exp(sc - m)
    pn = mn * jnp.exp(sn - m)
    den = jnp.sum(pc, axis=-1, keepdims=True) + jnp.sum(pn, axis=-1, keepdims=True)
    o = _dot_nt(pc.astype(BF16), cvt_ref[0].astype(BF16)) + _dot(pn.astype(BF16), vn)
    o = jnp.where(own, o / den, 0.0)
    o_ref[...] = jnp.sum(o.reshape(ATT_HEADS, steps, ATT_WIDTH), axis=0)


def _sample_attention(q, k_new, v_new, cache_kt, cache_vt, *, steps):
    nb, _, cache_len = cache_kt.shape
    tok = pl.BlockSpec((steps, ATT_WIDTH), lambda n: (n, 0))
    cache = pl.BlockSpec((1, ATT_WIDTH, cache_len), lambda n: (n, 0, 0))
    return pl.pallas_call(
        _sample_attn_kernel,
        out_shape=jax.ShapeDtypeStruct((nb * steps, ATT_WIDTH), F32),
        grid=(nb,),
        in_specs=[tok, tok, tok, cache, cache],
        out_specs=tok,
        compiler_params=_cparams(("arbitrary",)),
        name="sample_attn",
    )(q, k_new, v_new, cache_kt, cache_vt)


def _outproj_kernel(att_ref, gla_ref, x_ref, gate_ref, ng_ref, w_ref, g2_ref, sc2_ref, sh2_ref,
                    h_ref, hn_ref):
    att = jnp.concatenate([att_ref[hp].astype(F32) for hp in range(HEAD_PAIRS)], axis=1)
    rs = lax.rsqrt(jnp.mean(att * att, axis=-1, keepdims=True) + EPS)
    att_y = (att * rs * ng_ref[...]).astype(BF16)
    mix = _dot(att_y, w_ref[:ATT_WIDTH, :]) + _dot(gla_ref[...], w_ref[ATT_WIDTH:, :])
    h = x_ref[...] + gate_ref[...] * mix
    h_ref[...] = h
    hn_ref[...] = _norm_mod(h, g2_ref[...], sc2_ref[...], sh2_ref[...]).astype(BF16)


def _outproj(att, gla_y, x, gate, att_norm_g, w_out, g2, sc2, sh2, *, tm):
    t = x.shape[0]
    mod_rows = gate.shape[0]
    mod_map = (lambda i: (0, 0)) if mod_rows == 1 else (lambda i: (i, 0))
    mod = pl.BlockSpec((1, D_MODEL) if mod_rows == 1 else (tm, D_MODEL), mod_map)
    row = pl.BlockSpec((tm, D_MODEL), lambda i: (i, 0))
    return pl.pallas_call(
        _outproj_kernel,
        out_shape=(jax.ShapeDtypeStruct((t, D_MODEL), F32),
                   jax.ShapeDtypeStruct((t, D_MODEL), BF16)),
        grid=(t // tm,),
        in_specs=[
            pl.BlockSpec((HEAD_PAIRS, tm, LANES), lambda i: (0, i, 0)),
            pl.BlockSpec((tm, GLA_WIDTH), lambda i: (i, 0)),
            row, mod,
            pl.BlockSpec((1, ATT_WIDTH), lambda i: (0, 0)),
            pl.BlockSpec((ATT_WIDTH + GLA_WIDTH, D_MODEL), lambda i: (0, 0)),
            pl.BlockSpec((1, D_MODEL), lambda i: (0, 0)),
            mod, mod,
        ],
        out_specs=(row, row),
        compiler_params=_cparams(("arbitrary",)),
        name="outproj",
    )(att, gla_y, x, gate, att_norm_g, w_out, g2, sc2, sh2)


def _ffn_kernel(h_ref, hn_ref, gate_ref, fg_ref, wu_ref, wd_ref, y_ref):
    f = pl.program_id(1)

    def down():
        u = jnp.maximum(_dot(hn_ref[...], wu_ref[...]), 0.0)
        return _dot((u * u).astype(BF16), wd_ref[...])

    @pl.when(f == 0)
    def _():
        y_ref[...] = down()

    @pl.when(f > 0)
    def _():
        y_ref[...] += down()

    @pl.when(f == pl.num_programs(1) - 1)
    def _():
        h2 = h_ref[...] + gate_ref[...] * y_ref[...]
        rs = lax.rsqrt(jnp.mean(h2 * h2, axis=-1, keepdims=True) + EPS)
        y_ref[...] = h2 * rs * fg_ref[...]


def _ffn(h, hn, gate, final_g, w_up, w_down, *, tm, tf):
    t = h.shape[0]
    row = pl.BlockSpec((tm, D_MODEL), lambda i, f: (i, 0))
    return pl.pallas_call(
        _ffn_kernel,
        out_shape=jax.ShapeDtypeStruct((t, D_MODEL), F32),
        grid=(t // tm, D_FF // tf),
        in_specs=[
            row, row,
            _mod_spec(gate.shape[0], tm),
            pl.BlockSpec((1, D_MODEL), lambda i, f: (0, 0)),
            pl.BlockSpec((D_MODEL, tf), lambda i, f: (0, f)),
            pl.BlockSpec((tf, D_MODEL), lambda i, f: (f, 0)),
        ],
        out_specs=row,
        compiler_params=_cparams(("arbitrary", "arbitrary")),
        name="ffn",
    )(h, hn, gate, final_g, w_up, w_down)


def _split_ada(ada):
    return [ada[:, D_MODEL * i:D_MODEL * (i + 1)] for i in range(6)]


def kernel(x_prompt, x_sample, c_prompt, c_sample, cache_win_k, cache_win_v, state_gla, w_in, gla_gate_w, gla_gate_b, att_norm_g, gla_norm_g, w_out, norm1_g, norm2_g, w_ada, b_ada, w_up, w_down, final_g):
    assert w_in.shape[0] == 1, "single-layer model"
    n_p, t_p, _ = x_prompt.shape
    n_s, t_s, _ = x_sample.shape
    assert n_p == 1
    rows_s = n_s * t_s
    wb_p = min(MAX_WINDOW, t_p)

    n_main = 6 * 1024
    w_in_t = jnp.transpose(w_in[0])
    w_main = w_in_t[:n_main].astype(BF16)
    w_ag = jnp.pad(w_in_t[n_main:], ((0, LANES - GLA_GATE_RANK), (0, 0))).astype(BF16)
    gate_w = jnp.pad(gla_gate_w[0], ((0, LANES - GLA_GATE_RANK), (0, 0))).astype(BF16)
    gate_b = gla_gate_b[0][None, :]
    n1g, n2g = norm1_g[0][None, :], norm2_g[0][None, :]
    ang, gng = att_norm_g[0][None, :], gla_norm_g[0][None, :]
    fg = final_g[None, :]

    c_rows = n_p + n_s
    c_pad = -c_rows % 8
    c_all = jnp.concatenate([c_prompt, c_sample, jnp.zeros((c_pad, D_MODEL), F32)], axis=0)
    ada = _ada(c_all, w_ada[0], b_ada[0][None, :])
    sh1_p, sc1_p, g1_p, sh2_p, sc2_p, g2_p = _split_ada(ada[:n_p])
    sh1_s, sc1_s, g1_s, sh2_s, sc2_s, g2_s = _split_ada(
        jnp.repeat(ada[n_p:c_rows], t_s, axis=0))

    xp = x_prompt[0]
    cs_p, sn_p = _rope_tables(jnp.arange(t_p))
    outs = _inproj(xp, n1g, sc1_p, sh1_p, w_main, w_ag, gate_w, gate_b, cs_p, sn_p,
                   tm=512, emit_att=True)
    att, w_out_b, w_up_b, w_down_b = _prompt_attention(outs[:9], (w_out[0], w_up[0], w_down[0]))
    gla_in, log_a = outs[9], outs[10]
    win = _qkv_rows(xp, n1g, sc1_p, sh1_p, w_main, cs_p, sn_p,
                    tm=1024, first_row=t_p - wb_p, rows=wb_p, col0=1)
    win_k, win_v = win[0], win[1]
    gla_y, st_p = _gla_prompt(gla_in, log_a, gng, tb=512, chunk=64, sub=16)
    h_p, hn_p = _outproj(att, gla_y, xp, g1_p, ang, w_out_b, n2g, sc2_p, sh2_p, tm=512)
    y_p = _ffn(h_p, hn_p, g2_p, fg, w_up_b, w_down_b, tm=1024, tf=512)

    xs = x_sample.reshape(rows_s, D_MODEL)
    cs_s, sn_s = _rope_tables(jnp.tile(PAST_LEN + jnp.arange(t_s), n_s))
    gla_in_s, log_a_s = _inproj(xs, n1g, sc1_s, sh1_s, w_main, w_ag, gate_w, gate_b, cs_s, sn_s,
                                tm=rows_s, emit_att=False)
    qkv_s = _qkv_rows(xs, n1g, sc1_s, sh1_s, w_main, cs_s, sn_s,
                      tm=rows_s, first_row=0, rows=rows_s, col0=0)
    new_k, new_v = qkv_s[1], qkv_s[2]
    wc = cache_win_k.shape[2]
    cache_t = lambda c: jnp.transpose(c[0], (0, 2, 3, 1)).reshape(n_s, ATT_WIDTH, wc)
    att_tok = _sample_attention(qkv_s[0], new_k, new_v, cache_t(cache_win_k), cache_t(cache_win_v),
                                steps=t_s)
    att_s = jnp.transpose(att_tok.reshape(rows_s, HEAD_PAIRS, LANES), (1, 0, 2)).astype(BF16)
    gla_y_s, st_s = _gla_sample(gla_in_s.astype(F32), log_a_s, gng, state_gla[0], steps=t_s,
                                group=4 if n_s % 4 == 0 else 1)
    h_s, hn_s = _outproj(att_s, gla_y_s.astype(BF16), xs, g1_s, ang, w_out_b, n2g, sc2_s, sh2_s,
                         tm=rows_s)
    y_s = _ffn(h_s, hn_s, g2_s, fg, w_up_b, w_down_b, tm=rows_s, tf=2048)

    return (
        y_p[None],
        y_s.reshape(n_s, t_s, D_MODEL),
        win_k.reshape(1, n_p, wb_p, ATT_HEADS, ATT_HEAD_DIM),
        win_v.reshape(1, n_p, wb_p, ATT_HEADS, ATT_HEAD_DIM),
        st_p[None, None],
        new_k.reshape(1, n_s, t_s, ATT_HEADS, ATT_HEAD_DIM),
        new_v.reshape(1, n_s, t_s, ATT_HEADS, ATT_HEAD_DIM),
        st_s[None],
    )
```

```python
import functools

import jax
import jax.numpy as jnp
from jax import lax
from jax.experimental import pallas as pl
from jax.experimental.pallas import tpu as pltpu

F32 = jnp.float32
BF16 = jnp.bfloat16

D_MODEL = 2048
ATT_WIDTH = 1024
ATT_HEADS = 16
ATT_HEAD_DIM = 64
ROT_DIM = 16
ROPE_THETA = 500000.0
DILATIONS = (1, 4, 16)
BAND = 128
MAX_WINDOW = 2048
GLA_HEADS = 4
GLA_DK = 128
GLA_DV = 256
GLA_WIDTH = 1024
GLA_GATE_RANK = 16
GLA_TAU = 16.0
D_FF = 8192
EPS = 1e-6
PAST_LEN = 16384
NEG = -1e30

LANES = 128
HEAD_PAIRS = ATT_WIDTH // LANES
VMEM_LIMIT = 60000 * 1024


def _cparams(sem):
    return pltpu.CompilerParams(dimension_semantics=sem, vmem_limit_bytes=VMEM_LIMIT)


def _dot(a, b):
    return jnp.dot(a, b, preferred_element_type=F32)


def _dot_nt(a, b):
    return lax.dot_general(a, b, (((1,), (1,)), ((), ())), preferred_element_type=F32)


def _dot_tn(a, b):
    return lax.dot_general(a, b, (((0,), (0,)), ((), ())), preferred_element_type=F32)


def _silu(x):
    return x / (1.0 + jnp.exp(-x))


def _ada_kernel(c_ref, w_ref, b_ref, o_ref):
    s = _silu(c_ref[...]).astype(BF16)
    o_ref[...] = _dot(s, w_ref[...].astype(BF16)) + b_ref[...]


def _ada(c, w_ada, b_ada):
    rows = c.shape[0]
    n = w_ada.shape[1]
    tn = 1024
    return pl.pallas_call(
        _ada_kernel,
        out_shape=jax.ShapeDtypeStruct((rows, n), F32),
        grid=(n // tn,),
        in_specs=[
            pl.BlockSpec((rows, D_MODEL), lambda j: (0, 0)),
            pl.BlockSpec((D_MODEL, tn), lambda j: (0, j)),
            pl.BlockSpec((1, tn), lambda j: (0, j)),
        ],
        out_specs=pl.BlockSpec((rows, tn), lambda j: (0, j)),
        compiler_params=_cparams(("arbitrary",)),
        name="ada",
    )(c, w_ada, b_ada)


def _rope(slab, cs, sn):
    d = lax.broadcasted_iota(jnp.int32, (1, LANES), 1) % ATT_HEAD_DIM
    half = ROT_DIM // 2
    from_hi = jnp.where(d < half, -1.0, 0.0)
    from_lo = jnp.where((d >= half) & (d < ROT_DIM), 1.0, 0.0)
    return (slab * cs + pltpu.roll(slab, LANES - half, 1) * (sn * from_hi)
            + pltpu.roll(slab, half, 1) * (sn * from_lo))


def _norm_mod(x, g, sc, sh):
    rs = lax.rsqrt(jnp.mean(x * x, axis=-1, keepdims=True) + EPS)
    return x * rs * g * (1.0 + sc) + sh


def _inproj_kernel(*refs, emit_att):
    (x_ref, g_ref, sc_ref, sh_ref, w_ref, wag_ref, gw_ref, gb_ref, cs_ref, sn_ref), refs = (
        refs[:10], refs[10:])
    if emit_att:
        att_refs, refs = refs[:9], refs[9:]
    gla_ref, la_ref, xn_ref = refs[:3]
    j = pl.program_id(1)
    ii = pl.program_id(2)
    col = j if emit_att else j + 3

    @pl.when(j == 0)
    def _():
        xnb = _norm_mod(x_ref[...], g_ref[...], sc_ref[...], sh_ref[...]).astype(BF16)
        xn_ref[ii] = xnb
        ag = _dot_nt(xnb, wag_ref[...])
        z = _dot(ag.astype(BF16), gw_ref[...]) + gb_ref[...]
        log_sig = jnp.minimum(z, 0.0) - jnp.log(1.0 + jnp.exp(-jnp.abs(z)))
        la_ref[...] = log_sig / GLA_TAU

    if emit_att:
        s0_ref, s1_ref = refs[3:5]
        tm = x_ref.shape[0]
        n4, n16 = tm // 4, tm // 16
        group = 2 * LANES

        def emit(out1, out4, out16, slab_fn):
            for hp in range(HEAD_PAIRS):
                if hp % 2 == 0:
                    acc = _dot_nt(xn_ref[ii], w_ref[LANES * hp:LANES * hp + group, :])
                slab = slab_fn(acc[:, LANES * (hp % 2):LANES * (hp % 2 + 1)])
                out1[hp] = slab.astype(BF16)
                s0_ref[hp] = slab
                for r in range(4):
                    part = s0_ref[hp, pl.ds(r, n4, stride=4), :]
                    out4[hp, :, LANES * r:LANES * (r + 1)] = part.astype(BF16)
                    s1_ref[hp, n4 * r:n4 * (r + 1), :] = part
                for r in range(16):
                    part = s1_ref[hp, pl.ds(n4 * (r % 4) + r // 4, n16, stride=4), :]
                    out16[hp, :, LANES * r:LANES * (r + 1)] = part.astype(BF16)

        @pl.when(col == 0)
        def _():
            emit(*att_refs[0:3],
                 lambda s: _rope(s, cs_ref[...], sn_ref[...]) * (ATT_HEAD_DIM ** -0.5))

        @pl.when(col == 1)
        def _():
            emit(*att_refs[3:6], lambda s: _rope(s, cs_ref[...], sn_ref[...]))

        @pl.when(col == 2)
        def _():
            emit(*att_refs[6:9], lambda s: s)

    @pl.when(col == 3)
    def _():
        acc = _dot_nt(xn_ref[ii], w_ref[...])
        half = GLA_HEADS * GLA_DK
        gla_ref[:, :half] = (acc[:, :half] * (GLA_DK ** -0.5)).astype(BF16)
        gla_ref[:, half:] = acc[:, half:].astype(BF16)

    @pl.when(col >= 4)
    def _():
        gla_ref[...] = _dot_nt(xn_ref[ii], w_ref[...]).astype(BF16)


def _mod_spec(mod_rows, tm):
    if mod_rows == 1:
        return pl.BlockSpec((1, D_MODEL), lambda i, j: (0, 0))
    return pl.BlockSpec((tm, D_MODEL), lambda i, j: (i, 0))


def _inproj(x, g, sc, sh, w_main, w_ag, gate_w, gate_b, cs, sn, *, tm, emit_att):
    t = x.shape[0]
    col0 = 0 if emit_att else 3
    pair = 2 if (t // tm) % 2 == 0 else 1

    def rows_at(lo, hi):
        def tile(io, j, ii):
            return pair * io + jnp.where(j < lo, 0, jnp.where(j <= hi, ii, pair - 1))
        return tile

    def row_spec(block, lo, hi):
        tile = rows_at(lo, hi)
        return pl.BlockSpec(block, lambda io, j, ii: (tile(io, j, ii), 0))

    const = lambda block: pl.BlockSpec(block, lambda io, j, ii: (0, 0))
    mod = const((1, D_MODEL)) if sc.shape[0] == 1 else row_spec((tm, D_MODEL), 0, 0)
    out_shape, out_specs = [], []
    scratch = [pltpu.VMEM((pair, tm, D_MODEL), BF16)]
    if emit_att:
        for col in range(3):
            tile = rows_at(col, col)
            for d in DILATIONS:
                out_shape.append(jax.ShapeDtypeStruct((HEAD_PAIRS, t // d, d * LANES), BF16))
                out_specs.append(pl.BlockSpec(
                    (HEAD_PAIRS, tm // d, d * LANES),
                    lambda io, j, ii, tile=tile: (0, tile(io, j, ii), 0)))
        scratch += [pltpu.VMEM((HEAD_PAIRS, tm, LANES), F32) for _ in range(2)]
    gla_tile = rows_at(3 - col0, 5 - col0)
    out_shape += [jax.ShapeDtypeStruct((t, 3 * 1024), BF16),
                  jax.ShapeDtypeStruct((t, GLA_HEADS * GLA_DK), F32)]
    out_specs += [pl.BlockSpec((tm, 1024), lambda io, j, ii: (gla_tile(io, j, ii),
                                                              jnp.maximum(j + col0 - 3, 0))),
                  row_spec((tm, GLA_HEADS * GLA_DK), 0, 0)]
    return pl.pallas_call(
        functools.partial(_inproj_kernel, emit_att=emit_att),
        out_shape=tuple(out_shape),
        grid=(t // (pair * tm), 6 - col0, pair),
        in_specs=[
            row_spec((tm, D_MODEL), 0, 0),
            const((1, D_MODEL)),
            mod, mod,
            pl.BlockSpec((1024, D_MODEL), lambda io, j, ii: (j + col0, 0)),
            pl.BlockSpec((GLA_GATE_RANK, D_MODEL), lambda io, j, ii: (6 * 1024 // GLA_GATE_RANK, 0)),
            const((GLA_GATE_RANK, GLA_HEADS * GLA_DK)),
            const((1, GLA_HEADS * GLA_DK)),
            row_spec((tm, LANES), 0, 1),
            row_spec((tm, LANES), 0, 1),
        ],
        out_specs=tuple(out_specs),
        scratch_shapes=scratch,
        compiler_params=_cparams(("arbitrary", "arbitrary", "arbitrary")),
        name="inproj_att" if emit_att else "inproj_gla",
    )(x, g, sc, sh, w_main, w_ag, gate_w, gate_b, cs, sn)


def _qkv_rows_kernel(x_ref, g_ref, sc_ref, sh_ref, w_ref, cs_ref, sn_ref, o_ref, xn_ref, *, col0):
    j = pl.program_id(1)

    @pl.when(j == 0)
    def _():
        xn_ref[...] = _norm_mod(x_ref[...], g_ref[...], sc_ref[...], sh_ref[...]).astype(BF16)

    acc = _dot_nt(xn_ref[...], w_ref[...])
    col = j + col0

    @pl.when(col <= 1)
    def _():
        scale = jnp.where(col == 0, ATT_HEAD_DIM ** -0.5, 1.0)
        for hp in range(HEAD_PAIRS):
            lanes = slice(LANES * hp, LANES * (hp + 1))
            o_ref[0, :, lanes] = _rope(acc[:, lanes], cs_ref[...], sn_ref[...]) * scale

    @pl.when(col == 2)
    def _():
        o_ref[0] = acc


def _qkv_rows(x, g, sc, sh, w_main, cs, sn, *, tm, first_row, rows, col0):
    first = first_row // tm
    assert sc.shape[0] == 1 or first == 0
    ncols = 3 - col0
    mod = _mod_spec(sc.shape[0], tm)
    return pl.pallas_call(
        functools.partial(_qkv_rows_kernel, col0=col0),
        out_shape=jax.ShapeDtypeStruct((ncols, rows, ATT_WIDTH), F32),
        grid=(rows // tm, ncols),
        in_specs=[
            pl.BlockSpec((tm, D_MODEL), lambda i, j: (i + first, 0)),
            pl.BlockSpec((1, D_MODEL), lambda i, j: (0, 0)),
            mod, mod,
            pl.BlockSpec((1024, D_MODEL), lambda i, j: (j + col0, 0)),
            pl.BlockSpec((tm, LANES), lambda i, j: (i + first, 0)),
            pl.BlockSpec((tm, LANES), lambda i, j: (i + first, 0)),
        ],
        out_specs=pl.BlockSpec((1, tm, ATT_WIDTH), lambda i, j: (j, i, 0)),
        scratch_shapes=[pltpu.VMEM((tm, D_MODEL), BF16)],
        compiler_params=_cparams(("arbitrary", "arbitrary")),
        name="qkv_rows",
    )(x, g, sc, sh, w_main, cs, sn)


def _rope_tables(pos):
    half = ROT_DIM // 2
    inv_freq = ROPE_THETA ** (-jnp.arange(half, dtype=F32) / half)
    per_head = jnp.concatenate([inv_freq, inv_freq, jnp.zeros((ATT_HEAD_DIM - ROT_DIM,), F32)])
    ang = pos.astype(F32)[:, None] * jnp.tile(per_head, LANES // ATT_HEAD_DIM)[None, :]
    return jnp.cos(ang), jnp.sin(ang)


def _attn_kernel(*refs, n_cast):
    (q1, k1c, v1c, k1p, v1p, q4, k4c, v4c, k4p, v4p, q16, k16c, v16c, k16p, v16p), refs = (
        refs[:15], refs[15:])
    cast_in, o_ref, cast_out = refs[:n_cast], refs[n_cast], refs[n_cast + 1:2 * n_cast + 1]
    a1, m1, l1, a4, m4, l4, t16, s_scr, p_scr, bias_ref, o_scr = refs[2 * n_cast + 1:]
    for src, dst in zip(cast_in, cast_out):
        dst[...] = src[...].astype(dst.dtype)

    first_block = pl.program_id(1) == 0
    lane = lax.broadcasted_iota(jnp.int32, (1, LANES), 1)
    first = lane < ATT_HEAD_DIM
    head0 = jnp.where(first, 1.0, 0.0).astype(BF16)
    head1 = jnp.where(first, 0.0, 1.0).astype(BF16)
    ones = jnp.ones((2 * BAND, LANES), BF16)

    qi = lax.broadcasted_iota(jnp.int32, (2 * BAND, 2 * BAND), 0) % BAND
    kk = lax.broadcasted_iota(jnp.int32, (2 * BAND, 2 * BAND), 1)
    full = jnp.where(jnp.where(kk < BAND, kk - qi, qi - (kk - BAND)) >= 0, 0.0, NEG)
    bias_ref[0] = full
    bias_ref[1] = jnp.where(kk < BAND, full + jnp.where(first_block, NEG, 0.0), full)

    n1 = q1.shape[1] // BAND
    rows4 = q4.shape[1]
    n4 = rows4 // BAND

    def window(cur, prev, i, cols):
        if i == 0:
            return jnp.concatenate([prev[0, :, cols], cur[0, :BAND, cols]], axis=0)
        return cur[0, BAND * (i - 1):BAND * (i + 1), cols]

    jobs = []
    everything = slice(None)
    for i in range(n1):
        jobs.append(dict(q=(q1, i, everything), k=(k1c, k1p), v=(v1c, v1p), bias=int(i == 0),
                         dst=(a1, m1, l1, BAND * i)))
    for r in range(4):
        cols = slice(LANES * r, LANES * (r + 1))
        for i in range(n4):
            jobs.append(dict(q=(q4, i, cols), k=(k4c, k4p), v=(v4c, v4p), bias=int(i == 0),
                             dst=(a4, m4, l4, rows4 * r + BAND * i)))
    for r in range(16):
        cols = slice(LANES * r, LANES * (r + 1))
        jobs.append(dict(q=(q16, 0, cols), k=(k16c, k16p), v=(v16c, v16p), bias=1, merge=r))

    def scores(job, slot):
        ref, i, cols = job["q"]
        q = ref[0, BAND * i:BAND * (i + 1), cols]
        q2 = jnp.concatenate([q * head0, q * head1], axis=0)
        s_scr[slot] = _dot_nt(q2, window(*job["k"], i, cols))

    def softmax(job, slot):
        s = s_scr[slot] + bias_ref[job["bias"]]
        m = jnp.max(s, axis=-1, keepdims=True)
        p_scr[slot] = jnp.exp(s - m).astype(BF16)
        m = jnp.where(first, m[:BAND], m[BAND:])
        if "merge" in job:
            t16[slot] = m
        else:
            job["dst"][1][pl.ds(job["dst"][3], BAND), :] = m

    def values(job, slot):
        _, i, cols = job["q"]
        vw = jnp.concatenate([window(*job["v"], i, cols), ones], axis=1)
        o2 = _dot(p_scr[slot], vw)
        acc = jnp.where(first, o2[:BAND, :LANES], o2[BAND:, :LANES])
        den = jnp.where(first, o2[:BAND, LANES:], o2[BAND:, LANES:])
        if "merge" not in job:
            a_ref, _, l_ref, row = job["dst"]
            a_ref[pl.ds(row, BAND), :] = acc
            l_ref[pl.ds(row, BAND), :] = den
            return
        r = job["merge"]
        sl1 = pl.ds(r, BAND, stride=16)
        sl4 = pl.ds((r % 4) * rows4 + r // 4, BAND, stride=4)
        ma, mb, mc = m1[sl1, :], m4[sl4, :], t16[slot]
        m = jnp.maximum(jnp.maximum(ma, mb), mc)
        wa, wb, wc = jnp.exp(ma - m), jnp.exp(mb - m), jnp.exp(mc - m)
        num = wa * a1[sl1, :] + wb * a4[sl4, :] + wc * acc
        dsum = wa * l1[sl1, :] + wb * l4[sl4, :] + wc * den
        o_scr[sl1, :] = num / dsum

    slots = s_scr.shape[0]
    for step in range(len(jobs) + 2):
        if step < len(jobs):
            scores(jobs[step], step % slots)
        if 1 <= step <= len(jobs):
            softmax(jobs[step - 1], (step - 1) % slots)
        if step >= 2:
            values(jobs[step - 2], (step - 2) % slots)
    o_ref[0] = o_scr[...].astype(o_ref.dtype)


def _prompt_attention(views, weights):
    hp, t, _ = views[0].shape
    sb = 16 * BAND
    ns = t // sb
    in_specs, args = [], []
    for di, d in enumerate(DILATIONS):
        rows, width = sb // d, d * LANES
        cur = pl.BlockSpec((1, rows, width), lambda h, s: (h, s, 0))
        nb = rows // BAND
        prev = pl.BlockSpec((1, BAND, width),
                            lambda h, s, nb=nb: (h, jnp.maximum(s * nb - 1, 0), 0))
        q, k, v = views[di], views[3 + di], views[6 + di]
        in_specs += [cur, cur, cur, prev, prev]
        args += [q, k, v, k, v]
    cast_specs = []
    for w in weights:
        slab = w.shape[0] // (hp * ns)
        assert slab * hp * ns == w.shape[0] and slab % 16 == 0
        cast_specs.append(pl.BlockSpec((slab, w.shape[1]), lambda h, s: (h * ns + s, 0)))
    tile = (2 * BAND, 2 * BAND)
    return pl.pallas_call(
        functools.partial(_attn_kernel, n_cast=len(weights)),
        out_shape=tuple([jax.ShapeDtypeStruct((hp, t, LANES), BF16)]
                        + [jax.ShapeDtypeStruct(w.shape, BF16) for w in weights]),
        grid=(hp, ns),
        in_specs=in_specs + cast_specs,
        out_specs=tuple([pl.BlockSpec((1, sb, LANES), lambda h, s: (h, s, 0))] + cast_specs),
        scratch_shapes=(
            [pltpu.VMEM((sb, LANES), F32) for _ in range(6)]
            + [pltpu.VMEM((3, BAND, LANES), F32),
               pltpu.VMEM((3,) + tile, F32),
               pltpu.VMEM((3,) + tile, BF16),
               pltpu.VMEM((2,) + tile, F32),
               pltpu.VMEM((sb, LANES), F32)]),
        compiler_params=_cparams(("arbitrary", "arbitrary")),
        name="prompt_attn",
    )(*args, *weights)


def _gla_chunk(q, k, v, la, st_ref, h, tri, sub):
    c = q.shape[0]
    la_hi = la.astype(BF16)
    la_lo = (la - la_hi.astype(F32)).astype(BF16)
    b = _dot(tri, la_hi) + _dot(tri, la_lo)
    st = st_ref[h]
    o = _dot_nt((q * jnp.exp(b)).astype(BF16), st.astype(BF16))
    srow = lax.broadcasted_iota(jnp.int32, (c, GLA_DK), 0)
    att_rows = []
    for i in range(c // sub):
        lo, hi = i * sub, (i + 1) * sub
        ref = b[lo:lo + 1, :]
        qh = (q[lo:hi] * jnp.exp(b[lo:hi] - ref)).astype(BF16)
        kh = (k * jnp.exp(jnp.where(srow < hi, ref - b, 0.0))).astype(BF16)
        a = _dot_nt(qh, kh)
        ti = lax.broadcasted_iota(jnp.int32, (sub, c), 0) + lo
        si = lax.broadcasted_iota(jnp.int32, (sub, c), 1)
        att_rows.append(jnp.where(si <= ti, a, 0.0))
    att = att_rows[0] if len(att_rows) == 1 else jnp.concatenate(att_rows, axis=0)
    o = o + _dot(att.astype(BF16), v)
    b_last = b[c - 1:c, :]
    kbar = (k * jnp.exp(b_last - b)).astype(BF16)
    st_ref[h] = st * jnp.exp(b_last) + _dot_tn(v, kbar)
    return o


def _gla_out(o, g, gg):
    rs = lax.rsqrt(jnp.mean(o * o, axis=-1, keepdims=True) + EPS)
    return o * rs * g * _silu(gg)


def _tri(c):
    r = lax.broadcasted_iota(jnp.int32, (c, c), 0)
    s = lax.broadcasted_iota(jnp.int32, (c, c), 1)
    return jnp.where(s <= r, 1.0, 0.0).astype(BF16)


def _gla_prompt_kernel(qk_ref, v_ref, gg_ref, la_ref, g_ref, y_ref, s_ref, st_ref,
                       qd_ref, kl_ref, qs_ref, ks_ref, dec_ref, att_ref, o_ref, *, chunk, sub):
    i = pl.program_id(0)

    @pl.when(i == 0)
    def _():
        st_ref[...] = jnp.zeros_like(st_ref)

    tb = qk_ref.shape[0]
    nchunks, nsub = tb // chunk, chunk // sub
    half = GLA_HEADS * GLA_DK

    r = lax.broadcasted_iota(jnp.int32, (tb, tb), 0)
    c = lax.broadcasted_iota(jnp.int32, (tb, tb), 1)
    start = r - jnp.bitwise_and(r, chunk - 1)
    tri = jnp.where(c <= r, jnp.where(c >= start, 1.0, 0.0), 0.0).astype(BF16)
    la = la_ref[...]
    la_hi = la.astype(BF16)
    la_lo = (la - la_hi.astype(F32)).astype(BF16)
    b = _dot(tri, la_hi) + _dot(tri, la_lo)
    q = qk_ref[:, :half].astype(F32)
    k = qk_ref[:, half:].astype(F32)
    qd_ref[...] = (q * jnp.exp(b)).astype(BF16)
    srow = lax.broadcasted_iota(jnp.int32, (chunk, half), 0)
    for ci in range(nchunks):
        lo = ci * chunk
        bc, kc_ = b[lo:lo + chunk], k[lo:lo + chunk]
        b_last = bc[chunk - 1:chunk]
        dec_ref[ci:ci + 1, :] = jnp.exp(b_last)
        kl_ref[lo:lo + chunk, :] = (kc_ * jnp.exp(b_last - bc)).astype(BF16)
        for si in range(nsub):
            s_lo, s_hi = si * sub, (si + 1) * sub
            ref = bc[s_lo:s_lo + 1]
            qs_ref[lo + s_lo:lo + s_hi, :] = (
                q[lo + s_lo:lo + s_hi] * jnp.exp(bc[s_lo:s_hi] - ref)).astype(BF16)
            ks_ref[si, lo:lo + chunk, :] = (
                kc_ * jnp.exp(jnp.where(srow < s_hi, ref - bc, 0.0))).astype(BF16)

    ti = lax.broadcasted_iota(jnp.int32, (chunk, chunk), 0)
    si_ = lax.broadcasted_iota(jnp.int32, (chunk, chunk), 1)
    causal = si_ <= ti
    jobs = [(ci, h) for ci in range(nchunks) for h in range(GLA_HEADS)]

    def intra(job, slot):
        ci, h = job
        lo, kc = ci * chunk, slice(GLA_DK * h, GLA_DK * (h + 1))
        parts = [_dot_nt(qs_ref[lo + s * sub:lo + (s + 1) * sub, kc], ks_ref[s, lo:lo + chunk, kc])
                 for s in range(nsub)]
        att_ref[slot] = jnp.where(causal, jnp.concatenate(parts, axis=0), 0.0).astype(BF16)

    def state(job, slot):
        ci, h = job
        rows = slice(ci * chunk, (ci + 1) * chunk)
        kc, vc = slice(GLA_DK * h, GLA_DK * (h + 1)), slice(GLA_DV * h, GLA_DV * (h + 1))
        st = st_ref[h]
        v = v_ref[rows, vc]
        o_ref[slot] = _dot_nt(qd_ref[rows, kc], st.astype(BF16)) + _dot(att_ref[slot], v)
        st_ref[h] = st * dec_ref[ci:ci + 1, kc] + _dot_tn(v, kl_ref[rows, kc])

    def finish(job, slot):
        ci, h = job
        rows, vc = slice(ci * chunk, (ci + 1) * chunk), slice(GLA_DV * h, GLA_DV * (h + 1))
        y_ref[rows, vc] = _gla_out(o_ref[slot], g_ref[:, vc],
                                   gg_ref[rows, vc].astype(F32)).astype(y_ref.dtype)

    slots = att_ref.shape[0]
    for step in range(len(jobs) + 2):
        if step < len(jobs):
            intra(jobs[step], step % slots)
        if 1 <= step <= len(jobs):
            state(jobs[step - 1], (step - 1) % slots)
        if step >= 2:
            finish(jobs[step - 2], (step - 2) % slots)

    @pl.when(i == pl.num_programs(0) - 1)
    def _():
        for h in range(GLA_HEADS):
            s_ref[h] = st_ref[h].T


def _gla_prompt(gla_in, log_a, norm_g, *, tb, chunk, sub):
    t = gla_in.shape[0]
    return pl.pallas_call(
        functools.partial(_gla_prompt_kernel, chunk=chunk, sub=sub),
        out_shape=(
            jax.ShapeDtypeStruct((t, GLA_WIDTH), BF16),
            jax.ShapeDtypeStruct((GLA_HEADS, GLA_DK, GLA_DV), F32),
        ),
        grid=(t // tb,),
        in_specs=[
            pl.BlockSpec((tb, 1024), lambda i: (i, 0)),
            pl.BlockSpec((tb, 1024), lambda i: (i, 1)),
            pl.BlockSpec((tb, 1024), lambda i: (i, 2)),
            pl.BlockSpec((tb, GLA_HEADS * GLA_DK), lambda i: (i, 0)),
            pl.BlockSpec((1, GLA_WIDTH), lambda i: (0, 0)),
        ],
        out_specs=(
            pl.BlockSpec((tb, GLA_WIDTH), lambda i: (i, 0)),
            pl.BlockSpec((GLA_HEADS, GLA_DK, GLA_DV), lambda i: (0, 0, 0)),
        ),
        scratch_shapes=[
            pltpu.VMEM((GLA_HEADS, GLA_DV, GLA_DK), F32),
            pltpu.VMEM((tb, GLA_HEADS * GLA_DK), BF16),
            pltpu.VMEM((tb, GLA_HEADS * GLA_DK), BF16),
            pltpu.VMEM((tb, GLA_HEADS * GLA_DK), BF16),
            pltpu.VMEM((chunk // sub, tb, GLA_HEADS * GLA_DK), BF16),
            pltpu.VMEM((tb // chunk, GLA_HEADS * GLA_DK), F32),
            pltpu.VMEM((3, chunk, chunk), BF16),
            pltpu.VMEM((3, chunk, GLA_DV), F32),
        ],
        compiler_params=_cparams(("arbitrary",)),
        name="gla_prompt",
    )(gla_in, gla_in, gla_in, log_a, norm_g)


def _gla_sample_kernel(qk_ref, v_ref, gg_ref, la_ref, g_ref, s0_ref, y_ref, s_ref, st_ref, *, steps):
    group = s0_ref.shape[0]
    c = 2 * steps
    tri = _tri(c)
    half = GLA_HEADS * GLA_DK
    pad = lambda a: jnp.concatenate([a, jnp.zeros_like(a)], axis=0)
    for n in range(group):
        rows = slice(n * steps, (n + 1) * steps)
        for h in range(GLA_HEADS):
            idx = n * GLA_HEADS + h
            kc = slice(GLA_DK * h, GLA_DK * (h + 1))
            vc = slice(GLA_DV * h, GLA_DV * (h + 1))
            st_ref[idx] = s0_ref[n, h].T
            q = pad(qk_ref[rows, kc])
            k = pad(qk_ref[rows, half + GLA_DK * h:half + GLA_DK * (h + 1)])
            v = pad(v_ref[rows, vc]).astype(BF16)
            o = _gla_chunk(q, k, v, pad(la_ref[rows, kc]), st_ref, idx, tri, c)
            y_ref[rows, vc] = _gla_out(o[:steps], g_ref[:, vc], gg_ref[rows, vc])
            s_ref[n, h] = st_ref[idx].T


def _gla_sample(gla_in, log_a, norm_g, state, *, steps, group):
    t = gla_in.shape[0]
    rows = steps * group
    return pl.pallas_call(
        functools.partial(_gla_sample_kernel, steps=steps),
        out_shape=(
            jax.ShapeDtypeStruct((t, GLA_WIDTH), F32),
            jax.ShapeDtypeStruct(state.shape, F32),
        ),
        grid=(t // rows,),
        in_specs=[
            pl.BlockSpec((rows, 1024), lambda n: (n, 0)),
            pl.BlockSpec((rows, 1024), lambda n: (n, 1)),
            pl.BlockSpec((rows, 1024), lambda n: (n, 2)),
            pl.BlockSpec((rows, GLA_HEADS * GLA_DK), lambda n: (n, 0)),
            pl.BlockSpec((1, GLA_WIDTH), lambda n: (0, 0)),
            pl.BlockSpec((group, GLA_HEADS, GLA_DK, GLA_DV), lambda n: (n, 0, 0, 0)),
        ],
        out_specs=(
            pl.BlockSpec((rows, GLA_WIDTH), lambda n: (n, 0)),
            pl.BlockSpec((group, GLA_HEADS, GLA_DK, GLA_DV), lambda n: (n, 0, 0, 0)),
        ),
        scratch_shapes=[pltpu.VMEM((group * GLA_HEADS, GLA_DV, GLA_DK), F32)],
        compiler_params=_cparams(("arbitrary",)),
        name="gla_sample",
    )(gla_in, gla_in, gla_in, log_a, norm_g, state)


def _multiplicity(delta):
    total = jnp.zeros(delta.shape, F32)
    for d in DILATIONS:
        assert d & (d - 1) == 0
        hit = jnp.where(jnp.bitwise_and(delta, d - 1) == 0, 1.0, 0.0)
        total = total + jnp.where(delta <= BAND * d, hit, 0.0)
    return jnp.where(delta >= 0, total, 0.0)


def _sample_attn_kernel(q_ref, kn_ref, vn_ref, ckt_ref, cvt_ref, o_ref):
    steps = q_ref.shape[0]
    rows = ATT_HEADS * steps
    cache_len = ckt_ref.shape[2]
    qt = jnp.concatenate([q_ref[...]] * ATT_HEADS, axis=0)
    rh = lax.broadcasted_iota(jnp.int32, (rows, ATT_WIDTH), 0) // steps
    ch = lax.broadcasted_iota(jnp.int32, (rows, ATT_WIDTH), 1) // ATT_HEAD_DIM
    own = rh == ch
    qb = jnp.where(own, qt, 0.0).astype(BF16)
    pad = jnp.zeros((LANES - steps, ATT_WIDTH), F32)
    kn = jnp.concatenate([kn_ref[...], pad], axis=0).astype(BF16)
    vn = jnp.concatenate([vn_ref[...], pad], axis=0).astype(BF16)
    assert steps & (steps - 1) == 0
    t_c = jnp.bitwise_and(lax.broadcasted_iota(jnp.int32, (rows, cache_len), 0), steps - 1)
    mc = _multiplicity(cache_len + t_c - lax.broadcasted_iota(jnp.int32, (rows, cache_len), 1))
    t_n = jnp.bitwise_and(lax.broadcasted_iota(jnp.int32, (rows, LANES), 0), steps - 1)
    new = lax.broadcasted_iota(jnp.int32, (rows, LANES), 1)
    mn = jnp.where(new < steps, _multiplicity(t_n - new), 0.0)
    sc = jnp.where(mc > 0.0, _dot(qb, ckt_ref[0].astype(BF16)), NEG)
    sn = jnp.where(mn > 0.0, _dot_nt(qb, kn), NEG)
    m = jnp.maximum(jnp.max(sc, axis=-1, keepdims=True), jnp.max(sn, axis=-1, keepdims=True))
    pc = mc * jnp.exp(sc - m)
    pn = mn * jnp.exp(sn - m)
    den = jnp.sum(pc, axis=-1, keepdims=True) + jnp.sum(pn, axis=-1, keepdims=True)
    o = _dot_nt(pc.astype(BF16), cvt_ref[0].astype(BF16)) + _dot(pn.astype(BF16), vn)
    o = jnp.where(own, o / den, 0.0)
    o_ref[...] = jnp.sum(o.reshape(ATT_HEADS, steps, ATT_WIDTH), axis=0)


def _sample_attention(q, k_new, v_new, cache_kt, cache_vt, *, steps):
    nb, _, cache_len = cache_kt.shape
    tok = pl.BlockSpec((steps, ATT_WIDTH), lambda n: (n, 0))
    cache = pl.BlockSpec((1, ATT_WIDTH, cache_len), lambda n: (n, 0, 0))
    return pl.pallas_call(
        _sample_attn_kernel,
        out_shape=jax.ShapeDtypeStruct((nb * steps, ATT_WIDTH), F32),
        grid=(nb,),
        in_specs=[tok, tok, tok, cache, cache],
        out_specs=tok,
        compiler_params=_cparams(("arbitrary",)),
        name="sample_attn",
    )(q, k_new, v_new, cache_kt, cache_vt)


def _outproj_kernel(att_ref, gla_ref, x_ref, gate_ref, ng_ref, w_ref, g2_ref, sc2_ref, sh2_ref,
                    h_ref, hn_ref):
    att = jnp.concatenate([att_ref[hp].astype(F32) for hp in range(HEAD_PAIRS)], axis=1)
    rs = lax.rsqrt(jnp.mean(att * att, axis=-1, keepdims=True) + EPS)
    att_y = (att * rs * ng_ref[...]).astype(BF16)
    mix = _dot(att_y, w_ref[:ATT_WIDTH, :]) + _dot(gla_ref[...], w_ref[ATT_WIDTH:, :])
    h = x_ref[...] + gate_ref[...] * mix
    h_ref[...] = h
    hn_ref[...] = _norm_mod(h, g2_ref[...], sc2_ref[...], sh2_ref[...]).astype(BF16)


def _outproj(att, gla_y, x, gate, att_norm_g, w_out, g2, sc2, sh2, *, tm):
    t = x.shape[0]
    mod_rows = gate.shape[0]
    mod_map = (lambda i: (0, 0)) if mod_rows == 1 else (lambda i: (i, 0))
    mod = pl.BlockSpec((1, D_MODEL) if mod_rows == 1 else (tm, D_MODEL), mod_map)
    row = pl.BlockSpec((tm, D_MODEL), lambda i: (i, 0))
    return pl.pallas_call(
        _outproj_kernel,
        out_shape=(jax.ShapeDtypeStruct((t, D_MODEL), F32),
                   jax.ShapeDtypeStruct((t, D_MODEL), BF16)),
        grid=(t // tm,),
        in_specs=[
            pl.BlockSpec((HEAD_PAIRS, tm, LANES), lambda i: (0, i, 0)),
            pl.BlockSpec((tm, GLA_WIDTH), lambda i: (i, 0)),
            row, mod,
            pl.BlockSpec((1, ATT_WIDTH), lambda i: (0, 0)),
            pl.BlockSpec((ATT_WIDTH + GLA_WIDTH, D_MODEL), lambda i: (0, 0)),
            pl.BlockSpec((1, D_MODEL), lambda i: (0, 0)),
            mod, mod,
        ],
        out_specs=(row, row),
        compiler_params=_cparams(("arbitrary",)),
        name="outproj",
    )(att, gla_y, x, gate, att_norm_g, w_out, g2, sc2, sh2)


def _ffn_kernel(h_ref, hn_ref, gate_ref, fg_ref, wu_ref, wd_ref, y_ref):
    f = pl.program_id(1)

    def down():
        u = jnp.maximum(_dot(hn_ref[...], wu_ref[...]), 0.0)
        return _dot((u * u).astype(BF16), wd_ref[...])

    @pl.when(f == 0)
    def _():
        y_ref[...] = down()

    @pl.when(f > 0)
    def _():
        y_ref[...] += down()

    @pl.when(f == pl.num_programs(1) - 1)
    def _():
        h2 = h_ref[...] + gate_ref[...] * y_ref[...]
        rs = lax.rsqrt(jnp.mean(h2 * h2, axis=-1, keepdims=True) + EPS)
        y_ref[...] = h2 * rs * fg_ref[...]


def _ffn(h, hn, gate, final_g, w_up, w_down, *, tm, tf):
    t = h.shape[0]
    row = pl.BlockSpec((tm, D_MODEL), lambda i, f: (i, 0))
    return pl.pallas_call(
        _ffn_kernel,
        out_shape=jax.ShapeDtypeStruct((t, D_MODEL), F32),
        grid=(t // tm, D_FF // tf),
        in_specs=[
            row, row,
            _mod_spec(gate.shape[0], tm),
            pl.BlockSpec((1, D_MODEL), lambda i, f: (0, 0)),
            pl.BlockSpec((D_MODEL, tf), lambda i, f: (0, f)),
            pl.BlockSpec((tf, D_MODEL), lambda i, f: (f, 0)),
        ],
        out_specs=row,
        compiler_params=_cparams(("arbitrary", "arbitrary")),
        name="ffn",
    )(h, hn, gate, final_g, w_up, w_down)


def _split_ada(ada):
    return [ada[:, D_MODEL * i:D_MODEL * (i + 1)] for i in range(6)]


def kernel(x_prompt, x_sample, c_prompt, c_sample, cache_win_k, cache_win_v, state_gla, w_in, gla_gate_w, gla_gate_b, att_norm_g, gla_norm_g, w_out, norm1_g, norm2_g, w_ada, b_ada, w_up, w_down, final_g):
    assert w_in.shape[0] == 1, "single-layer model"
    n_p, t_p, _ = x_prompt.shape
    n_s, t_s, _ = x_sample.shape
    assert n_p == 1
    rows_s = n_s * t_s
    wb_p = min(MAX_WINDOW, t_p)

    w_main = jnp.transpose(w_in[0]).astype(BF16)
    w_ag = w_main
    gate_w = gla_gate_w[0].astype(BF16)
    gate_b = gla_gate_b[0][None, :]
    n1g, n2g = norm1_g[0][None, :], norm2_g[0][None, :]
    ang, gng = att_norm_g[0][None, :], gla_norm_g[0][None, :]
    fg = final_g[None, :]

    c_rows = n_p + n_s
    c_pad = -c_rows % 8
    c_all = jnp.concatenate([c_prompt, c_sample, jnp.zeros((c_pad, D_MODEL), F32)], axis=0)
    ada = _ada(c_all, w_ada[0], b_ada[0][None, :])
    sh1_p, sc1_p, g1_p, sh2_p, sc2_p, g2_p = _split_ada(ada[:n_p])
    sh1_s, sc1_s, g1_s, sh2_s, sc2_s, g2_s = _split_ada(
        jnp.repeat(ada[n_p:c_rows], t_s, axis=0))

    xp = x_prompt[0]
    cs_p, sn_p = _rope_tables(jnp.arange(t_p))
    outs = _inproj(xp, n1g, sc1_p, sh1_p, w_main, w_ag, gate_w, gate_b, cs_p, sn_p,
                   tm=512, emit_att=True)
    att, w_out_b, w_up_b, w_down_b = _prompt_attention(outs[:9], (w_out[0], w_up[0], w_down[0]))
    gla_in, log_a = outs[9], outs[10]
    win = _qkv_rows(xp, n1g, sc1_p, sh1_p, w_main, cs_p, sn_p,
                    tm=1024, first_row=t_p - wb_p, rows=wb_p, col0=1)
    win_k, win_v = win[0], win[1]
    gla_y, st_p = _gla_prompt(gla_in, log_a, gng, tb=512, chunk=64, sub=16)
    h_p, hn_p = _outproj(att, gla_y, xp, g1_p, ang, w_out_b, n2g, sc2_p, sh2_p, tm=512)
    y_p = _ffn(h_p, hn_p, g2_p, fg, w_up_b, w_down_b, tm=1024, tf=512)

    xs = x_sample.reshape(rows_s, D_MODEL)
    cs_s, sn_s = _rope_tables(jnp.tile(PAST_LEN + jnp.arange(t_s), n_s))
    gla_in_s, log_a_s = _inproj(xs, n1g, sc1_s, sh1_s, w_main, w_ag, gate_w, gate_b, cs_s, sn_s,
                                tm=rows_s, emit_att=False)
    qkv_s = _qkv_rows(xs, n1g, sc1_s, sh1_s, w_main, cs_s, sn_s,
                      tm=rows_s, first_row=0, rows=rows_s, col0=0)
    new_k, new_v = qkv_s[1], qkv_s[2]
    wc = cache_win_k.shape[2]
    cache_t = lambda c: jnp.transpose(c[0], (0, 2, 3, 1)).reshape(n_s, ATT_WIDTH, wc)
    att_tok = _sample_attention(qkv_s[0], new_k, new_v, cache_t(cache_win_k), cache_t(cache_win_v),
                                steps=t_s)
    att_s = jnp.transpose(att_tok.reshape(rows_s, HEAD_PAIRS, LANES), (1, 0, 2)).astype(BF16)
    gla_y_s, st_s = _gla_sample(gla_in_s.astype(F32), log_a_s, gng, state_gla[0], steps=t_s,
                                group=4 if n_s % 4 == 0 else 1)
    h_s, hn_s = _outproj(att_s, gla_y_s.astype(BF16), xs, g1_s, ang, w_out_b, n2g, sc2_s, sh2_s,
                         tm=rows_s)
    y_s = _ffn(h_s, hn_s, g2_s, fg, w_up_b, w_down_b, tm=rows_s, tf=2048)

    return (
        y_p[None],
        y_s.reshape(n_s, t_s, D_MODEL),
        win_k.reshape(1, n_p, wb_p, ATT_HEADS, ATT_HEAD_DIM),
        win_v.reshape(1, n_p, wb_p, ATT_HEADS, ATT_HEAD_DIM),
        st_p[None, None],
        new_k.reshape(1, n_s, t_s, ATT_HEADS, ATT_HEAD_DIM),
        new_v.reshape(1, n_s, t_s, ATT_HEADS, ATT_HEAD_DIM),
        st_s[None],
    )
```

```python
import functools

import jax
import jax.numpy as jnp
from jax import lax
from jax.experimental import pallas as pl
from jax.experimental.pallas import tpu as pltpu

F32 = jnp.float32
BF16 = jnp.bfloat16

D_MODEL = 2048
ATT_WIDTH = 1024
ATT_HEADS = 16
ATT_HEAD_DIM = 64
ROT_DIM = 16
ROPE_THETA = 500000.0
DILATIONS = (1, 4, 16)
BAND = 128
MAX_WINDOW = 2048
GLA_HEADS = 4
GLA_DK = 128
GLA_DV = 256
GLA_WIDTH = 1024
GLA_GATE_RANK = 16
GLA_TAU = 16.0
D_FF = 8192
EPS = 1e-6
PAST_LEN = 16384
NEG = -1e30

LANES = 128
HEAD_PAIRS = ATT_WIDTH // LANES
VMEM_LIMIT = 60000 * 1024


def _cparams(sem):
    return pltpu.CompilerParams(dimension_semantics=sem, vmem_limit_bytes=VMEM_LIMIT)


def _dot(a, b):
    return jnp.dot(a, b, preferred_element_type=F32)


def _dot_nt(a, b):
    return lax.dot_general(a, b, (((1,), (1,)), ((), ())), preferred_element_type=F32)


def _dot_tn(a, b):
    return lax.dot_general(a, b, (((0,), (0,)), ((), ())), preferred_element_type=F32)


def _silu(x):
    return x / (1.0 + jnp.exp(-x))


def _ada_kernel(c_ref, w_ref, b_ref, o_ref):
    s = _silu(c_ref[...]).astype(BF16)
    o_ref[...] = _dot(s, w_ref[...].astype(BF16)) + b_ref[...]


def _ada(c, w_ada, b_ada):
    rows = c.shape[0]
    n = w_ada.shape[1]
    tn = 1024
    return pl.pallas_call(
        _ada_kernel,
        out_shape=jax.ShapeDtypeStruct((rows, n), F32),
        grid=(n // tn,),
        in_specs=[
            pl.BlockSpec((rows, D_MODEL), lambda j: (0, 0)),
            pl.BlockSpec((D_MODEL, tn), lambda j: (0, j)),
            pl.BlockSpec((1, tn), lambda j: (0, j)),
        ],
        out_specs=pl.BlockSpec((rows, tn), lambda j: (0, j)),
        compiler_params=_cparams(("arbitrary",)),
        name="ada",
    )(c, w_ada, b_ada)


def _rope(slab, cs, sn):
    d = lax.broadcasted_iota(jnp.int32, (1, LANES), 1) % ATT_HEAD_DIM
    half = ROT_DIM // 2
    from_hi = jnp.where(d < half, -1.0, 0.0)
    from_lo = jnp.where((d >= half) & (d < ROT_DIM), 1.0, 0.0)
    return (slab * cs + pltpu.roll(slab, LANES - half, 1) * (sn * from_hi)
            + pltpu.roll(slab, half, 1) * (sn * from_lo))


def _norm_mod(x, g, sc, sh):
    rs = lax.rsqrt(jnp.mean(x * x, axis=-1, keepdims=True) + EPS)
    return x * rs * g * (1.0 + sc) + sh


def _inproj_kernel(*refs, emit_att):
    (x_ref, g_ref, sc_ref, sh_ref, w_ref, wag_ref, gw_ref, gb_ref, cs_ref, sn_ref), refs = (
        refs[:10], refs[10:])
    if emit_att:
        (out1, out4, out16), refs = refs[:3], refs[3:]
    gla_ref, la_ref, xn_ref = refs[:3]
    j = pl.program_id(1)
    col = j if emit_att else j + 3

    @pl.when(j == 0)
    def _():
        xnb = _norm_mod(x_ref[...], g_ref[...], sc_ref[...], sh_ref[...]).astype(BF16)
        xn_ref[...] = xnb
        ag = _dot_nt(xnb, wag_ref[...])
        z = _dot(ag.astype(BF16), gw_ref[...]) + gb_ref[...]
        log_sig = jnp.minimum(z, 0.0) - jnp.log(1.0 + jnp.exp(-jnp.abs(z)))
        la_ref[...] = log_sig / GLA_TAU

    if emit_att:
        s0_ref, s1_ref = refs[3:5]
        tm = x_ref.shape[0]
        n4, n16 = tm // 4, tm // 16
        group = 2 * LANES

        def emit(slab_fn):
            for hp in range(HEAD_PAIRS):
                half = hp % 2
                if half == 0:
                    acc = _dot_nt(xn_ref[...], w_ref[LANES * hp:LANES * hp + group, :])
                slab = slab_fn(acc[:, LANES * half:LANES * (half + 1)])
                out1[0, hp] = slab.astype(BF16)
                s0_ref[half] = slab
                for r in range(4):
                    part = s0_ref[half, pl.ds(r, n4, stride=4), :]
                    out4[0, hp, :, LANES * r:LANES * (r + 1)] = part.astype(BF16)
                    s1_ref[half, n4 * r:n4 * (r + 1), :] = part
                for r in range(16):
                    part = s1_ref[half, pl.ds(n4 * (r % 4) + r // 4, n16, stride=4), :]
                    out16[0, hp, :, LANES * r:LANES * (r + 1)] = part.astype(BF16)

        @pl.when(col == 0)
        def _():
            emit(lambda s: _rope(s, cs_ref[...], sn_ref[...]) * (ATT_HEAD_DIM ** -0.5))

        @pl.when(col == 1)
        def _():
            emit(lambda s: _rope(s, cs_ref[...], sn_ref[...]))

        @pl.when(col == 2)
        def _():
            emit(lambda s: s)

    @pl.when(col == 3)
    def _():
        acc = _dot_nt(xn_ref[...], w_ref[...])
        half = GLA_HEADS * GLA_DK
        gla_ref[:, :half] = (acc[:, :half] * (GLA_DK ** -0.5)).astype(BF16)
        gla_ref[:, half:] = acc[:, half:].astype(BF16)

    @pl.when(col >= 4)
    def _():
        gla_ref[...] = _dot_nt(xn_ref[...], w_ref[...]).astype(BF16)


def _mod_spec(mod_rows, tm):
    if mod_rows == 1:
        return pl.BlockSpec((1, D_MODEL), lambda i, j: (0, 0))
    return pl.BlockSpec((tm, D_MODEL), lambda i, j: (i, 0))


def _inproj(x, g, sc, sh, w_all, gate_w, gate_b, cs, sn, *, tm, emit_att):
    t = x.shape[0]
    col0 = 0 if emit_att else 3
    mod = _mod_spec(sc.shape[0], tm)
    const = lambda block: pl.BlockSpec(block, lambda i, j: (0, 0))
    out_shape, out_specs, scratch = [], [], [pltpu.VMEM((tm, D_MODEL), BF16)]
    if emit_att:
        for d in DILATIONS:
            out_shape.append(jax.ShapeDtypeStruct((3, HEAD_PAIRS, t // d, d * LANES), BF16))
            out_specs.append(pl.BlockSpec((1, HEAD_PAIRS, tm // d, d * LANES),
                                          lambda i, j: (jnp.minimum(j, 2), 0, i, 0)))
        scratch += [pltpu.VMEM((2, tm, LANES), F32) for _ in range(2)]
    out_shape += [jax.ShapeDtypeStruct((t, 3 * 1024), BF16),
                  jax.ShapeDtypeStruct((t, GLA_HEADS * GLA_DK), F32)]
    out_specs += [pl.BlockSpec((tm, 1024), lambda i, j: (i, jnp.maximum(j + col0 - 3, 0))),
                  pl.BlockSpec((tm, GLA_HEADS * GLA_DK), lambda i, j: (i, 0))]
    gate_rows = (6 * 1024) // GLA_GATE_RANK
    return pl.pallas_call(
        functools.partial(_inproj_kernel, emit_att=emit_att),
        out_shape=tuple(out_shape),
        grid=(t // tm, 6 - col0),
        in_specs=[
            pl.BlockSpec((tm, D_MODEL), lambda i, j: (i, 0)),
            const((1, D_MODEL)),
            mod, mod,
            pl.BlockSpec((1024, D_MODEL), lambda i, j: (j + col0, 0)),
            pl.BlockSpec((GLA_GATE_RANK, D_MODEL), lambda i, j: (gate_rows, 0)),
            const((GLA_GATE_RANK, GLA_HEADS * GLA_DK)),
            const((1, GLA_HEADS * GLA_DK)),
            pl.BlockSpec((tm, LANES), lambda i, j: (i, 0)),
            pl.BlockSpec((tm, LANES), lambda i, j: (i, 0)),
        ],
        out_specs=tuple(out_specs),
        scratch_shapes=scratch,
        compiler_params=_cparams(("arbitrary", "arbitrary")),
        name="inproj_att" if emit_att else "inproj_gla",
    )(x, g, sc, sh, w_all, w_all, gate_w, gate_b, cs, sn)


def _qkv_rows_kernel(x_ref, g_ref, sc_ref, sh_ref, w_ref, cs_ref, sn_ref, o_ref, xn_ref, *, col0):
    j = pl.program_id(1)

    @pl.when(j == 0)
    def _():
        xn_ref[...] = _norm_mod(x_ref[...], g_ref[...], sc_ref[...], sh_ref[...]).astype(BF16)

    acc = _dot_nt(xn_ref[...], w_ref[...])
    col = j + col0

    @pl.when(col <= 1)
    def _():
        scale = jnp.where(col == 0, ATT_HEAD_DIM ** -0.5, 1.0)
        for hp in range(HEAD_PAIRS):
            lanes = slice(LANES * hp, LANES * (hp + 1))
            o_ref[0, :, lanes] = _rope(acc[:, lanes], cs_ref[...], sn_ref[...]) * scale

    @pl.when(col == 2)
    def _():
        o_ref[0] = acc


def _qkv_rows(x, g, sc, sh, w_main, cs, sn, *, tm, first_row, rows, col0):
    first = first_row // tm
    assert sc.shape[0] == 1 or first == 0
    ncols = 3 - col0
    mod = _mod_spec(sc.shape[0], tm)
    return pl.pallas_call(
        functools.partial(_qkv_rows_kernel, col0=col0),
        out_shape=jax.ShapeDtypeStruct((ncols, rows, ATT_WIDTH), F32),
        grid=(rows // tm, ncols),
        in_specs=[
            pl.BlockSpec((tm, D_MODEL), lambda i, j: (i + first, 0)),
            pl.BlockSpec((1, D_MODEL), lambda i, j: (0, 0)),
            mod, mod,
            pl.BlockSpec((1024, D_MODEL), lambda i, j: (j + col0, 0)),
            pl.BlockSpec((tm, LANES), lambda i, j: (i + first, 0)),
            pl.BlockSpec((tm, LANES), lambda i, j: (i + first, 0)),
        ],
        out_specs=pl.BlockSpec((1, tm, ATT_WIDTH), lambda i, j: (j, i, 0)),
        scratch_shapes=[pltpu.VMEM((tm, D_MODEL), BF16)],
        compiler_params=_cparams(("arbitrary", "arbitrary")),
        name="qkv_rows",
    )(x, g, sc, sh, w_main, cs, sn)


def _rope_tables(pos):
    half = ROT_DIM // 2
    inv_freq = ROPE_THETA ** (-jnp.arange(half, dtype=F32) / half)
    per_head = jnp.concatenate([inv_freq, inv_freq, jnp.zeros((ATT_HEAD_DIM - ROT_DIM,), F32)])
    ang = pos.astype(F32)[:, None] * jnp.tile(per_head, LANES // ATT_HEAD_DIM)[None, :]
    return jnp.cos(ang), jnp.sin(ang)


def _attn_kernel(*refs, n_cast):
    (q1, k1c, v1c, k1p, v1p, q4, k4c, v4c, k4p, v4p, q16, k16c, v16c, k16p, v16p), refs = (
        refs[:15], refs[15:])
    cast_in, o_ref, cast_out = refs[:n_cast], refs[n_cast], refs[n_cast + 1:2 * n_cast + 1]
    a1, m1, l1, a4, m4, l4, t16, s_scr, p_scr, bias_ref, o_scr = refs[2 * n_cast + 1:]
    for src, dst in zip(cast_in, cast_out):
        dst[...] = src[...].astype(dst.dtype)

    first_block = pl.program_id(1) == 0
    lane = lax.broadcasted_iota(jnp.int32, (1, LANES), 1)
    first = lane < ATT_HEAD_DIM
    head0 = jnp.where(first, 1.0, 0.0).astype(BF16)
    head1 = jnp.where(first, 0.0, 1.0).astype(BF16)
    ones = jnp.ones((2 * BAND, LANES), BF16)

    qi = lax.broadcasted_iota(jnp.int32, (2 * BAND, 2 * BAND), 0) % BAND
    kk = lax.broadcasted_iota(jnp.int32, (2 * BAND, 2 * BAND), 1)
    full = jnp.where(jnp.where(kk < BAND, kk - qi, qi - (kk - BAND)) >= 0, 0.0, NEG)
    bias_ref[0] = full
    bias_ref[1] = jnp.where(kk < BAND, full + jnp.where(first_block, NEG, 0.0), full)

    n1 = q1.shape[1] // BAND
    rows4 = q4.shape[1]
    n4 = rows4 // BAND

    def window(cur, prev, i, cols):
        if i == 0:
            return jnp.concatenate([prev[0, :, cols], cur[0, :BAND, cols]], axis=0)
        return cur[0, BAND * (i - 1):BAND * (i + 1), cols]

    jobs = []
    everything = slice(None)
    for i in range(n1):
        jobs.append(dict(q=(q1, i, everything), k=(k1c, k1p), v=(v1c, v1p), bias=int(i == 0),
                         dst=(a1, m1, l1, BAND * i)))
    for r in range(4):
        cols = slice(LANES * r, LANES * (r + 1))
        for i in range(n4):
            jobs.append(dict(q=(q4, i, cols), k=(k4c, k4p), v=(v4c, v4p), bias=int(i == 0),
                             dst=(a4, m4, l4, rows4 * r + BAND * i)))
    for r in range(16):
        cols = slice(LANES * r, LANES * (r + 1))
        jobs.append(dict(q=(q16, 0, cols), k=(k16c, k16p), v=(v16c, v16p), bias=1, merge=r))

    def scores(job, slot):
        ref, i, cols = job["q"]
        q = ref[0, BAND * i:BAND * (i + 1), cols]
        q2 = jnp.concatenate([q * head0, q * head1], axis=0)
        s_scr[slot] = _dot_nt(q2, window(*job["k"], i, cols))

    def softmax(job, slot):
        s = s_scr[slot] + bias_ref[job["bias"]]
        m = jnp.max(s, axis=-1, keepdims=True)
        p_scr[slot] = jnp.exp(s - m).astype(BF16)
        m = jnp.where(first, m[:BAND], m[BAND:])
        if "merge" in job:
            t16[slot] = m
        else:
            job["dst"][1][pl.ds(job["dst"][3], BAND), :] = m

    def values(job, slot):
        _, i, cols = job["q"]
        vw = jnp.concatenate([window(*job["v"], i, cols), ones], axis=1)
        o2 = _dot(p_scr[slot], vw)
        acc = jnp.where(first, o2[:BAND, :LANES], o2[BAND:, :LANES])
        den = jnp.where(first, o2[:BAND, LANES:], o2[BAND:, LANES:])
        if "merge" not in job:
            a_ref, _, l_ref, row = job["dst"]
            a_ref[pl.ds(row, BAND), :] = acc
            l_ref[pl.ds(row, BAND), :] = den
            return
        r = job["merge"]
        sl1 = pl.ds(r, BAND, stride=16)
        sl4 = pl.ds((r % 4) * rows4 + r // 4, BAND, stride=4)
        ma, mb, mc = m1[sl1, :], m4[sl4, :], t16[slot]
        m = jnp.maximum(jnp.maximum(ma, mb), mc)
        wa, wb, wc = jnp.exp(ma - m), jnp.exp(mb - m), jnp.exp(mc - m)
        num = wa * a1[sl1, :] + wb * a4[sl4, :] + wc * acc
        dsum = wa * l1[sl1, :] + wb * l4[sl4, :] + wc * den
        o_scr[sl1, :] = num / dsum

    slots = s_scr.shape[0]
    for step in range(len(jobs) + 2):
        if step < len(jobs):
            scores(jobs[step], step % slots)
        if 1 <= step <= len(jobs):
            softmax(jobs[step - 1], (step - 1) % slots)
        if step >= 2:
            values(jobs[step - 2], (step - 2) % slots)
    o_ref[0] = o_scr[...].astype(o_ref.dtype)


def _prompt_attention(views, weights):
    _, hp, t, _ = views[0].shape
    sb = 16 * BAND
    ns = t // sb
    in_specs, args = [], []
    for arr, d in zip(views, DILATIONS):
        rows, width = sb // d, d * LANES
        nb = rows // BAND

        def cur(c, rows=rows, width=width):
            return pl.BlockSpec((None, 1, rows, width), lambda h, s: (c, h, s, 0))

        def prev(c, width=width, nb=nb):
            return pl.BlockSpec((None, 1, BAND, width),
                                lambda h, s: (c, h, jnp.maximum(s * nb - 1, 0), 0))

        in_specs += [cur(0), cur(1), cur(2), prev(1), prev(2)]
        args += [arr] * 5
    cast_specs = []
    for w in weights:
        slab = w.shape[0] // (hp * ns)
        assert slab * hp * ns == w.shape[0] and slab % 16 == 0
        cast_specs.append(pl.BlockSpec((slab, w.shape[1]), lambda h, s: (h * ns + s, 0)))
    tile = (2 * BAND, 2 * BAND)
    return pl.pallas_call(
        functools.partial(_attn_kernel, n_cast=len(weights)),
        out_shape=tuple([jax.ShapeDtypeStruct((hp, t, LANES), BF16)]
                        + [jax.ShapeDtypeStruct(w.shape, BF16) for w in weights]),
        grid=(hp, ns),
        in_specs=in_specs + cast_specs,
        out_specs=tuple([pl.BlockSpec((1, sb, LANES), lambda h, s: (h, s, 0))] + cast_specs),
        scratch_shapes=(
            [pltpu.VMEM((sb, LANES), F32) for _ in range(6)]
            + [pltpu.VMEM((3, BAND, LANES), F32),
               pltpu.VMEM((3,) + tile, F32),
               pltpu.VMEM((3,) + tile, BF16),
               pltpu.VMEM((2,) + tile, F32),
               pltpu.VMEM((sb, LANES), F32)]),
        compiler_params=_cparams(("arbitrary", "arbitrary")),
        name="prompt_attn",
    )(*args, *weights)


def _gla_chunk(q, k, v, la, st_ref, h, tri, sub):
    c = q.shape[0]
    la_hi = la.astype(BF16)
    la_lo = (la - la_hi.astype(F32)).astype(BF16)
    b = _dot(tri, la_hi) + _dot(tri, la_lo)
    st = st_ref[h]
    o = _dot_nt((q * jnp.exp(b)).astype(BF16), st.astype(BF16))
    srow = lax.broadcasted_iota(jnp.int32, (c, GLA_DK), 0)
    att_rows = []
    for i in range(c // sub):
        lo, hi = i * sub, (i + 1) * sub
        ref = b[lo:lo + 1, :]
        qh = (q[lo:hi] * jnp.exp(b[lo:hi] - ref)).astype(BF16)
        kh = (k * jnp.exp(jnp.where(srow < hi, ref - b, 0.0))).astype(BF16)
        a = _dot_nt(qh, kh)
        ti = lax.broadcasted_iota(jnp.int32, (sub, c), 0) + lo
        si = lax.broadcasted_iota(jnp.int32, (sub, c), 1)
        att_rows.append(jnp.where(si <= ti, a, 0.0))
    att = att_rows[0] if len(att_rows) == 1 else jnp.concatenate(att_rows, axis=0)
    o = o + _dot(att.astype(BF16), v)
    b_last = b[c - 1:c, :]
    kbar = (k * jnp.exp(b_last - b)).astype(BF16)
    st_ref[h] = st * jnp.exp(b_last) + _dot_tn(v, kbar)
    return o


def _gla_out(o, g, gg):
    rs = lax.rsqrt(jnp.mean(o * o, axis=-1, keepdims=True) + EPS)
    return o * rs * g * _silu(gg)


def _tri(c):
    r = lax.broadcasted_iota(jnp.int32, (c, c), 0)
    s = lax.broadcasted_iota(jnp.int32, (c, c), 1)
    return jnp.where(s <= r, 1.0, 0.0).astype(BF16)


def _gla_prompt_kernel(qk_ref, v_ref, gg_ref, la_ref, g_ref, y_ref, s_ref, st_ref,
                       qd_ref, kl_ref, qs_ref, ks_ref, dec_ref, att_ref, o_ref, *, chunk, sub):
    i = pl.program_id(0)

    @pl.when(i == 0)
    def _():
        st_ref[...] = jnp.zeros_like(st_ref)

    tb = qk_ref.shape[0]
    nchunks, nsub = tb // chunk, chunk // sub
    half = GLA_HEADS * GLA_DK

    r = lax.broadcasted_iota(jnp.int32, (tb, tb), 0)
    c = lax.broadcasted_iota(jnp.int32, (tb, tb), 1)
    start = r - jnp.bitwise_and(r, chunk - 1)
    tri = jnp.where(c <= r, jnp.where(c >= start, 1.0, 0.0), 0.0).astype(BF16)
    la = la_ref[...]
    la_hi = la.astype(BF16)
    la_lo = (la - la_hi.astype(F32)).astype(BF16)
    b = _dot(tri, la_hi) + _dot(tri, la_lo)
    q = qk_ref[:, :half].astype(F32)
    k = qk_ref[:, half:].astype(F32)
    qd_ref[...] = (q * jnp.exp(b)).astype(BF16)
    srow = lax.broadcasted_iota(jnp.int32, (chunk, half), 0)
    for ci in range(nchunks):
        lo = ci * chunk
        bc, kc_ = b[lo:lo + chunk], k[lo:lo + chunk]
        b_last = bc[chunk - 1:chunk]
        dec_ref[ci:ci + 1, :] = jnp.exp(b_last)
        kl_ref[lo:lo + chunk, :] = (kc_ * jnp.exp(b_last - bc)).astype(BF16)
        for si in range(nsub):
            s_lo, s_hi = si * sub, (si + 1) * sub
            ref = bc[s_lo:s_lo + 1]
            qs_ref[lo + s_lo:lo + s_hi, :] = (
                q[lo + s_lo:lo + s_hi] * jnp.exp(bc[s_lo:s_hi] - ref)).astype(BF16)
            ks_ref[si, lo:lo + chunk, :] = (
                kc_ * jnp.exp(jnp.where(srow < s_hi, ref - bc, 0.0))).astype(BF16)

    ti = lax.broadcasted_iota(jnp.int32, (chunk, chunk), 0)
    si_ = lax.broadcasted_iota(jnp.int32, (chunk, chunk), 1)
    causal = si_ <= ti
    jobs = [(ci, h) for ci in range(nchunks) for h in range(GLA_HEADS)]

    def intra(job, slot):
        ci, h = job
        lo, kc = ci * chunk, slice(GLA_DK * h, GLA_DK * (h + 1))
        parts = [_dot_nt(qs_ref[lo + s * sub:lo + (s + 1) * sub, kc], ks_ref[s, lo:lo + chunk, kc])
                 for s in range(nsub)]
        att_ref[slot] = jnp.where(causal, jnp.concatenate(parts, axis=0), 0.0).astype(BF16)

    def state(job, slot):
        ci, h = job
        rows = slice(ci * chunk, (ci + 1) * chunk)
        kc, vc = slice(GLA_DK * h, GLA_DK * (h + 1)), slice(GLA_DV * h, GLA_DV * (h + 1))
        st = st_ref[h]
        v = v_ref[rows, vc]
        o_ref[slot] = _dot_nt(qd_ref[rows, kc], st.astype(BF16)) + _dot(att_ref[slot], v)
        st_ref[h] = st * dec_ref[ci:ci + 1, kc] + _dot_tn(v, kl_ref[rows, kc])

    def finish(job, slot):
        ci, h = job
        rows, vc = slice(ci * chunk, (ci + 1) * chunk), slice(GLA_DV * h, GLA_DV * (h + 1))
        y_ref[rows, vc] = _gla_out(o_ref[slot], g_ref[:, vc],
                                   gg_ref[rows, vc].astype(F32)).astype(y_ref.dtype)

    slots = att_ref.shape[0]
    for step in range(len(jobs) + 2):
        if step < len(jobs):
            intra(jobs[step], step % slots)
        if 1 <= step <= len(jobs):
            state(jobs[step - 1], (step - 1) % slots)
        if step >= 2:
            finish(jobs[step - 2], (step - 2) % slots)

    @pl.when(i == pl.num_programs(0) - 1)
    def _():
        for h in range(GLA_HEADS):
            s_ref[h] = st_ref[h].T


def _gla_prompt(gla_in, log_a, norm_g, *, tb, chunk, sub):
    t = gla_in.shape[0]
    return pl.pallas_call(
        functools.partial(_gla_prompt_kernel, chunk=chunk, sub=sub),
        out_shape=(
            jax.ShapeDtypeStruct((t, GLA_WIDTH), BF16),
            jax.ShapeDtypeStruct((GLA_HEADS, GLA_DK, GLA_DV), F32),
        ),
        grid=(t // tb,),
        in_specs=[
            pl.BlockSpec((tb, 1024), lambda i: (i, 0)),
            pl.BlockSpec((tb, 1024), lambda i: (i, 1)),
            pl.BlockSpec((tb, 1024), lambda i: (i, 2)),
            pl.BlockSpec((tb, GLA_HEADS * GLA_DK), lambda i: (i, 0)),
            pl.BlockSpec((1, GLA_WIDTH), lambda i: (0, 0)),
        ],
        out_specs=(
            pl.BlockSpec((tb, GLA_WIDTH), lambda i: (i, 0)),
            pl.BlockSpec((GLA_HEADS, GLA_DK, GLA_DV), lambda i: (0, 0, 0)),
        ),
        scratch_shapes=[
            pltpu.VMEM((GLA_HEADS, GLA_DV, GLA_DK), F32),
            pltpu.VMEM((tb, GLA_HEADS * GLA_DK), BF16),
            pltpu.VMEM((tb, GLA_HEADS * GLA_DK), BF16),
            pltpu.VMEM((tb, GLA_HEADS * GLA_DK), BF16),
            pltpu.VMEM((chunk // sub, tb, GLA_HEADS * GLA_DK), BF16),
            pltpu.VMEM((tb // chunk, GLA_HEADS * GLA_DK), F32),
            pltpu.VMEM((3, chunk, chunk), BF16),
            pltpu.VMEM((3, chunk, GLA_DV), F32),
        ],
        compiler_params=_cparams(("arbitrary",)),
        name="gla_prompt",
    )(gla_in, gla_in, gla_in, log_a, norm_g)


def _gla_sample_kernel(qk_ref, v_ref, gg_ref, la_ref, g_ref, s0_ref, y_ref, s_ref, st_ref, *, steps):
    group = s0_ref.shape[0]
    c = 2 * steps
    tri = _tri(c)
    half = GLA_HEADS * GLA_DK
    pad = lambda a: jnp.concatenate([a, jnp.zeros_like(a)], axis=0)
    for n in range(group):
        rows = slice(n * steps, (n + 1) * steps)
        for h in range(GLA_HEADS):
            idx = n * GLA_HEADS + h
            kc = slice(GLA_DK * h, GLA_DK * (h + 1))
            vc = slice(GLA_DV * h, GLA_DV * (h + 1))
            st_ref[idx] = s0_ref[n, h].T
            q = pad(qk_ref[rows, kc])
            k = pad(qk_ref[rows, half + GLA_DK * h:half + GLA_DK * (h + 1)])
            v = pad(v_ref[rows, vc]).astype(BF16)
            o = _gla_chunk(q, k, v, pad(la_ref[rows, kc]), st_ref, idx, tri, c)
            y_ref[rows, vc] = _gla_out(o[:steps], g_ref[:, vc], gg_ref[rows, vc])
            s_ref[n, h] = st_ref[idx].T


def _gla_sample(gla_in, log_a, norm_g, state, *, steps, group):
    t = gla_in.shape[0]
    rows = steps * group
    return pl.pallas_call(
        functools.partial(_gla_sample_kernel, steps=steps),
        out_shape=(
            jax.ShapeDtypeStruct((t, GLA_WIDTH), F32),
            jax.ShapeDtypeStruct(state.shape, F32),
        ),
        grid=(t // rows,),
        in_specs=[
            pl.BlockSpec((rows, 1024), lambda n: (n, 0)),
            pl.BlockSpec((rows, 1024), lambda n: (n, 1)),
            pl.BlockSpec((rows, 1024), lambda n: (n, 2)),
            pl.BlockSpec((rows, GLA_HEADS * GLA_DK), lambda n: (n, 0)),
            pl.BlockSpec((1, GLA_WIDTH), lambda n: (0, 0)),
            pl.BlockSpec((group, GLA_HEADS, GLA_DK, GLA_DV), lambda n: (n, 0, 0, 0)),
        ],
        out_specs=(
            pl.BlockSpec((rows, GLA_WIDTH), lambda n: (n, 0)),
            pl.BlockSpec((group, GLA_HEADS, GLA_DK, GLA_DV), lambda n: (n, 0, 0, 0)),
        ),
        scratch_shapes=[pltpu.VMEM((group * GLA_HEADS, GLA_DV, GLA_DK), F32)],
        compiler_params=_cparams(("arbitrary",)),
        name="gla_sample",
    )(gla_in, gla_in, gla_in, log_a, norm_g, state)


def _multiplicity(delta):
    total = jnp.zeros(delta.shape, F32)
    for d in DILATIONS:
        assert d & (d - 1) == 0
        hit = jnp.where(jnp.bitwise_and(delta, d - 1) == 0, 1.0, 0.0)
        total = total + jnp.where(delta <= BAND * d, hit, 0.0)
    return jnp.where(delta >= 0, total, 0.0)


def _sample_attn_kernel(q_ref, kn_ref, vn_ref, ckt_ref, cvt_ref, o_ref):
    steps = q_ref.shape[0]
    rows = ATT_HEADS * steps
    cache_len = ckt_ref.shape[2]
    qt = jnp.concatenate([q_ref[...]] * ATT_HEADS, axis=0)
    rh = lax.broadcasted_iota(jnp.int32, (rows, ATT_WIDTH), 0) // steps
    ch = lax.broadcasted_iota(jnp.int32, (rows, ATT_WIDTH), 1) // ATT_HEAD_DIM
    own = rh == ch
    qb = jnp.where(own, qt, 0.0).astype(BF16)
    pad = jnp.zeros((LANES - steps, ATT_WIDTH), F32)
    kn = jnp.concatenate([kn_ref[...], pad], axis=0).astype(BF16)
    vn = jnp.concatenate([vn_ref[...], pad], axis=0).astype(BF16)
    assert steps & (steps - 1) == 0
    t_c = jnp.bitwise_and(lax.broadcasted_iota(jnp.int32, (rows, cache_len), 0), steps - 1)
    mc = _multiplicity(cache_len + t_c - lax.broadcasted_iota(jnp.int32, (rows, cache_len), 1))
    t_n = jnp.bitwise_and(lax.broadcasted_iota(jnp.int32, (rows, LANES), 0), steps - 1)
    new = lax.broadcasted_iota(jnp.int32, (rows, LANES), 1)
    mn = jnp.where(new < steps, _multiplicity(t_n - new), 0.0)
    sc = jnp.where(mc > 0.0, _dot(qb, ckt_ref[0].astype(BF16)), NEG)
    sn = jnp.where(mn > 0.0, _dot_nt(qb, kn), NEG)
    m = jnp.maximum(jnp.max(sc, axis=-1, keepdims=True), jnp.max(sn, axis=-1, keepdims=True))
    pc = mc * jnp.exp(sc - m)
    pn = mn * jnp.exp(sn - m)
    den = jnp.sum(pc, axis=-1, keepdims=True) + jnp.sum(pn, axis=-1, keepdims=True)
    o = _dot_nt(pc.astype(BF16), cvt_ref[0].astype(BF16)) + _dot(pn.astype(BF16), vn)
    o = jnp.where(own, o / den, 0.0)
    o_ref[...] = jnp.sum(o.reshape(ATT_HEADS, steps, ATT_WIDTH), axis=0)


def _sample_attention(q, k_new, v_new, cache_kt, cache_vt, *, steps):
    nb, _, cache_len = cache_kt.shape
    tok = pl.BlockSpec((steps, ATT_WIDTH), lambda n: (n, 0))
    cache = pl.BlockSpec((1, ATT_WIDTH, cache_len), lambda n: (n, 0, 0))
    return pl.pallas_call(
        _sample_attn_kernel,
        out_shape=jax.ShapeDtypeStruct((nb * steps, ATT_WIDTH), F32),
        grid=(nb,),
        in_specs=[tok, tok, tok, cache, cache],
        out_specs=tok,
        compiler_params=_cparams(("arbitrary",)),
        name="sample_attn",
    )(q, k_new, v_new, cache_kt, cache_vt)


def _outproj_kernel(att_ref, gla_ref, x_ref, gate_ref, ng_ref, w_ref, g2_ref, sc2_ref, sh2_ref,
                    h_ref, hn_ref):
    att = jnp.concatenate([att_ref[hp].astype(F32) for hp in range(HEAD_PAIRS)], axis=1)
    rs = lax.rsqrt(jnp.mean(att * att, axis=-1, keepdims=True) + EPS)
    att_y = (att * rs * ng_ref[...]).astype(BF16)
    mix = _dot(att_y, w_ref[:ATT_WIDTH, :]) + _dot(gla_ref[...], w_ref[ATT_WIDTH:, :])
    h = x_ref[...] + gate_ref[...] * mix
    h_ref[...] = h
    hn_ref[...] = _norm_mod(h, g2_ref[...], sc2_ref[...], sh2_ref[...]).astype(BF16)


def _outproj(att, gla_y, x, gate, att_norm_g, w_out, g2, sc2, sh2, *, tm):
    t = x.shape[0]
    mod_rows = gate.shape[0]
    mod_map = (lambda i: (0, 0)) if mod_rows == 1 else (lambda i: (i, 0))
    mod = pl.BlockSpec((1, D_MODEL) if mod_rows == 1 else (tm, D_MODEL), mod_map)
    row = pl.BlockSpec((tm, D_MODEL), lambda i: (i, 0))
    return pl.pallas_call(
        _outproj_kernel,
        out_shape=(jax.ShapeDtypeStruct((t, D_MODEL), F32),
                   jax.ShapeDtypeStruct((t, D_MODEL), BF16)),
        grid=(t // tm,),
        in_specs=[
            pl.BlockSpec((HEAD_PAIRS, tm, LANES), lambda i: (0, i, 0)),
            pl.BlockSpec((tm, GLA_WIDTH), lambda i: (i, 0)),
            row, mod,
            pl.BlockSpec((1, ATT_WIDTH), lambda i: (0, 0)),
            pl.BlockSpec((ATT_WIDTH + GLA_WIDTH, D_MODEL), lambda i: (0, 0)),
            pl.BlockSpec((1, D_MODEL), lambda i: (0, 0)),
            mod, mod,
        ],
        out_specs=(row, row),
        compiler_params=_cparams(("arbitrary",)),
        name="outproj",
    )(att, gla_y, x, gate, att_norm_g, w_out, g2, sc2, sh2)


def _ffn_kernel(h_ref, hn_ref, gate_ref, fg_ref, wu_ref, wd_ref, y_ref):
    f = pl.program_id(1)
    last = pl.num_programs(1) - 1

    def hidden():
        u = jnp.maximum(_dot(hn_ref[...], wu_ref[...]), 0.0)
        return (u * u).astype(BF16)

    @pl.when(f == 0)
    def _():
        y_ref[...] = _dot(hidden(), wd_ref[...])

    @pl.when(jnp.logical_and(f > 0, f < last))
    def _():
        y_ref[...] += _dot(hidden(), wd_ref[...])

    @pl.when(f == last)
    def _():
        u2 = hidden()
        width = D_MODEL // 4
        ss = jnp.zeros((h_ref.shape[0], 1), F32)
        for c in range(0, D_MODEL, width):
            cols = slice(c, c + width)
            y = y_ref[:, cols] + _dot(u2, wd_ref[:, cols])
            h2 = h_ref[:, cols] + gate_ref[:, cols] * y
            y_ref[:, cols] = h2
            ss = ss + jnp.sum(h2 * h2, axis=-1, keepdims=True)
        y_ref[...] = y_ref[...] * lax.rsqrt(ss / D_MODEL + EPS) * fg_ref[...]


def _ffn(h, hn, gate, final_g, w_up, w_down, *, tm, tf):
    t = h.shape[0]
    assert D_FF // tf >= 2, "first and last hidden chunks are separate code paths"
    row = pl.BlockSpec((tm, D_MODEL), lambda i, f: (i, 0))
    return pl.pallas_call(
        _ffn_kernel,
        out_shape=jax.ShapeDtypeStruct((t, D_MODEL), F32),
        grid=(t // tm, D_FF // tf),
        in_specs=[
            row, row,
            _mod_spec(gate.shape[0], tm),
            pl.BlockSpec((1, D_MODEL), lambda i, f: (0, 0)),
            pl.BlockSpec((D_MODEL, tf), lambda i, f: (0, f)),
            pl.BlockSpec((tf, D_MODEL), lambda i, f: (f, 0)),
        ],
        out_specs=row,
        compiler_params=_cparams(("arbitrary", "arbitrary")),
        name="ffn",
    )(h, hn, gate, final_g, w_up, w_down)


def _split_ada(ada):
    return [ada[:, D_MODEL * i:D_MODEL * (i + 1)] for i in range(6)]


def kernel(x_prompt, x_sample, c_prompt, c_sample, cache_win_k, cache_win_v, state_gla, w_in, gla_gate_w, gla_gate_b, att_norm_g, gla_norm_g, w_out, norm1_g, norm2_g, w_ada, b_ada, w_up, w_down, final_g):
    assert w_in.shape[0] == 1, "single-layer model"
    n_p, t_p, _ = x_prompt.shape
    n_s, t_s, _ = x_sample.shape
    assert n_p == 1
    rows_s = n_s * t_s
    wb_p = min(MAX_WINDOW, t_p)

    w_main = jnp.transpose(w_in[0]).astype(BF16)
    gate_w = gla_gate_w[0].astype(BF16)
    gate_b = gla_gate_b[0][None, :]
    n1g, n2g = norm1_g[0][None, :], norm2_g[0][None, :]
    ang, gng = att_norm_g[0][None, :], gla_norm_g[0][None, :]
    fg = final_g[None, :]

    c_rows = n_p + n_s
    c_pad = -c_rows % 8
    c_all = jnp.concatenate([c_prompt, c_sample, jnp.zeros((c_pad, D_MODEL), F32)], axis=0)
    ada = _ada(c_all, w_ada[0], b_ada[0][None, :])
    sh1_p, sc1_p, g1_p, sh2_p, sc2_p, g2_p = _split_ada(ada[:n_p])
    sh1_s, sc1_s, g1_s, sh2_s, sc2_s, g2_s = _split_ada(
        jnp.repeat(ada[n_p:c_rows], t_s, axis=0))

    xp = x_prompt[0]
    cs_p, sn_p = _rope_tables(jnp.arange(t_p))
    outs = _inproj(xp, n1g, sc1_p, sh1_p, w_main, gate_w, gate_b, cs_p, sn_p,
                   tm=1024, emit_att=True)
    att, w_out_b, w_up_b, w_down_b = _prompt_attention(outs[:3], (w_out[0], w_up[0], w_down[0]))
    gla_in, log_a = outs[3], outs[4]
    win = _qkv_rows(xp, n1g, sc1_p, sh1_p, w_main, cs_p, sn_p,
                    tm=1024, first_row=t_p - wb_p, rows=wb_p, col0=1)
    win_k, win_v = win[0], win[1]
    gla_y, st_p = _gla_prompt(gla_in, log_a, gng, tb=512, chunk=64, sub=16)
    h_p, hn_p = _outproj(att, gla_y, xp, g1_p, ang, w_out_b, n2g, sc2_p, sh2_p, tm=512)
    y_p = _ffn(h_p, hn_p, g2_p, fg, w_up_b, w_down_b, tm=1024, tf=512)

    xs = x_sample.reshape(rows_s, D_MODEL)
    cs_s, sn_s = _rope_tables(jnp.tile(PAST_LEN + jnp.arange(t_s), n_s))
    gla_in_s, log_a_s = _inproj(xs, n1g, sc1_s, sh1_s, w_main, gate_w, gate_b, cs_s, sn_s,
                                tm=rows_s, emit_att=False)
    qkv_s = _qkv_rows(xs, n1g, sc1_s, sh1_s, w_main, cs_s, sn_s,
                      tm=rows_s, first_row=0, rows=rows_s, col0=0)
    new_k, new_v = qkv_s[1], qkv_s[2]
    wc = cache_win_k.shape[2]
    cache_t = lambda c: jnp.transpose(c[0], (0, 2, 3, 1)).reshape(n_s, ATT_WIDTH, wc)
    att_tok = _sample_attention(qkv_s[0], new_k, new_v, cache_t(cache_win_k), cache_t(cache_win_v),
                                steps=t_s)
    att_s = jnp.transpose(att_tok.reshape(rows_s, HEAD_PAIRS, LANES), (1, 0, 2)).astype(BF16)
    gla_y_s, st_s = _gla_sample(gla_in_s.astype(F32), log_a_s, gng, state_gla[0], steps=t_s,
                                group=4 if n_s % 4 == 0 else 1)
    h_s, hn_s = _outproj(att_s, gla_y_s.astype(BF16), xs, g1_s, ang, w_out_b, n2g, sc2_s, sh2_s,
                         tm=rows_s)
    y_s = _ffn(h_s, hn_s, g2_s, fg, w_up_b, w_down_b, tm=rows_s, tf=2048)

    return (
        y_p[None],
        y_s.reshape(n_s, t_s, D_MODEL),
        win_k.reshape(1, n_p, wb_p, ATT_HEADS, ATT_HEAD_DIM),
        win_v.reshape(1, n_p, wb_p, ATT_HEADS, ATT_HEAD_DIM),
        st_p[None, None],
        new_k.reshape(1, n_s, t_s, ATT_HEADS, ATT_HEAD_DIM),
        new_v.reshape(1, n_s, t_s, ATT_HEADS, ATT_HEAD_DIM),
        st_s[None],
    )
```

```python
import functools

import jax
import jax.numpy as jnp
from jax import lax
from jax.experimental import pallas as pl
from jax.experimental.pallas import tpu as pltpu

F32 = jnp.float32
BF16 = jnp.bfloat16

D_MODEL = 2048
ATT_WIDTH = 1024
ATT_HEADS = 16
ATT_HEAD_DIM = 64
ROT_DIM = 16
ROPE_THETA = 500000.0
DILATIONS = (1, 4, 16)
BAND = 128
MAX_WINDOW = 2048
GLA_HEADS = 4
GLA_DK = 128
GLA_DV = 256
GLA_WIDTH = 1024
GLA_GATE_RANK = 16
GLA_TAU = 16.0
D_FF = 8192
EPS = 1e-6
PAST_LEN = 16384
NEG = -1e30

LANES = 128
HEAD_PAIRS = ATT_WIDTH // LANES
VMEM_LIMIT = 60000 * 1024


def _cparams(sem):
    return pltpu.CompilerParams(dimension_semantics=sem, vmem_limit_bytes=VMEM_LIMIT)


def _dot(a, b):
    return jnp.dot(a, b, preferred_element_type=F32)


def _dot_nt(a, b):
    return lax.dot_general(a, b, (((1,), (1,)), ((), ())), preferred_element_type=F32)


def _dot_tn(a, b):
    return lax.dot_general(a, b, (((0,), (0,)), ((), ())), preferred_element_type=F32)


def _silu(x):
    return x / (1.0 + jnp.exp(-x))


def _ada_kernel(c_ref, w_ref, b_ref, o_ref):
    s = _silu(c_ref[...]).astype(BF16)
    o_ref[...] = _dot(s, w_ref[...].astype(BF16)) + b_ref[...]


def _ada(c, w_ada, b_ada):
    rows = c.shape[0]
    n = w_ada.shape[1]
    tn = 1024
    return pl.pallas_call(
        _ada_kernel,
        out_shape=jax.ShapeDtypeStruct((rows, n), F32),
        grid=(n // tn,),
        in_specs=[
            pl.BlockSpec((rows, D_MODEL), lambda j: (0, 0)),
            pl.BlockSpec((D_MODEL, tn), lambda j: (0, j)),
            pl.BlockSpec((1, tn), lambda j: (0, j)),
        ],
        out_specs=pl.BlockSpec((rows, tn), lambda j: (0, j)),
        compiler_params=_cparams(("arbitrary",)),
        name="ada",
    )(c, w_ada, b_ada)


def _rope(slab, cs, sn):
    d = lax.broadcasted_iota(jnp.int32, (1, LANES), 1) % ATT_HEAD_DIM
    half = ROT_DIM // 2
    from_hi = jnp.where(d < half, -1.0, 0.0)
    from_lo = jnp.where((d >= half) & (d < ROT_DIM), 1.0, 0.0)
    return (slab * cs + pltpu.roll(slab, LANES - half, 1) * (sn * from_hi)
            + pltpu.roll(slab, half, 1) * (sn * from_lo))


def _norm_mod(x, g, sc, sh):
    rs = lax.rsqrt(jnp.mean(x * x, axis=-1, keepdims=True) + EPS)
    return x * rs * g * (1.0 + sc) + sh


def _inproj_kernel(*refs, emit_att):
    (x_ref, g_ref, sc_ref, sh_ref, w_ref, wag_ref, gw_ref, gb_ref, cs_ref, sn_ref), refs = (
        refs[:10], refs[10:])
    if emit_att:
        (out1, out4, out16), refs = refs[:3], refs[3:]
    gla_ref, la_ref, xn_ref = refs[:3]
    j = pl.program_id(1)
    col = j if emit_att else j + 3

    @pl.when(j == 0)
    def _():
        xnb = _norm_mod(x_ref[...], g_ref[...], sc_ref[...], sh_ref[...]).astype(BF16)
        xn_ref[...] = xnb
        ag = _dot_nt(xnb, wag_ref[...])
        z = _dot(ag.astype(BF16), gw_ref[...]) + gb_ref[...]
        log_sig = jnp.minimum(z, 0.0) - jnp.log(1.0 + jnp.exp(-jnp.abs(z)))
        la_ref[...] = log_sig / GLA_TAU

    if emit_att:
        s0_ref, s1_ref = refs[3:5]
        tm = x_ref.shape[0]
        n4, n16 = tm // 4, tm // 16
        group = 2 * LANES

        def emit(slab_fn):
            for hp in range(HEAD_PAIRS):
                half = hp % 2
                if half == 0:
                    acc = _dot_nt(xn_ref[...], w_ref[LANES * hp:LANES * hp + group, :])
                slab = slab_fn(acc[:, LANES * half:LANES * (half + 1)])
                out1[0, hp] = slab.astype(BF16)
                s0_ref[half] = slab
                for r in range(4):
                    part = s0_ref[half, pl.ds(r, n4, stride=4), :]
                    out4[0, hp, :, LANES * r:LANES * (r + 1)] = part.astype(BF16)
                    s1_ref[half, n4 * r:n4 * (r + 1), :] = part
                for r in range(16):
                    part = s1_ref[half, pl.ds(n4 * (r % 4) + r // 4, n16, stride=4), :]
                    out16[0, hp, :, LANES * r:LANES * (r + 1)] = part.astype(BF16)

        @pl.when(col == 0)
        def _():
            emit(lambda s: _rope(s, cs_ref[...], sn_ref[...]) * (ATT_HEAD_DIM ** -0.5))

        @pl.when(col == 1)
        def _():
            emit(lambda s: _rope(s, cs_ref[...], sn_ref[...]))

        @pl.when(col == 2)
        def _():
            emit(lambda s: s)

    @pl.when(col == 3)
    def _():
        acc = _dot_nt(xn_ref[...], w_ref[...])
        half = GLA_HEADS * GLA_DK
        gla_ref[:, :half] = (acc[:, :half] * (GLA_DK ** -0.5)).astype(BF16)
        gla_ref[:, half:] = acc[:, half:].astype(BF16)

    @pl.when(col >= 4)
    def _():
        gla_ref[...] = _dot_nt(xn_ref[...], w_ref[...]).astype(BF16)


def _mod_spec(mod_rows, tm):
    if mod_rows == 1:
        return pl.BlockSpec((1, D_MODEL), lambda i, j: (0, 0))
    return pl.BlockSpec((tm, D_MODEL), lambda i, j: (i, 0))


def _inproj(x, g, sc, sh, w_all, gate_w, gate_b, cs, sn, *, tm, emit_att):
    t = x.shape[0]
    col0 = 0 if emit_att else 3
    mod = _mod_spec(sc.shape[0], tm)
    const = lambda block: pl.BlockSpec(block, lambda i, j: (0, 0))
    out_shape, out_specs, scratch = [], [], [pltpu.VMEM((tm, D_MODEL), BF16)]
    if emit_att:
        for d in DILATIONS:
            out_shape.append(jax.ShapeDtypeStruct((3, HEAD_PAIRS, t // d, d * LANES), BF16))
            out_specs.append(pl.BlockSpec((1, HEAD_PAIRS, tm // d, d * LANES),
                                          lambda i, j: (jnp.minimum(j, 2), 0, i, 0)))
        scratch += [pltpu.VMEM((2, tm, LANES), F32) for _ in range(2)]
    out_shape += [jax.ShapeDtypeStruct((t, 3 * 1024), BF16),
                  jax.ShapeDtypeStruct((t, GLA_HEADS * GLA_DK), F32)]
    out_specs += [pl.BlockSpec((tm, 1024), lambda i, j: (i, jnp.maximum(j + col0 - 3, 0))),
                  pl.BlockSpec((tm, GLA_HEADS * GLA_DK), lambda i, j: (i, 0))]
    gate_rows = (6 * 1024) // GLA_GATE_RANK
    return pl.pallas_call(
        functools.partial(_inproj_kernel, emit_att=emit_att),
        out_shape=tuple(out_shape),
        grid=(t // tm, 6 - col0),
        in_specs=[
            pl.BlockSpec((tm, D_MODEL), lambda i, j: (i, 0)),
            const((1, D_MODEL)),
            mod, mod,
            pl.BlockSpec((1024, D_MODEL), lambda i, j: (j + col0, 0)),
            pl.BlockSpec((GLA_GATE_RANK, D_MODEL), lambda i, j: (gate_rows, 0)),
            const((GLA_GATE_RANK, GLA_HEADS * GLA_DK)),
            const((1, GLA_HEADS * GLA_DK)),
            pl.BlockSpec((tm, LANES), lambda i, j: (i, 0)),
            pl.BlockSpec((tm, LANES), lambda i, j: (i, 0)),
        ],
        out_specs=tuple(out_specs),
        scratch_shapes=scratch,
        compiler_params=_cparams(("arbitrary", "arbitrary")),
        name="inproj_att" if emit_att else "inproj_gla",
    )(x, g, sc, sh, w_all, w_all, gate_w, gate_b, cs, sn)


def _qkv_rows_kernel(x_ref, g_ref, sc_ref, sh_ref, w_ref, cs_ref, sn_ref, o_ref, xn_ref, *, col0):
    j = pl.program_id(1)

    @pl.when(j == 0)
    def _():
        xn_ref[...] = _norm_mod(x_ref[...], g_ref[...], sc_ref[...], sh_ref[...]).astype(BF16)

    acc = _dot_nt(xn_ref[...], w_ref[...])
    col = j + col0

    @pl.when(col <= 1)
    def _():
        scale = jnp.where(col == 0, ATT_HEAD_DIM ** -0.5, 1.0)
        for hp in range(HEAD_PAIRS):
            lanes = slice(LANES * hp, LANES * (hp + 1))
            o_ref[0, :, lanes] = _rope(acc[:, lanes], cs_ref[...], sn_ref[...]) * scale

    @pl.when(col == 2)
    def _():
        o_ref[0] = acc


def _qkv_rows(x, g, sc, sh, w_main, cs, sn, *, tm, first_row, rows, col0):
    first = first_row // tm
    assert sc.shape[0] == 1 or first == 0
    ncols = 3 - col0
    mod = _mod_spec(sc.shape[0], tm)
    return pl.pallas_call(
        functools.partial(_qkv_rows_kernel, col0=col0),
        out_shape=jax.ShapeDtypeStruct((ncols, rows, ATT_WIDTH), F32),
        grid=(rows // tm, ncols),
        in_specs=[
            pl.BlockSpec((tm, D_MODEL), lambda i, j: (i + first, 0)),
            pl.BlockSpec((1, D_MODEL), lambda i, j: (0, 0)),
            mod, mod,
            pl.BlockSpec((1024, D_MODEL), lambda i, j: (j + col0, 0)),
            pl.BlockSpec((tm, LANES), lambda i, j: (i + first, 0)),
            pl.BlockSpec((tm, LANES), lambda i, j: (i + first, 0)),
        ],
        out_specs=pl.BlockSpec((1, tm, ATT_WIDTH), lambda i, j: (j, i, 0)),
        scratch_shapes=[pltpu.VMEM((tm, D_MODEL), BF16)],
        compiler_params=_cparams(("arbitrary", "arbitrary")),
        name="qkv_rows",
    )(x, g, sc, sh, w_main, cs, sn)


def _rope_tables(pos):
    half = ROT_DIM // 2
    inv_freq = ROPE_THETA ** (-jnp.arange(half, dtype=F32) / half)
    per_head = jnp.concatenate([inv_freq, inv_freq, jnp.zeros((ATT_HEAD_DIM - ROT_DIM,), F32)])
    ang = pos.astype(F32)[:, None] * jnp.tile(per_head, LANES // ATT_HEAD_DIM)[None, :]
    return jnp.cos(ang), jnp.sin(ang)


def _attn_kernel(*refs, n_cast):
    (q1, k1c, v1c, k1p, v1p, q4, k4c, v4c, k4p, v4p, q16, k16c, v16c, k16p, v16p), refs = (
        refs[:15], refs[15:])
    cast_in, o_ref, cast_out = refs[:n_cast], refs[n_cast], refs[n_cast + 1:2 * n_cast + 1]
    a1, m1, l1, a4, m4, l4, t16, s_scr, p_scr, bias_ref, o_scr = refs[2 * n_cast + 1:]
    for src, dst in zip(cast_in, cast_out):
        dst[...] = src[...].astype(dst.dtype)

    first_block = pl.program_id(1) == 0
    lane = lax.broadcasted_iota(jnp.int32, (1, LANES), 1)
    first = lane < ATT_HEAD_DIM
    head0 = jnp.where(first, 1.0, 0.0).astype(BF16)
    head1 = jnp.where(first, 0.0, 1.0).astype(BF16)
    ones = jnp.ones((2 * BAND, LANES), BF16)

    qi = lax.broadcasted_iota(jnp.int32, (2 * BAND, 2 * BAND), 0) % BAND
    kk = lax.broadcasted_iota(jnp.int32, (2 * BAND, 2 * BAND), 1)
    full = jnp.where(jnp.where(kk < BAND, kk - qi, qi - (kk - BAND)) >= 0, 0.0, NEG)
    bias_ref[0] = full
    bias_ref[1] = jnp.where(kk < BAND, full + jnp.where(first_block, NEG, 0.0), full)

    n1 = q1.shape[1] // BAND
    rows4 = q4.shape[1]
    n4 = rows4 // BAND

    def window(cur, prev, i, cols):
        if i == 0:
            return jnp.concatenate([prev[0, :, cols], cur[0, :BAND, cols]], axis=0)
        return cur[0, BAND * (i - 1):BAND * (i + 1), cols]

    jobs = []
    everything = slice(None)
    for i in range(n1):
        jobs.append(dict(q=(q1, i, everything), k=(k1c, k1p), v=(v1c, v1p), bias=int(i == 0),
                         dst=(a1, m1, l1, BAND * i)))
    for r in range(4):
        cols = slice(LANES * r, LANES * (r + 1))
        for i in range(n4):
            jobs.append(dict(q=(q4, i, cols), k=(k4c, k4p), v=(v4c, v4p), bias=int(i == 0),
                             dst=(a4, m4, l4, rows4 * r + BAND * i)))
    for r in range(16):
        cols = slice(LANES * r, LANES * (r + 1))
        jobs.append(dict(q=(q16, 0, cols), k=(k16c, k16p), v=(v16c, v16p), bias=1, merge=r))

    def scores(job, slot):
        ref, i, cols = job["q"]
        q = ref[0, BAND * i:BAND * (i + 1), cols]
        q2 = jnp.concatenate([q * head0, q * head1], axis=0)
        s_scr[slot] = _dot_nt(q2, window(*job["k"], i, cols))

    def softmax(job, slot):
        s = s_scr[slot] + bias_ref[job["bias"]]
        m = jnp.max(s, axis=-1, keepdims=True)
        p_scr[slot] = jnp.exp(s - m).astype(BF16)
        m = jnp.where(first, m[:BAND], m[BAND:])
        if "merge" in job:
            t16[slot] = m
        else:
            job["dst"][1][pl.ds(job["dst"][3], BAND), :] = m

    def values(job, slot):
        _, i, cols = job["q"]
        vw = jnp.concatenate([window(*job["v"], i, cols), ones], axis=1)
        o2 = _dot(p_scr[slot], vw)
        acc = jnp.where(first, o2[:BAND, :LANES], o2[BAND:, :LANES])
        den = jnp.where(first, o2[:BAND, LANES:], o2[BAND:, LANES:])
        if "merge" not in job:
            a_ref, _, l_ref, row = job["dst"]
            a_ref[pl.ds(row, BAND), :] = acc
            l_ref[pl.ds(row, BAND), :] = den
            return
        r = job["merge"]
        sl1 = pl.ds(r, BAND, stride=16)
        sl4 = pl.ds((r % 4) * rows4 + r // 4, BAND, stride=4)
        ma, mb, mc = m1[sl1, :], m4[sl4, :], t16[slot]
        m = jnp.maximum(jnp.maximum(ma, mb), mc)
        wa, wb, wc = jnp.exp(ma - m), jnp.exp(mb - m), jnp.exp(mc - m)
        num = wa * a1[sl1, :] + wb * a4[sl4, :] + wc * acc
        dsum = wa * l1[sl1, :] + wb * l4[sl4, :] + wc * den
        o_scr[sl1, :] = num / dsum

    slots = s_scr.shape[0]
    for step in range(len(jobs) + 2):
        if step < len(jobs):
            scores(jobs[step], step % slots)
        if 1 <= step <= len(jobs):
            softmax(jobs[step - 1], (step - 1) % slots)
        if step >= 2:
            values(jobs[step - 2], (step - 2) % slots)
    o_ref[0] = o_scr[...].astype(o_ref.dtype)


def _prompt_attention(views, weights):
    _, hp, t, _ = views[0].shape
    sb = 16 * BAND
    ns = t // sb
    in_specs, args = [], []
    for arr, d in zip(views, DILATIONS):
        rows, width = sb // d, d * LANES
        nb = rows // BAND

        def cur(c, rows=rows, width=width):
            return pl.BlockSpec((None, 1, rows, width), lambda h, s: (c, h, s, 0))

        def prev(c, width=width, nb=nb):
            return pl.BlockSpec((None, 1, BAND, width),
                                lambda h, s: (c, h, jnp.maximum(s * nb - 1, 0), 0))

        in_specs += [cur(0), cur(1), cur(2), prev(1), prev(2)]
        args += [arr] * 5
    cast_specs = []
    for w in weights:
        slab = w.shape[0] // (hp * ns)
        assert slab * hp * ns == w.shape[0] and slab % 16 == 0
        cast_specs.append(pl.BlockSpec((slab, w.shape[1]), lambda h, s: (h * ns + s, 0)))
    tile = (2 * BAND, 2 * BAND)
    return pl.pallas_call(
        functools.partial(_attn_kernel, n_cast=len(weights)),
        out_shape=tuple([jax.ShapeDtypeStruct((hp, t, LANES), BF16)]
                        + [jax.ShapeDtypeStruct(w.shape, BF16) for w in weights]),
        grid=(hp, ns),
        in_specs=in_specs + cast_specs,
        out_specs=tuple([pl.BlockSpec((1, sb, LANES), lambda h, s: (h, s, 0))] + cast_specs),
        scratch_shapes=(
            [pltpu.VMEM((sb, LANES), F32) for _ in range(6)]
            + [pltpu.VMEM((3, BAND, LANES), F32),
               pltpu.VMEM((3,) + tile, F32),
               pltpu.VMEM((3,) + tile, BF16),
               pltpu.VMEM((2,) + tile, F32),
               pltpu.VMEM((sb, LANES), F32)]),
        compiler_params=_cparams(("arbitrary", "arbitrary")),
        name="prompt_attn",
    )(*args, *weights)


def _gla_chunk(q, k, v, la, st_ref, h, tri, sub):
    c = q.shape[0]
    la_hi = la.astype(BF16)
    la_lo = (la - la_hi.astype(F32)).astype(BF16)
    b = _dot(tri, la_hi) + _dot(tri, la_lo)
    st = st_ref[h]
    o = _dot_nt((q * jnp.exp(b)).astype(BF16), st.astype(BF16))
    srow = lax.broadcasted_iota(jnp.int32, (c, GLA_DK), 0)
    att_rows = []
    for i in range(c // sub):
        lo, hi = i * sub, (i + 1) * sub
        ref = b[lo:lo + 1, :]
        qh = (q[lo:hi] * jnp.exp(b[lo:hi] - ref)).astype(BF16)
        kh = (k * jnp.exp(jnp.where(srow < hi, ref - b, 0.0))).astype(BF16)
        a = _dot_nt(qh, kh)
        ti = lax.broadcasted_iota(jnp.int32, (sub, c), 0) + lo
        si = lax.broadcasted_iota(jnp.int32, (sub, c), 1)
        att_rows.append(jnp.where(si <= ti, a, 0.0))
    att = att_rows[0] if len(att_rows) == 1 else jnp.concatenate(att_rows, axis=0)
    o = o + _dot(att.astype(BF16), v)
    b_last = b[c - 1:c, :]
    kbar = (k * jnp.exp(b_last - b)).astype(BF16)
    st_ref[h] = st * jnp.exp(b_last) + _dot_tn(v, kbar)
    return o


def _gla_out(o, g, gg):
    rs = lax.rsqrt(jnp.mean(o * o, axis=-1, keepdims=True) + EPS)
    return o * rs * g * _silu(gg)


def _tri(c):
    r = lax.broadcasted_iota(jnp.int32, (c, c), 0)
    s = lax.broadcasted_iota(jnp.int32, (c, c), 1)
    return jnp.where(s <= r, 1.0, 0.0).astype(BF16)


def _gla_prompt_kernel(qk_ref, v_ref, gg_ref, la_ref, g_ref, y_ref, s_ref, st_ref,
                       qd_ref, kl_ref, qs_ref, ks_ref, dec_ref, att_ref, o_ref, *, chunk, sub):
    i = pl.program_id(0)

    @pl.when(i == 0)
    def _():
        st_ref[...] = jnp.zeros_like(st_ref)

    tb = qk_ref.shape[0]
    nchunks, nsub = tb // chunk, chunk // sub
    half = GLA_HEADS * GLA_DK

    r = lax.broadcasted_iota(jnp.int32, (tb, tb), 0)
    c = lax.broadcasted_iota(jnp.int32, (tb, tb), 1)
    start = r - jnp.bitwise_and(r, chunk - 1)
    tri = jnp.where(c <= r, jnp.where(c >= start, 1.0, 0.0), 0.0).astype(BF16)
    la = la_ref[...]
    la_hi = la.astype(BF16)
    la_lo = (la - la_hi.astype(F32)).astype(BF16)
    b = _dot(tri, la_hi) + _dot(tri, la_lo)
    q = qk_ref[:, :half].astype(F32)
    k = qk_ref[:, half:].astype(F32)
    qd_ref[...] = (q * jnp.exp(b)).astype(BF16)
    srow = lax.broadcasted_iota(jnp.int32, (chunk, half), 0)
    for ci in range(nchunks):
        lo = ci * chunk
        bc, kc_ = b[lo:lo + chunk], k[lo:lo + chunk]
        b_last = bc[chunk - 1:chunk]
        dec_ref[ci:ci + 1, :] = jnp.exp(b_last)
        kl_ref[lo:lo + chunk, :] = (kc_ * jnp.exp(b_last - bc)).astype(BF16)
        for si in range(nsub):
            s_lo, s_hi = si * sub, (si + 1) * sub
            ref = bc[s_lo:s_lo + 1]
            qs_ref[lo + s_lo:lo + s_hi, :] = (
                q[lo + s_lo:lo + s_hi] * jnp.exp(bc[s_lo:s_hi] - ref)).astype(BF16)
            ks_ref[si, lo:lo + chunk, :] = (
                kc_ * jnp.exp(jnp.where(srow < s_hi, ref - bc, 0.0))).astype(BF16)

    ti = lax.broadcasted_iota(jnp.int32, (chunk, chunk), 0)
    si_ = lax.broadcasted_iota(jnp.int32, (chunk, chunk), 1)
    causal = si_ <= ti
    jobs = [(ci, h) for ci in range(nchunks) for h in range(GLA_HEADS)]

    def intra(job, slot):
        ci, h = job
        lo, kc = ci * chunk, slice(GLA_DK * h, GLA_DK * (h + 1))
        parts = [_dot_nt(qs_ref[lo + s * sub:lo + (s + 1) * sub, kc], ks_ref[s, lo:lo + chunk, kc])
                 for s in range(nsub)]
        att_ref[slot] = jnp.where(causal, jnp.concatenate(parts, axis=0), 0.0).astype(BF16)

    def state(job, slot):
        ci, h = job
        rows = slice(ci * chunk, (ci + 1) * chunk)
        kc, vc = slice(GLA_DK * h, GLA_DK * (h + 1)), slice(GLA_DV * h, GLA_DV * (h + 1))
        st = st_ref[h]
        v = v_ref[rows, vc]
        o_ref[slot] = _dot_nt(qd_ref[rows, kc], st.astype(BF16)) + _dot(att_ref[slot], v)
        st_ref[h] = st * dec_ref[ci:ci + 1, kc] + _dot_tn(v, kl_ref[rows, kc])

    def finish(job, slot):
        ci, h = job
        rows, vc = slice(ci * chunk, (ci + 1) * chunk), slice(GLA_DV * h, GLA_DV * (h + 1))
        y_ref[rows, vc] = _gla_out(o_ref[slot], g_ref[:, vc],
                                   gg_ref[rows, vc].astype(F32)).astype(y_ref.dtype)

    slots = att_ref.shape[0]
    for step in range(len(jobs) + 2):
        if step < len(jobs):
            intra(jobs[step], step % slots)
        if 1 <= step <= len(jobs):
            state(jobs[step - 1], (step - 1) % slots)
        if step >= 2:
            finish(jobs[step - 2], (step - 2) % slots)

    @pl.when(i == pl.num_programs(0) - 1)
    def _():
        for h in range(GLA_HEADS):
            s_ref[h] = st_ref[h].T


def _gla_prompt(gla_in, log_a, norm_g, *, tb, chunk, sub):
    t = gla_in.shape[0]
    return pl.pallas_call(
        functools.partial(_gla_prompt_kernel, chunk=chunk, sub=sub),
        out_shape=(
            jax.ShapeDtypeStruct((t, GLA_WIDTH), BF16),
            jax.ShapeDtypeStruct((GLA_HEADS, GLA_DK, GLA_DV), F32),
        ),
        grid=(t // tb,),
        in_specs=[
            pl.BlockSpec((tb, 1024), lambda i: (i, 0)),
            pl.BlockSpec((tb, 1024), lambda i: (i, 1)),
            pl.BlockSpec((tb, 1024), lambda i: (i, 2)),
            pl.BlockSpec((tb, GLA_HEADS * GLA_DK), lambda i: (i, 0)),
            pl.BlockSpec((1, GLA_WIDTH), lambda i: (0, 0)),
        ],
        out_specs=(
            pl.BlockSpec((tb, GLA_WIDTH), lambda i: (i, 0)),
            pl.BlockSpec((GLA_HEADS, GLA_DK, GLA_DV), lambda i: (0, 0, 0)),
        ),
        scratch_shapes=[
            pltpu.VMEM((GLA_HEADS, GLA_DV, GLA_DK), F32),
            pltpu.VMEM((tb, GLA_HEADS * GLA_DK), BF16),
            pltpu.VMEM((tb, GLA_HEADS * GLA_DK), BF16),
            pltpu.VMEM((tb, GLA_HEADS * GLA_DK), BF16),
            pltpu.VMEM((chunk // sub, tb, GLA_HEADS * GLA_DK), BF16),
            pltpu.VMEM((tb // chunk, GLA_HEADS * GLA_DK), F32),
            pltpu.VMEM((3, chunk, chunk), BF16),
            pltpu.VMEM((3, chunk, GLA_DV), F32),
        ],
        compiler_params=_cparams(("arbitrary",)),
        name="gla_prompt",
    )(gla_in, gla_in, gla_in, log_a, norm_g)


def _gla_sample_kernel(qk_ref, v_ref, gg_ref, la_ref, g_ref, s0_ref, y_ref, s_ref, st_ref, *, steps):
    group = s0_ref.shape[0]
    c = 2 * steps
    tri = _tri(c)
    half = GLA_HEADS * GLA_DK
    pad = lambda a: jnp.concatenate([a, jnp.zeros_like(a)], axis=0)
    for n in range(group):
        rows = slice(n * steps, (n + 1) * steps)
        for h in range(GLA_HEADS):
            idx = n * GLA_HEADS + h
            kc = slice(GLA_DK * h, GLA_DK * (h + 1))
            vc = slice(GLA_DV * h, GLA_DV * (h + 1))
            st_ref[idx] = s0_ref[n, h].T
            q = pad(qk_ref[rows, kc])
            k = pad(qk_ref[rows, half + GLA_DK * h:half + GLA_DK * (h + 1)])
            v = pad(v_ref[rows, vc]).astype(BF16)
            o = _gla_chunk(q, k, v, pad(la_ref[rows, kc]), st_ref, idx, tri, c)
            y_ref[rows, vc] = _gla_out(o[:steps], g_ref[:, vc], gg_ref[rows, vc])
            s_ref[n, h] = st_ref[idx].T


def _gla_sample(gla_in, log_a, norm_g, state, *, steps, group):
    t = gla_in.shape[0]
    rows = steps * group
    return pl.pallas_call(
        functools.partial(_gla_sample_kernel, steps=steps),
        out_shape=(
            jax.ShapeDtypeStruct((t, GLA_WIDTH), F32),
            jax.ShapeDtypeStruct(state.shape, F32),
        ),
        grid=(t // rows,),
        in_specs=[
            pl.BlockSpec((rows, 1024), lambda n: (n, 0)),
            pl.BlockSpec((rows, 1024), lambda n: (n, 1)),
            pl.BlockSpec((rows, 1024), lambda n: (n, 2)),
            pl.BlockSpec((rows, GLA_HEADS * GLA_DK), lambda n: (n, 0)),
            pl.BlockSpec((1, GLA_WIDTH), lambda n: (0, 0)),
            pl.BlockSpec((group, GLA_HEADS, GLA_DK, GLA_DV), lambda n: (n, 0, 0, 0)),
        ],
        out_specs=(
            pl.BlockSpec((rows, GLA_WIDTH), lambda n: (n, 0)),
            pl.BlockSpec((group, GLA_HEADS, GLA_DK, GLA_DV), lambda n: (n, 0, 0, 0)),
        ),
        scratch_shapes=[pltpu.VMEM((group * GLA_HEADS, GLA_DV, GLA_DK), F32)],
        compiler_params=_cparams(("arbitrary",)),
        name="gla_sample",
    )(gla_in, gla_in, gla_in, log_a, norm_g, state)


def _multiplicity(delta):
    total = jnp.zeros(delta.shape, F32)
    for d in DILATIONS:
        assert d & (d - 1) == 0
        hit = jnp.where(jnp.bitwise_and(delta, d - 1) == 0, 1.0, 0.0)
        total = total + jnp.where(delta <= BAND * d, hit, 0.0)
    return jnp.where(delta >= 0, total, 0.0)


def _sample_attn_kernel(q_ref, kn_ref, vn_ref, ckt_ref, cvt_ref, o_ref):
    steps = q_ref.shape[0]
    rows = ATT_HEADS * steps
    cache_len = ckt_ref.shape[2]
    qt = jnp.concatenate([q_ref[...]] * ATT_HEADS, axis=0)
    rh = lax.broadcasted_iota(jnp.int32, (rows, ATT_WIDTH), 0) // steps
    ch = lax.broadcasted_iota(jnp.int32, (rows, ATT_WIDTH), 1) // ATT_HEAD_DIM
    own = rh == ch
    qb = jnp.where(own, qt, 0.0).astype(BF16)
    pad = jnp.zeros((LANES - steps, ATT_WIDTH), F32)
    kn = jnp.concatenate([kn_ref[...], pad], axis=0).astype(BF16)
    vn = jnp.concatenate([vn_ref[...], pad], axis=0).astype(BF16)
    assert steps & (steps - 1) == 0
    t_c = jnp.bitwise_and(lax.broadcasted_iota(jnp.int32, (rows, cache_len), 0), steps - 1)
    mc = _multiplicity(cache_len + t_c - lax.broadcasted_iota(jnp.int32, (rows, cache_len), 1))
    t_n = jnp.bitwise_and(lax.broadcasted_iota(jnp.int32, (rows, LANES), 0), steps - 1)
    new = lax.broadcasted_iota(jnp.int32, (rows, LANES), 1)
    mn = jnp.where(new < steps, _multiplicity(t_n - new), 0.0)
    sc = jnp.where(mc > 0.0, _dot(qb, ckt_ref[0].astype(BF16)), NEG)
    sn = jnp.where(mn > 0.0, _dot_nt(qb, kn), NEG)
    m = jnp.maximum(jnp.max(sc, axis=-1, keepdims=True), jnp.max(sn, axis=-1, keepdims=True))
    pc = mc * jnp.exp(sc - m)
    pn = mn * jnp.exp(sn - m)
    den = jnp.sum(pc, axis=-1, keepdims=True) + jnp.sum(pn, axis=-1, keepdims=True)
    o = _dot_nt(pc.astype(BF16), cvt_ref[0].astype(BF16)) + _dot(pn.astype(BF16), vn)
    o = jnp.where(own, o / den, 0.0)
    o_ref[...] = jnp.sum(o.reshape(ATT_HEADS, steps, ATT_WIDTH), axis=0)


def _sample_attention(q, k_new, v_new, cache_kt, cache_vt, *, steps):
    nb, _, cache_len = cache_kt.shape
    tok = pl.BlockSpec((steps, ATT_WIDTH), lambda n: (n, 0))
    cache = pl.BlockSpec((1, ATT_WIDTH, cache_len), lambda n: (n, 0, 0))
    return pl.pallas_call(
        _sample_attn_kernel,
        out_shape=jax.ShapeDtypeStruct((nb * steps, ATT_WIDTH), F32),
        grid=(nb,),
        in_specs=[tok, tok, tok, cache, cache],
        out_specs=tok,
        compiler_params=_cparams(("arbitrary",)),
        name="sample_attn",
    )(q, k_new, v_new, cache_kt, cache_vt)


def _outproj_kernel(att_ref, gla_ref, x_ref, gate_ref, ng_ref, w_ref, g2_ref, sc2_ref, sh2_ref,
                    h_ref, hn_ref):
    att = jnp.concatenate([att_ref[hp].astype(F32) for hp in range(HEAD_PAIRS)], axis=1)
    rs = lax.rsqrt(jnp.mean(att * att, axis=-1, keepdims=True) + EPS)
    att_y = (att * rs * ng_ref[...]).astype(BF16)
    mix = _dot(att_y, w_ref[:ATT_WIDTH, :]) + _dot(gla_ref[...], w_ref[ATT_WIDTH:, :])
    h = x_ref[...] + gate_ref[...] * mix
    h_ref[...] = h
    hn_ref[...] = _norm_mod(h, g2_ref[...], sc2_ref[...], sh2_ref[...]).astype(BF16)


def _outproj(att, gla_y, x, gate, att_norm_g, w_out, g2, sc2, sh2, *, tm):
    t = x.shape[0]
    mod_rows = gate.shape[0]
    mod_map = (lambda i: (0, 0)) if mod_rows == 1 else (lambda i: (i, 0))
    mod = pl.BlockSpec((1, D_MODEL) if mod_rows == 1 else (tm, D_MODEL), mod_map)
    row = pl.BlockSpec((tm, D_MODEL), lambda i: (i, 0))
    return pl.pallas_call(
        _outproj_kernel,
        out_shape=(jax.ShapeDtypeStruct((t, D_MODEL), F32),
                   jax.ShapeDtypeStruct((t, D_MODEL), BF16)),
        grid=(t // tm,),
        in_specs=[
            pl.BlockSpec((HEAD_PAIRS, tm, LANES), lambda i: (0, i, 0)),
            pl.BlockSpec((tm, GLA_WIDTH), lambda i: (i, 0)),
            row, mod,
            pl.BlockSpec((1, ATT_WIDTH), lambda i: (0, 0)),
            pl.BlockSpec((ATT_WIDTH + GLA_WIDTH, D_MODEL), lambda i: (0, 0)),
            pl.BlockSpec((1, D_MODEL), lambda i: (0, 0)),
            mod, mod,
        ],
        out_specs=(row, row),
        compiler_params=_cparams(("arbitrary",)),
        name="outproj",
    )(att, gla_y, x, gate, att_norm_g, w_out, g2, sc2, sh2)


def _ffn_kernel(h_hbm, hn_ref, gate_ref, fg_ref, wu_ref, wd_ref, y_ref, h_ref, h_sem):
    i = pl.program_id(0)
    f = pl.program_id(1)
    last = pl.num_programs(1) - 1
    tm = hn_ref.shape[0]

    def residual_copy():
        rows = pl.ds(pl.multiple_of(i * tm, tm), tm)
        return pltpu.make_async_copy(h_hbm.at[rows, :], h_ref, h_sem)

    def hidden():
        u = jnp.maximum(_dot(hn_ref[...], wu_ref[...]), 0.0)
        return (u * u).astype(BF16)

    @pl.when(f == 0)
    def _():
        residual_copy().start()
        y_ref[...] = _dot(hidden(), wd_ref[...])

    @pl.when(jnp.logical_and(f > 0, f < last))
    def _():
        y_ref[...] += _dot(hidden(), wd_ref[...])

    @pl.when(f == last)
    def _():
        u2 = hidden()
        residual_copy().wait()
        width = D_MODEL // 4
        ss = jnp.zeros((h_ref.shape[0], 1), F32)
        for c in range(0, D_MODEL, width):
            cols = slice(c, c + width)
            y = y_ref[:, cols] + _dot(u2, wd_ref[:, cols])
            h2 = h_ref[:, cols] + gate_ref[:, cols] * y
            y_ref[:, cols] = h2
            ss = ss + jnp.sum(h2 * h2, axis=-1, keepdims=True)
        y_ref[...] = y_ref[...] * lax.rsqrt(ss / D_MODEL + EPS) * fg_ref[...]


def _ffn(h, hn, gate, final_g, w_up, w_down, *, tm, tf):
    t = h.shape[0]
    assert D_FF // tf >= 2, "first and last hidden chunks are separate code paths"
    row = pl.BlockSpec((tm, D_MODEL), lambda i, f: (i, 0))
    return pl.pallas_call(
        _ffn_kernel,
        out_shape=jax.ShapeDtypeStruct((t, D_MODEL), F32),
        grid=(t // tm, D_FF // tf),
        in_specs=[
            pl.BlockSpec(memory_space=pl.ANY),
            row,
            _mod_spec(gate.shape[0], tm),
            pl.BlockSpec((1, D_MODEL), lambda i, f: (0, 0)),
            pl.BlockSpec((D_MODEL, tf), lambda i, f: (0, f)),
            pl.BlockSpec((tf, D_MODEL), lambda i, f: (f, 0)),
        ],
        out_specs=row,
        scratch_shapes=[pltpu.VMEM((tm, D_MODEL), F32), pltpu.SemaphoreType.DMA(())],
        compiler_params=_cparams(("arbitrary", "arbitrary")),
        name="ffn",
    )(h, hn, gate, final_g, w_up, w_down)


def _split_ada(ada):
    return [ada[:, D_MODEL * i:D_MODEL * (i + 1)] for i in range(6)]


def kernel(x_prompt, x_sample, c_prompt, c_sample, cache_win_k, cache_win_v, state_gla, w_in, gla_gate_w, gla_gate_b, att_norm_g, gla_norm_g, w_out, norm1_g, norm2_g, w_ada, b_ada, w_up, w_down, final_g):
    assert w_in.shape[0] == 1, "single-layer model"
    n_p, t_p, _ = x_prompt.shape
    n_s, t_s, _ = x_sample.shape
    assert n_p == 1
    rows_s = n_s * t_s
    wb_p = min(MAX_WINDOW, t_p)

    w_main = jnp.transpose(w_in[0]).astype(BF16)
    gate_w = gla_gate_w[0].astype(BF16)
    gate_b = gla_gate_b[0][None, :]
    n1g, n2g = norm1_g[0][None, :], norm2_g[0][None, :]
    ang, gng = att_norm_g[0][None, :], gla_norm_g[0][None, :]
    fg = final_g[None, :]

    c_rows = n_p + n_s
    c_pad = -c_rows % 8
    c_all = jnp.concatenate([c_prompt, c_sample, jnp.zeros((c_pad, D_MODEL), F32)], axis=0)
    ada = _ada(c_all, w_ada[0], b_ada[0][None, :])
    sh1_p, sc1_p, g1_p, sh2_p, sc2_p, g2_p = _split_ada(ada[:n_p])
    sh1_s, sc1_s, g1_s, sh2_s, sc2_s, g2_s = _split_ada(
        jnp.repeat(ada[n_p:c_rows], t_s, axis=0))

    xp = x_prompt[0]
    cs_p, sn_p = _rope_tables(jnp.arange(t_p))
    outs = _inproj(xp, n1g, sc1_p, sh1_p, w_main, gate_w, gate_b, cs_p, sn_p,
                   tm=1024, emit_att=True)
    att, w_out_b, w_up_b, w_down_b = _prompt_attention(outs[:3], (w_out[0], w_up[0], w_down[0]))
    gla_in, log_a = outs[3], outs[4]
    win = _qkv_rows(xp, n1g, sc1_p, sh1_p, w_main, cs_p, sn_p,
                    tm=1024, first_row=t_p - wb_p, rows=wb_p, col0=1)
    win_k, win_v = win[0], win[1]
    gla_y, st_p = _gla_prompt(gla_in, log_a, gng, tb=512, chunk=64, sub=16)
    h_p, hn_p = _outproj(att, gla_y, xp, g1_p, ang, w_out_b, n2g, sc2_p, sh2_p, tm=512)
    y_p = _ffn(h_p, hn_p, g2_p, fg, w_up_b, w_down_b, tm=1024, tf=1024)

    xs = x_sample.reshape(rows_s, D_MODEL)
    cs_s, sn_s = _rope_tables(jnp.tile(PAST_LEN + jnp.arange(t_s), n_s))
    gla_in_s, log_a_s = _inproj(xs, n1g, sc1_s, sh1_s, w_main, gate_w, gate_b, cs_s, sn_s,
                                tm=rows_s, emit_att=False)
    qkv_s = _qkv_rows(xs, n1g, sc1_s, sh1_s, w_main, cs_s, sn_s,
                      tm=rows_s, first_row=0, rows=rows_s, col0=0)
    new_k, new_v = qkv_s[1], qkv_s[2]
    wc = cache_win_k.shape[2]
    cache_t = lambda c: jnp.transpose(c[0], (0, 2, 3, 1)).reshape(n_s, ATT_WIDTH, wc)
    att_tok = _sample_attention(qkv_s[0], new_k, new_v, cache_t(cache_win_k), cache_t(cache_win_v),
                                steps=t_s)
    att_s = jnp.transpose(att_tok.reshape(rows_s, HEAD_PAIRS, LANES), (1, 0, 2)).astype(BF16)
    gla_y_s, st_s = _gla_sample(gla_in_s.astype(F32), log_a_s, gng, state_gla[0], steps=t_s,
                                group=4 if n_s % 4 == 0 else 1)
    h_s, hn_s = _outproj(att_s, gla_y_s.astype(BF16), xs, g1_s, ang, w_out_b, n2g, sc2_s, sh2_s,
                         tm=rows_s)
    y_s = _ffn(h_s, hn_s, g2_s, fg, w_up_b, w_down_b, tm=rows_s, tf=2048)

    return (
        y_p[None],
        y_s.reshape(n_s, t_s, D_MODEL),
        win_k.reshape(1, n_p, wb_p, ATT_HEADS, ATT_HEAD_DIM),
        win_v.reshape(1, n_p, wb_p, ATT_HEADS, ATT_HEAD_DIM),
        st_p[None, None],
        new_k.reshape(1, n_s, t_s, ATT_HEADS, ATT_HEAD_DIM),
        new_v.reshape(1, n_s, t_s, ATT_HEADS, ATT_HEAD_DIM),
        st_s[None],
    )
```

```python
import functools

import jax
import jax.numpy as jnp
from jax import lax
from jax.experimental import pallas as pl
from jax.experimental.pallas import tpu as pltpu

F32 = jnp.float32
BF16 = jnp.bfloat16

D_MODEL = 2048
ATT_WIDTH = 1024
ATT_HEADS = 16
ATT_HEAD_DIM = 64
ROT_DIM = 16
ROPE_THETA = 500000.0
DILATIONS = (1, 4, 16)
BAND = 128
MAX_WINDOW = 2048
GLA_HEADS = 4
GLA_DK = 128
GLA_DV = 256
GLA_WIDTH = 1024
GLA_GATE_RANK = 16
GLA_TAU = 16.0
D_FF = 8192
EPS = 1e-6
PAST_LEN = 16384
NEG = -1e30

LANES = 128
HEAD_PAIRS = ATT_WIDTH // LANES
VMEM_LIMIT = 60000 * 1024


def _cparams(sem):
    return pltpu.CompilerParams(dimension_semantics=sem, vmem_limit_bytes=VMEM_LIMIT)


def _dot(a, b):
    return jnp.dot(a, b, preferred_element_type=F32)


def _dot_nt(a, b):
    return lax.dot_general(a, b, (((1,), (1,)), ((), ())), preferred_element_type=F32)


def _dot_tn(a, b):
    return lax.dot_general(a, b, (((0,), (0,)), ((), ())), preferred_element_type=F32)


def _silu(x):
    return x / (1.0 + jnp.exp(-x))


def _ada_kernel(c_ref, w_ref, b_ref, o_ref):
    s = _silu(c_ref[...]).astype(BF16)
    o_ref[...] = _dot(s, w_ref[...].astype(BF16)) + b_ref[...]


def _ada(c, w_ada, b_ada):
    rows = c.shape[0]
    n = w_ada.shape[1]
    tn = 1024
    return pl.pallas_call(
        _ada_kernel,
        out_shape=jax.ShapeDtypeStruct((rows, n), F32),
        grid=(n // tn,),
        in_specs=[
            pl.BlockSpec((rows, D_MODEL), lambda j: (0, 0)),
            pl.BlockSpec((D_MODEL, tn), lambda j: (0, j)),
            pl.BlockSpec((1, tn), lambda j: (0, j)),
        ],
        out_specs=pl.BlockSpec((rows, tn), lambda j: (0, j)),
        compiler_params=_cparams(("arbitrary",)),
        name="ada",
    )(c, w_ada, b_ada)


def _rope(slab, cs, sn):
    d = lax.broadcasted_iota(jnp.int32, (1, LANES), 1) % ATT_HEAD_DIM
    half = ROT_DIM // 2
    from_hi = jnp.where(d < half, -1.0, 0.0)
    from_lo = jnp.where((d >= half) & (d < ROT_DIM), 1.0, 0.0)
    return (slab * cs + pltpu.roll(slab, LANES - half, 1) * (sn * from_hi)
            + pltpu.roll(slab, half, 1) * (sn * from_lo))


def _norm_mod(x, g, sc, sh):
    rs = lax.rsqrt(jnp.mean(x * x, axis=-1, keepdims=True) + EPS)
    return x * rs * g * (1.0 + sc) + sh


def _inproj_kernel(*refs, emit_att, first_tail):
    (x_ref, g_ref, sc_ref, sh_ref, w_ref, wag_ref, gw_ref, gb_ref, cs_ref, sn_ref), refs = (
        refs[:10], refs[10:])
    i = pl.program_id(0)
    j = pl.program_id(1)
    if emit_att:
        (out1, out4, out16), refs = refs[:3], refs[3:]
        gla_ref, la_ref, tail_ref, xn_ref, s0_ref, s1_ref, x_buf, x_sem = refs
        tm = x_buf.shape[0]
        x_hbm, x_ref = x_ref, x_buf

        def x_copy(tile):
            rows = pl.ds(pl.multiple_of(tile * tm, tm), tm)
            return pltpu.make_async_copy(x_hbm.at[rows, :], x_buf, x_sem)

        @pl.when(jnp.logical_and(i == 0, j == 0))
        def _():
            x_copy(0).start()

        @pl.when(jnp.logical_and(j == 1, i + 1 < pl.num_programs(0)))
        def _():
            x_copy(i + 1).start()
    else:
        gla_ref, la_ref, xn_ref = refs[:3]
    col = j if emit_att else j + 3

    @pl.when(j == 0)
    def _():
        if emit_att:
            x_copy(i).wait()
        xnb = _norm_mod(x_ref[...], g_ref[...], sc_ref[...], sh_ref[...]).astype(BF16)
        xn_ref[...] = xnb
        ag = _dot_nt(xnb, wag_ref[...])
        z = _dot(ag.astype(BF16), gw_ref[...]) + gb_ref[...]
        log_sig = jnp.minimum(z, 0.0) - jnp.log(1.0 + jnp.exp(-jnp.abs(z)))
        la_ref[...] = log_sig / GLA_TAU

    if emit_att:
        n4, n16 = tm // 4, tm // 16
        group = 2 * LANES
        in_tail = i >= first_tail

        def emit(slab_fn, keep_f32=False):
            for hp in range(HEAD_PAIRS):
                half = hp % 2
                if half == 0:
                    acc = _dot_nt(xn_ref[...], w_ref[LANES * hp:LANES * hp + group, :])
                slab = slab_fn(acc[:, LANES * half:LANES * (half + 1)])
                out1[0, hp] = slab.astype(BF16)
                if keep_f32:
                    tail_ref[0, :, LANES * hp:LANES * (hp + 1)] = slab
                s0_ref[half] = slab
                for r in range(4):
                    part = s0_ref[half, pl.ds(r, n4, stride=4), :]
                    out4[0, hp, :, LANES * r:LANES * (r + 1)] = part.astype(BF16)
                    s1_ref[half, n4 * r:n4 * (r + 1), :] = part
                for r in range(16):
                    part = s1_ref[half, pl.ds(n4 * (r % 4) + r // 4, n16, stride=4), :]
                    out16[0, hp, :, LANES * r:LANES * (r + 1)] = part.astype(BF16)

        @pl.when(col == 0)
        def _():
            emit(lambda s: _rope(s, cs_ref[...], sn_ref[...]) * (ATT_HEAD_DIM ** -0.5))

        for c, slab_fn in ((1, lambda s: _rope(s, cs_ref[...], sn_ref[...])), (2, lambda s: s)):
            pl.when(jnp.logical_and(col == c, jnp.logical_not(in_tail)))(
                functools.partial(emit, slab_fn))
            pl.when(jnp.logical_and(col == c, in_tail))(
                functools.partial(emit, slab_fn, keep_f32=True))

    @pl.when(col == 3)
    def _():
        acc = _dot_nt(xn_ref[...], w_ref[...])
        half = GLA_HEADS * GLA_DK
        gla_ref[:, :half] = (acc[:, :half] * (GLA_DK ** -0.5)).astype(BF16)
        gla_ref[:, half:] = acc[:, half:].astype(BF16)

    @pl.when(col >= 4)
    def _():
        gla_ref[...] = _dot_nt(xn_ref[...], w_ref[...]).astype(BF16)


def _mod_spec(mod_rows, tm):
    if mod_rows == 1:
        return pl.BlockSpec((1, D_MODEL), lambda i, j: (0, 0))
    return pl.BlockSpec((tm, D_MODEL), lambda i, j: (i, 0))


def _inproj(x, g, sc, sh, w_all, gate_w, gate_b, cs, sn, *, tm, emit_att, tail_rows=0):
    t = x.shape[0]
    col0 = 0 if emit_att else 3
    first_tail = (t - tail_rows) // tm
    mod = _mod_spec(sc.shape[0], tm)
    const = lambda block: pl.BlockSpec(block, lambda i, j: (0, 0))
    out_shape, out_specs, scratch = [], [], [pltpu.VMEM((tm, D_MODEL), BF16)]
    if emit_att:
        for d in DILATIONS:
            out_shape.append(jax.ShapeDtypeStruct((3, HEAD_PAIRS, t // d, d * LANES), BF16))
            out_specs.append(pl.BlockSpec((1, HEAD_PAIRS, tm // d, d * LANES),
                                          lambda i, j: (jnp.minimum(j, 2), 0, i, 0)))
        scratch += [pltpu.VMEM((2, tm, LANES), F32) for _ in range(2)]
        scratch += [pltpu.VMEM((tm, D_MODEL), F32), pltpu.SemaphoreType.DMA(())]
    out_shape += [jax.ShapeDtypeStruct((t, 3 * 1024), BF16),
                  jax.ShapeDtypeStruct((t, GLA_HEADS * GLA_DK), F32)]
    out_specs += [pl.BlockSpec((tm, 1024), lambda i, j: (i, jnp.maximum(j + col0 - 3, 0))),
                  pl.BlockSpec((tm, GLA_HEADS * GLA_DK), lambda i, j: (i, 0))]
    if emit_att:
        assert tail_rows % tm == 0 and 0 < tail_rows <= t
        out_shape.append(jax.ShapeDtypeStruct((2, tail_rows, ATT_WIDTH), F32))
        out_specs.append(pl.BlockSpec(
            (1, tm, ATT_WIDTH),
            lambda i, j: (jnp.where(i < first_tail, 0, jnp.clip(j - 1, 0, 1)),
                          jnp.maximum(i - first_tail, 0), 0)))
    gate_rows = (6 * 1024) // GLA_GATE_RANK
    return pl.pallas_call(
        functools.partial(_inproj_kernel, emit_att=emit_att, first_tail=first_tail),
        out_shape=tuple(out_shape),
        grid=(t // tm, 6 - col0),
        in_specs=[
            (pl.BlockSpec(memory_space=pl.ANY) if emit_att
             else pl.BlockSpec((tm, D_MODEL), lambda i, j: (i, 0))),
            const((1, D_MODEL)),
            mod, mod,
            pl.BlockSpec((1024, D_MODEL), lambda i, j: (j + col0, 0)),
            pl.BlockSpec((GLA_GATE_RANK, D_MODEL), lambda i, j: (gate_rows, 0)),
            const((GLA_GATE_RANK, GLA_HEADS * GLA_DK)),
            const((1, GLA_HEADS * GLA_DK)),
            pl.BlockSpec((tm, LANES), lambda i, j: (i, 0)),
            pl.BlockSpec((tm, LANES), lambda i, j: (i, 0)),
        ],
        out_specs=tuple(out_specs),
        scratch_shapes=scratch,
        compiler_params=_cparams(("arbitrary", "arbitrary")),
        name="inproj_att" if emit_att else "inproj_gla",
    )(x, g, sc, sh, w_all, w_all, gate_w, gate_b, cs, sn)


def _qkv_rows_kernel(x_ref, g_ref, sc_ref, sh_ref, w_ref, cs_ref, sn_ref, o_ref, xn_ref, *, col0):
    j = pl.program_id(1)

    @pl.when(j == 0)
    def _():
        xn_ref[...] = _norm_mod(x_ref[...], g_ref[...], sc_ref[...], sh_ref[...]).astype(BF16)

    acc = _dot_nt(xn_ref[...], w_ref[...])
    col = j + col0

    @pl.when(col <= 1)
    def _():
        scale = jnp.where(col == 0, ATT_HEAD_DIM ** -0.5, 1.0)
        for hp in range(HEAD_PAIRS):
            lanes = slice(LANES * hp, LANES * (hp + 1))
            o_ref[0, :, lanes] = _rope(acc[:, lanes], cs_ref[...], sn_ref[...]) * scale

    @pl.when(col == 2)
    def _():
        o_ref[0] = acc


def _qkv_rows(x, g, sc, sh, w_main, cs, sn, *, tm, first_row, rows, col0):
    first = first_row // tm
    assert sc.shape[0] == 1 or first == 0
    ncols = 3 - col0
    mod = _mod_spec(sc.shape[0], tm)
    return pl.pallas_call(
        functools.partial(_qkv_rows_kernel, col0=col0),
        out_shape=jax.ShapeDtypeStruct((ncols, rows, ATT_WIDTH), F32),
        grid=(rows // tm, ncols),
        in_specs=[
            pl.BlockSpec((tm, D_MODEL), lambda i, j: (i + first, 0)),
            pl.BlockSpec((1, D_MODEL), lambda i, j: (0, 0)),
            mod, mod,
            pl.BlockSpec((1024, D_MODEL), lambda i, j: (j + col0, 0)),
            pl.BlockSpec((tm, LANES), lambda i, j: (i + first, 0)),
            pl.BlockSpec((tm, LANES), lambda i, j: (i + first, 0)),
        ],
        out_specs=pl.BlockSpec((1, tm, ATT_WIDTH), lambda i, j: (j, i, 0)),
        scratch_shapes=[pltpu.VMEM((tm, D_MODEL), BF16)],
        compiler_params=_cparams(("arbitrary", "arbitrary")),
        name="qkv_rows",
    )(x, g, sc, sh, w_main, cs, sn)


def _rope_tables(pos):
    half = ROT_DIM // 2
    inv_freq = ROPE_THETA ** (-jnp.arange(half, dtype=F32) / half)
    per_head = jnp.concatenate([inv_freq, inv_freq, jnp.zeros((ATT_HEAD_DIM - ROT_DIM,), F32)])
    ang = pos.astype(F32)[:, None] * jnp.tile(per_head, LANES // ATT_HEAD_DIM)[None, :]
    return jnp.cos(ang), jnp.sin(ang)


def _attn_kernel(*refs, n_cast):
    (q1, k1c, v1c, k1p, v1p, q4, k4c, v4c, k4p, v4p, q16, k16c, v16c, k16p, v16p), refs = (
        refs[:15], refs[15:])
    cast_in, o_ref, cast_out = refs[:n_cast], refs[n_cast], refs[n_cast + 1:2 * n_cast + 1]
    a1, m1, l1, a4, m4, l4, t16, s_scr, p_scr, bias_ref, o_scr = refs[2 * n_cast + 1:]
    for src, dst in zip(cast_in, cast_out):
        dst[...] = src[...].astype(dst.dtype)

    first_block = pl.program_id(1) == 0
    lane = lax.broadcasted_iota(jnp.int32, (1, LANES), 1)
    first = lane < ATT_HEAD_DIM
    head0 = jnp.where(first, 1.0, 0.0).astype(BF16)
    head1 = jnp.where(first, 0.0, 1.0).astype(BF16)
    ones = jnp.ones((2 * BAND, LANES), BF16)

    qi = lax.broadcasted_iota(jnp.int32, (2 * BAND, 2 * BAND), 0) % BAND
    kk = lax.broadcasted_iota(jnp.int32, (2 * BAND, 2 * BAND), 1)
    full = jnp.where(jnp.where(kk < BAND, kk - qi, qi - (kk - BAND)) >= 0, 0.0, NEG)
    bias_ref[0] = full
    bias_ref[1] = jnp.where(kk < BAND, full + jnp.where(first_block, NEG, 0.0), full)

    n1 = q1.shape[1] // BAND
    rows4 = q4.shape[1]
    n4 = rows4 // BAND

    def window(cur, prev, i, cols):
        if i == 0:
            return jnp.concatenate([prev[0, :, cols], cur[0, :BAND, cols]], axis=0)
        return cur[0, BAND * (i - 1):BAND * (i + 1), cols]

    jobs = []
    everything = slice(None)
    for i in range(n1):
        jobs.append(dict(q=(q1, i, everything), k=(k1c, k1p), v=(v1c, v1p), bias=int(i == 0),
                         dst=(a1, m1, l1, BAND * i)))
    for r in range(4):
        cols = slice(LANES * r, LANES * (r + 1))
        for i in range(n4):
            jobs.append(dict(q=(q4, i, cols), k=(k4c, k4p), v=(v4c, v4p), bias=int(i == 0),
                             dst=(a4, m4, l4, rows4 * r + BAND * i)))
    for r in range(16):
        cols = slice(LANES * r, LANES * (r + 1))
        jobs.append(dict(q=(q16, 0, cols), k=(k16c, k16p), v=(v16c, v16p), bias=1, merge=r))

    def scores(job, slot):
        ref, i, cols = job["q"]
        q = ref[0, BAND * i:BAND * (i + 1), cols]
        q2 = jnp.concatenate([q * head0, q * head1], axis=0)
        s_scr[slot] = _dot_nt(q2, window(*job["k"], i, cols))

    def softmax(job, slot):
        s = s_scr[slot] + bias_ref[job["bias"]]
        m = jnp.max(s, axis=-1, keepdims=True)
        p_scr[slot] = jnp.exp(s - m).astype(BF16)
        m = jnp.where(first, m[:BAND], m[BAND:])
        if "merge" in job:
            t16[slot] = m
        else:
            job["dst"][1][pl.ds(job["dst"][3], BAND), :] = m

    def values(job, slot):
        _, i, cols = job["q"]
        vw = jnp.concatenate([window(*job["v"], i, cols), ones], axis=1)
        o2 = _dot(p_scr[slot], vw)
        acc = jnp.where(first, o2[:BAND, :LANES], o2[BAND:, :LANES])
        den = jnp.where(first, o2[:BAND, LANES:], o2[BAND:, LANES:])
        if "merge" not in job:
            a_ref, _, l_ref, row = job["dst"]
            a_ref[pl.ds(row, BAND), :] = acc
            l_ref[pl.ds(row, BAND), :] = den
            return
        r = job["merge"]
        sl1 = pl.ds(r, BAND, stride=16)
        sl4 = pl.ds((r % 4) * rows4 + r // 4, BAND, stride=4)
        ma, mb, mc = m1[sl1, :], m4[sl4, :], t16[slot]
        m = jnp.maximum(jnp.maximum(ma, mb), mc)
        wa, wb, wc = jnp.exp(ma - m), jnp.exp(mb - m), jnp.exp(mc - m)
        num = wa * a1[sl1, :] + wb * a4[sl4, :] + wc * acc
        dsum = wa * l1[sl1, :] + wb * l4[sl4, :] + wc * den
        o_scr[sl1, :] = num / dsum

    slots = s_scr.shape[0]
    for step in range(len(jobs) + 2):
        if step < len(jobs):
            scores(jobs[step], step % slots)
        if 1 <= step <= len(jobs):
            softmax(jobs[step - 1], (step - 1) % slots)
        if step >= 2:
            values(jobs[step - 2], (step - 2) % slots)
    o_ref[0] = o_scr[...].astype(o_ref.dtype)


def _prompt_attention(views, weights):
    _, hp, t, _ = views[0].shape
    sb = 16 * BAND
    ns = t // sb
    in_specs, args = [], []
    for arr, d in zip(views, DILATIONS):
        rows, width = sb // d, d * LANES
        nb = rows // BAND

        def cur(c, rows=rows, width=width):
            return pl.BlockSpec((None, 1, rows, width), lambda h, s: (c, h, s, 0))

        def prev(c, width=width, nb=nb):
            return pl.BlockSpec((None, 1, BAND, width),
                                lambda h, s: (c, h, jnp.maximum(s * nb - 1, 0), 0))

        in_specs += [cur(0), cur(1), cur(2), prev(1), prev(2)]
        args += [arr] * 5
    cast_specs = []
    for w in weights:
        slab = w.shape[0] // (hp * ns)
        assert slab * hp * ns == w.shape[0] and slab % 16 == 0
        cast_specs.append(pl.BlockSpec((slab, w.shape[1]), lambda h, s: (h * ns + s, 0)))
    tile = (2 * BAND, 2 * BAND)
    return pl.pallas_call(
        functools.partial(_attn_kernel, n_cast=len(weights)),
        out_shape=tuple([jax.ShapeDtypeStruct((hp, t, LANES), BF16)]
                        + [jax.ShapeDtypeStruct(w.shape, BF16) for w in weights]),
        grid=(hp, ns),
        in_specs=in_specs + cast_specs,
        out_specs=tuple([pl.BlockSpec((1, sb, LANES), lambda h, s: (h, s, 0))] + cast_specs),
        scratch_shapes=(
            [pltpu.VMEM((sb, LANES), F32) for _ in range(6)]
            + [pltpu.VMEM((3, BAND, LANES), F32),
               pltpu.VMEM((3,) + tile, F32),
               pltpu.VMEM((3,) + tile, BF16),
               pltpu.VMEM((2,) + tile, F32),
               pltpu.VMEM((sb, LANES), F32)]),
        compiler_params=_cparams(("arbitrary", "arbitrary")),
        name="prompt_attn",
    )(*args, *weights)


def _gla_chunk(q, k, v, la, st_ref, h, tri, sub):
    c = q.shape[0]
    la_hi = la.astype(BF16)
    la_lo = (la - la_hi.astype(F32)).astype(BF16)
    b = _dot(tri, la_hi) + _dot(tri, la_lo)
    st = st_ref[h]
    o = _dot_nt((q * jnp.exp(b)).astype(BF16), st.astype(BF16))
    srow = lax.broadcasted_iota(jnp.int32, (c, GLA_DK), 0)
    att_rows = []
    for i in range(c // sub):
        lo, hi = i * sub, (i + 1) * sub
        ref = b[lo:lo + 1, :]
        qh = (q[lo:hi] * jnp.exp(b[lo:hi] - ref)).astype(BF16)
        kh = (k * jnp.exp(jnp.where(srow < hi, ref - b, 0.0))).astype(BF16)
        a = _dot_nt(qh, kh)
        ti = lax.broadcasted_iota(jnp.int32, (sub, c), 0) + lo
        si = lax.broadcasted_iota(jnp.int32, (sub, c), 1)
        att_rows.append(jnp.where(si <= ti, a, 0.0))
    att = att_rows[0] if len(att_rows) == 1 else jnp.concatenate(att_rows, axis=0)
    o = o + _dot(att.astype(BF16), v)
    b_last = b[c - 1:c, :]
    kbar = (k * jnp.exp(b_last - b)).astype(BF16)
    st_ref[h] = st * jnp.exp(b_last) + _dot_tn(v, kbar)
    return o


def _gla_out(o, g, gg):
    rs = lax.rsqrt(jnp.mean(o * o, axis=-1, keepdims=True) + EPS)
    return o * rs * g * _silu(gg)


def _tri(c):
    r = lax.broadcasted_iota(jnp.int32, (c, c), 0)
    s = lax.broadcasted_iota(jnp.int32, (c, c), 1)
    return jnp.where(s <= r, 1.0, 0.0).astype(BF16)


def _gla_prompt_kernel(qk_ref, v_ref, gg_ref, la_ref, g_ref, y_ref, s_ref, st_ref,
                       qd_ref, kl_ref, qs_ref, ks_ref, dec_ref, att_ref, o_ref, *, chunk, sub):
    i = pl.program_id(0)

    @pl.when(i == 0)
    def _():
        st_ref[...] = jnp.zeros_like(st_ref)

    tb = qk_ref.shape[0]
    nchunks, nsub = tb // chunk, chunk // sub
    half = GLA_HEADS * GLA_DK

    r = lax.broadcasted_iota(jnp.int32, (tb, tb), 0)
    c = lax.broadcasted_iota(jnp.int32, (tb, tb), 1)
    start = r - jnp.bitwise_and(r, chunk - 1)
    tri = jnp.where(c <= r, jnp.where(c >= start, 1.0, 0.0), 0.0).astype(BF16)
    la = la_ref[...]
    la_hi = la.astype(BF16)
    la_lo = (la - la_hi.astype(F32)).astype(BF16)
    b = _dot(tri, la_hi) + _dot(tri, la_lo)
    q = qk_ref[:, :half].astype(F32)
    k = qk_ref[:, half:].astype(F32)
    qd_ref[...] = (q * jnp.exp(b)).astype(BF16)
    srow = lax.broadcasted_iota(jnp.int32, (chunk, half), 0)
    for ci in range(nchunks):
        lo = ci * chunk
        bc, kc_ = b[lo:lo + chunk], k[lo:lo + chunk]
        b_last = bc[chunk - 1:chunk]
        dec_ref[ci:ci + 1, :] = jnp.exp(b_last)
        kl_ref[lo:lo + chunk, :] = (kc_ * jnp.exp(b_last - bc)).astype(BF16)
        for si in range(nsub):
            s_lo, s_hi = si * sub, (si + 1) * sub
            ref = bc[s_lo:s_lo + 1]
            qs_ref[lo + s_lo:lo + s_hi, :] = (
                q[lo + s_lo:lo + s_hi] * jnp.exp(bc[s_lo:s_hi] - ref)).astype(BF16)
            ks_ref[si, lo:lo + chunk, :] = (
                kc_ * jnp.exp(jnp.where(srow < s_hi, ref - bc, 0.0))).astype(BF16)

    ti = lax.broadcasted_iota(jnp.int32, (chunk, chunk), 0)
    si_ = lax.broadcasted_iota(jnp.int32, (chunk, chunk), 1)
    causal = si_ <= ti
    jobs = [(ci, h) for ci in range(nchunks) for h in range(GLA_HEADS)]

    def intra(job, slot):
        ci, h = job
        lo, kc = ci * chunk, slice(GLA_DK * h, GLA_DK * (h + 1))
        parts = [_dot_nt(qs_ref[lo + s * sub:lo + (s + 1) * sub, kc], ks_ref[s, lo:lo + chunk, kc])
                 for s in range(nsub)]
        att_ref[slot] = jnp.where(causal, jnp.concatenate(parts, axis=0), 0.0).astype(BF16)

    def state(job, slot):
        ci, h = job
        rows = slice(ci * chunk, (ci + 1) * chunk)
        kc, vc = slice(GLA_DK * h, GLA_DK * (h + 1)), slice(GLA_DV * h, GLA_DV * (h + 1))
        st = st_ref[h]
        v = v_ref[rows, vc]
        o_ref[slot] = _dot_nt(qd_ref[rows, kc], st.astype(BF16)) + _dot(att_ref[slot], v)
        st_ref[h] = st * dec_ref[ci:ci + 1, kc] + _dot_tn(v, kl_ref[rows, kc])

    def finish(job, slot):
        ci, h = job
        rows, vc = slice(ci * chunk, (ci + 1) * chunk), slice(GLA_DV * h, GLA_DV * (h + 1))
        y_ref[rows, vc] = _gla_out(o_ref[slot], g_ref[:, vc],
                                   gg_ref[rows, vc].astype(F32)).astype(y_ref.dtype)

    slots = att_ref.shape[0]
    for step in range(len(jobs) + 2):
        if step < len(jobs):
            intra(jobs[step], step % slots)
        if 1 <= step <= len(jobs):
            state(jobs[step - 1], (step - 1) % slots)
        if step >= 2:
            finish(jobs[step - 2], (step - 2) % slots)

    @pl.when(i == pl.num_programs(0) - 1)
    def _():
        for h in range(GLA_HEADS):
            s_ref[h] = st_ref[h].T


def _gla_prompt(gla_in, log_a, norm_g, *, tb, chunk, sub):
    t = gla_in.shape[0]
    return pl.pallas_call(
        functools.partial(_gla_prompt_kernel, chunk=chunk, sub=sub),
        out_shape=(
            jax.ShapeDtypeStruct((t, GLA_WIDTH), BF16),
            jax.ShapeDtypeStruct((GLA_HEADS, GLA_DK, GLA_DV), F32),
        ),
        grid=(t // tb,),
        in_specs=[
            pl.BlockSpec((tb, 1024), lambda i: (i, 0)),
            pl.BlockSpec((tb, 1024), lambda i: (i, 1)),
            pl.BlockSpec((tb, 1024), lambda i: (i, 2)),
            pl.BlockSpec((tb, GLA_HEADS * GLA_DK), lambda i: (i, 0)),
            pl.BlockSpec((1, GLA_WIDTH), lambda i: (0, 0)),
        ],
        out_specs=(
            pl.BlockSpec((tb, GLA_WIDTH), lambda i: (i, 0)),
            pl.BlockSpec((GLA_HEADS, GLA_DK, GLA_DV), lambda i: (0, 0, 0)),
        ),
        scratch_shapes=[
            pltpu.VMEM((GLA_HEADS, GLA_DV, GLA_DK), F32),
            pltpu.VMEM((tb, GLA_HEADS * GLA_DK), BF16),
            pltpu.VMEM((tb, GLA_HEADS * GLA_DK), BF16),
            pltpu.VMEM((tb, GLA_HEADS * GLA_DK), BF16),
            pltpu.VMEM((chunk // sub, tb, GLA_HEADS * GLA_DK), BF16),
            pltpu.VMEM((tb // chunk, GLA_HEADS * GLA_DK), F32),
            pltpu.VMEM((3, chunk, chunk), BF16),
            pltpu.VMEM((3, chunk, GLA_DV), F32),
        ],
        compiler_params=_cparams(("arbitrary",)),
        name="gla_prompt",
    )(gla_in, gla_in, gla_in, log_a, norm_g)


def _gla_sample_kernel(qk_ref, v_ref, gg_ref, la_ref, g_ref, s0_ref, y_ref, s_ref, st_ref, *, steps):
    group = s0_ref.shape[0]
    c = 2 * steps
    tri = _tri(c)
    half = GLA_HEADS * GLA_DK
    pad = lambda a: jnp.concatenate([a, jnp.zeros_like(a)], axis=0)
    for n in range(group):
        rows = slice(n * steps, (n + 1) * steps)
        for h in range(GLA_HEADS):
            idx = n * GLA_HEADS + h
            kc = slice(GLA_DK * h, GLA_DK * (h + 1))
            vc = slice(GLA_DV * h, GLA_DV * (h + 1))
            st_ref[idx] = s0_ref[n, h].T
            q = pad(qk_ref[rows, kc])
            k = pad(qk_ref[rows, half + GLA_DK * h:half + GLA_DK * (h + 1)])
            v = pad(v_ref[rows, vc]).astype(BF16)
            o = _gla_chunk(q, k, v, pad(la_ref[rows, kc]), st_ref, idx, tri, c)
            y_ref[rows, vc] = _gla_out(o[:steps], g_ref[:, vc], gg_ref[rows, vc])
            s_ref[n, h] = st_ref[idx].T


def _gla_sample(gla_in, log_a, norm_g, state, *, steps, group):
    t = gla_in.shape[0]
    rows = steps * group
    return pl.pallas_call(
        functools.partial(_gla_sample_kernel, steps=steps),
        out_shape=(
            jax.ShapeDtypeStruct((t, GLA_WIDTH), F32),
            jax.ShapeDtypeStruct(state.shape, F32),
        ),
        grid=(t // rows,),
        in_specs=[
            pl.BlockSpec((rows, 1024), lambda n: (n, 0)),
            pl.BlockSpec((rows, 1024), lambda n: (n, 1)),
            pl.BlockSpec((rows, 1024), lambda n: (n, 2)),
            pl.BlockSpec((rows, GLA_HEADS * GLA_DK), lambda n: (n, 0)),
            pl.BlockSpec((1, GLA_WIDTH), lambda n: (0, 0)),
            pl.BlockSpec((group, GLA_HEADS, GLA_DK, GLA_DV), lambda n: (n, 0, 0, 0)),
        ],
        out_specs=(
            pl.BlockSpec((rows, GLA_WIDTH), lambda n: (n, 0)),
            pl.BlockSpec((group, GLA_HEADS, GLA_DK, GLA_DV), lambda n: (n, 0, 0, 0)),
        ),
        scratch_shapes=[pltpu.VMEM((group * GLA_HEADS, GLA_DV, GLA_DK), F32)],
        compiler_params=_cparams(("arbitrary",)),
        name="gla_sample",
    )(gla_in, gla_in, gla_in, log_a, norm_g, state)


def _multiplicity(delta):
    total = jnp.zeros(delta.shape, F32)
    for d in DILATIONS:
        assert d & (d - 1) == 0
        hit = jnp.where(jnp.bitwise_and(delta, d - 1) == 0, 1.0, 0.0)
        total = total + jnp.where(delta <= BAND * d, hit, 0.0)
    return jnp.where(delta >= 0, total, 0.0)


def _sample_attn_kernel(q_ref, kn_ref, vn_ref, ckt_ref, cvt_ref, o_ref):
    steps = q_ref.shape[0]
    rows = ATT_HEADS * steps
    cache_len = ckt_ref.shape[2]
    qt = jnp.concatenate([q_ref[...]] * ATT_HEADS, axis=0)
    rh = lax.broadcasted_iota(jnp.int32, (rows, ATT_WIDTH), 0) // steps
    ch = lax.broadcasted_iota(jnp.int32, (rows, ATT_WIDTH), 1) // ATT_HEAD_DIM
    own = rh == ch
    qb = jnp.where(own, qt, 0.0).astype(BF16)
    pad = jnp.zeros((LANES - steps, ATT_WIDTH), F32)
    kn = jnp.concatenate([kn_ref[...], pad], axis=0).astype(BF16)
    vn = jnp.concatenate([vn_ref[...], pad], axis=0).astype(BF16)
    assert steps & (steps - 1) == 0
    t_c = jnp.bitwise_and(lax.broadcasted_iota(jnp.int32, (rows, cache_len), 0), steps - 1)
    mc = _multiplicity(cache_len + t_c - lax.broadcasted_iota(jnp.int32, (rows, cache_len), 1))
    t_n = jnp.bitwise_and(lax.broadcasted_iota(jnp.int32, (rows, LANES), 0), steps - 1)
    new = lax.broadcasted_iota(jnp.int32, (rows, LANES), 1)
    mn = jnp.where(new < steps, _multiplicity(t_n - new), 0.0)
    sc = jnp.where(mc > 0.0, _dot(qb, ckt_ref[0].astype(BF16)), NEG)
    sn = jnp.where(mn > 0.0, _dot_nt(qb, kn), NEG)
    m = jnp.maximum(jnp.max(sc, axis=-1, keepdims=True), jnp.max(sn, axis=-1, keepdims=True))
    pc = mc * jnp.exp(sc - m)
    pn = mn * jnp.exp(sn - m)
    den = jnp.sum(pc, axis=-1, keepdims=True) + jnp.sum(pn, axis=-1, keepdims=True)
    o = _dot_nt(pc.astype(BF16), cvt_ref[0].astype(BF16)) + _dot(pn.astype(BF16), vn)
    o = jnp.where(own, o / den, 0.0)
    o_ref[...] = jnp.sum(o.reshape(ATT_HEADS, steps, ATT_WIDTH), axis=0)


def _sample_attention(q, k_new, v_new, cache_kt, cache_vt, *, steps):
    nb, _, cache_len = cache_kt.shape
    tok = pl.BlockSpec((steps, ATT_WIDTH), lambda n: (n, 0))
    cache = pl.BlockSpec((1, ATT_WIDTH, cache_len), lambda n: (n, 0, 0))
    return pl.pallas_call(
        _sample_attn_kernel,
        out_shape=jax.ShapeDtypeStruct((nb * steps, ATT_WIDTH), F32),
        grid=(nb,),
        in_specs=[tok, tok, tok, cache, cache],
        out_specs=tok,
        compiler_params=_cparams(("arbitrary",)),
        name="sample_attn",
    )(q, k_new, v_new, cache_kt, cache_vt)


def _outproj_kernel(att_ref, gla_ref, x_ref, gate_ref, ng_ref, w_ref, g2_ref, sc2_ref, sh2_ref,
                    h_ref, hn_ref):
    att = jnp.concatenate([att_ref[hp].astype(F32) for hp in range(HEAD_PAIRS)], axis=1)
    rs = lax.rsqrt(jnp.mean(att * att, axis=-1, keepdims=True) + EPS)
    att_y = (att * rs * ng_ref[...]).astype(BF16)
    mix = _dot(att_y, w_ref[:ATT_WIDTH, :]) + _dot(gla_ref[...], w_ref[ATT_WIDTH:, :])
    h = x_ref[...] + gate_ref[...] * mix
    h_ref[...] = h
    hn_ref[...] = _norm_mod(h, g2_ref[...], sc2_ref[...], sh2_ref[...]).astype(BF16)


def _outproj(att, gla_y, x, gate, att_norm_g, w_out, g2, sc2, sh2, *, tm):
    t = x.shape[0]
    mod_rows = gate.shape[0]
    mod_map = (lambda i: (0, 0)) if mod_rows == 1 else (lambda i: (i, 0))
    mod = pl.BlockSpec((1, D_MODEL) if mod_rows == 1 else (tm, D_MODEL), mod_map)
    row = pl.BlockSpec((tm, D_MODEL), lambda i: (i, 0))
    return pl.pallas_call(
        _outproj_kernel,
        out_shape=(jax.ShapeDtypeStruct((t, D_MODEL), F32),
                   jax.ShapeDtypeStruct((t, D_MODEL), BF16)),
        grid=(t // tm,),
        in_specs=[
            pl.BlockSpec((HEAD_PAIRS, tm, LANES), lambda i: (0, i, 0)),
            pl.BlockSpec((tm, GLA_WIDTH), lambda i: (i, 0)),
            row, mod,
            pl.BlockSpec((1, ATT_WIDTH), lambda i: (0, 0)),
            pl.BlockSpec((ATT_WIDTH + GLA_WIDTH, D_MODEL), lambda i: (0, 0)),
            pl.BlockSpec((1, D_MODEL), lambda i: (0, 0)),
            mod, mod,
        ],
        out_specs=(row, row),
        compiler_params=_cparams(("arbitrary",)),
        name="outproj",
    )(att, gla_y, x, gate, att_norm_g, w_out, g2, sc2, sh2)


def _ffn_kernel(h_hbm, hn_ref, gate_ref, fg_ref, wu_ref, wd_ref, y_ref, h_ref, h_sem):
    i = pl.program_id(0)
    f = pl.program_id(1)
    last = pl.num_programs(1) - 1
    tm = hn_ref.shape[0]

    def residual_copy():
        rows = pl.ds(pl.multiple_of(i * tm, tm), tm)
        return pltpu.make_async_copy(h_hbm.at[rows, :], h_ref, h_sem)

    def hidden():
        u = jnp.maximum(_dot(hn_ref[...], wu_ref[...]), 0.0)
        return (u * u).astype(BF16)

    @pl.when(f == 0)
    def _():
        residual_copy().start()
        y_ref[...] = _dot(hidden(), wd_ref[...])

    @pl.when(jnp.logical_and(f > 0, f < last))
    def _():
        y_ref[...] += _dot(hidden(), wd_ref[...])

    @pl.when(f == last)
    def _():
        u2 = hidden()
        residual_copy().wait()
        width = D_MODEL // 4
        ss = jnp.zeros((h_ref.shape[0], 1), F32)
        for c in range(0, D_MODEL, width):
            cols = slice(c, c + width)
            y = y_ref[:, cols] + _dot(u2, wd_ref[:, cols])
            h2 = h_ref[:, cols] + gate_ref[:, cols] * y
            y_ref[:, cols] = h2
            ss = ss + jnp.sum(h2 * h2, axis=-1, keepdims=True)
        y_ref[...] = y_ref[...] * lax.rsqrt(ss / D_MODEL + EPS) * fg_ref[...]


def _ffn(h, hn, gate, final_g, w_up, w_down, *, tm, tf):
    t = h.shape[0]
    assert D_FF // tf >= 2, "first and last hidden chunks are separate code paths"
    row = pl.BlockSpec((tm, D_MODEL), lambda i, f: (i, 0))
    return pl.pallas_call(
        _ffn_kernel,
        out_shape=jax.ShapeDtypeStruct((t, D_MODEL), F32),
        grid=(t // tm, D_FF // tf),
        in_specs=[
            pl.BlockSpec(memory_space=pl.ANY),
            row,
            _mod_spec(gate.shape[0], tm),
            pl.BlockSpec((1, D_MODEL), lambda i, f: (0, 0)),
            pl.BlockSpec((D_MODEL, tf), lambda i, f: (0, f)),
            pl.BlockSpec((tf, D_MODEL), lambda i, f: (f, 0)),
        ],
        out_specs=row,
        scratch_shapes=[pltpu.VMEM((tm, D_MODEL), F32), pltpu.SemaphoreType.DMA(())],
        compiler_params=_cparams(("arbitrary", "arbitrary")),
        name="ffn",
    )(h, hn, gate, final_g, w_up, w_down)


def _split_ada(ada):
    return [ada[:, D_MODEL * i:D_MODEL * (i + 1)] for i in range(6)]


def kernel(x_prompt, x_sample, c_prompt, c_sample, cache_win_k, cache_win_v, state_gla, w_in, gla_gate_w, gla_gate_b, att_norm_g, gla_norm_g, w_out, norm1_g, norm2_g, w_ada, b_ada, w_up, w_down, final_g):
    assert w_in.shape[0] == 1, "single-layer model"
    n_p, t_p, _ = x_prompt.shape
    n_s, t_s, _ = x_sample.shape
    assert n_p == 1
    rows_s = n_s * t_s
    wb_p = min(MAX_WINDOW, t_p)

    w_main = jnp.transpose(w_in[0]).astype(BF16)
    gate_w = gla_gate_w[0].astype(BF16)
    gate_b = gla_gate_b[0][None, :]
    n1g, n2g = norm1_g[0][None, :], norm2_g[0][None, :]
    ang, gng = att_norm_g[0][None, :], gla_norm_g[0][None, :]
    fg = final_g[None, :]

    c_rows = n_p + n_s
    c_pad = -c_rows % 8
    c_all = jnp.concatenate([c_prompt, c_sample, jnp.zeros((c_pad, D_MODEL), F32)], axis=0)
    ada = _ada(c_all, w_ada[0], b_ada[0][None, :])
    sh1_p, sc1_p, g1_p, sh2_p, sc2_p, g2_p = _split_ada(ada[:n_p])
    sh1_s, sc1_s, g1_s, sh2_s, sc2_s, g2_s = _split_ada(
        jnp.repeat(ada[n_p:c_rows], t_s, axis=0))

    xp = x_prompt[0]
    cs_p, sn_p = _rope_tables(jnp.arange(t_p))
    outs = _inproj(xp, n1g, sc1_p, sh1_p, w_main, gate_w, gate_b, cs_p, sn_p,
                   tm=1024, emit_att=True, tail_rows=wb_p)
    att, w_out_b, w_up_b, w_down_b = _prompt_attention(outs[:3], (w_out[0], w_up[0], w_down[0]))
    gla_in, log_a = outs[3], outs[4]
    win_k, win_v = outs[5][0], outs[5][1]
    gla_y, st_p = _gla_prompt(gla_in, log_a, gng, tb=512, chunk=64, sub=16)
    h_p, hn_p = _outproj(att, gla_y, xp, g1_p, ang, w_out_b, n2g, sc2_p, sh2_p, tm=512)
    y_p = _ffn(h_p, hn_p, g2_p, fg, w_up_b, w_down_b, tm=1024, tf=1024)

    xs = x_sample.reshape(rows_s, D_MODEL)
    cs_s, sn_s = _rope_tables(jnp.tile(PAST_LEN + jnp.arange(t_s), n_s))
    gla_in_s, log_a_s = _inproj(xs, n1g, sc1_s, sh1_s, w_main, gate_w, gate_b, cs_s, sn_s,
                                tm=rows_s, emit_att=False)
    qkv_s = _qkv_rows(xs, n1g, sc1_s, sh1_s, w_main, cs_s, sn_s,
                      tm=rows_s, first_row=0, rows=rows_s, col0=0)
    new_k, new_v = qkv_s[1], qkv_s[2]
    wc = cache_win_k.shape[2]
    cache_t = lambda c: jnp.transpose(c[0], (0, 2, 3, 1)).reshape(n_s, ATT_WIDTH, wc)
    att_tok = _sample_attention(qkv_s[0], new_k, new_v, cache_t(cache_win_k), cache_t(cache_win_v),
                                steps=t_s)
    att_s = jnp.transpose(att_tok.reshape(rows_s, HEAD_PAIRS, LANES), (1, 0, 2)).astype(BF16)
    gla_y_s, st_s = _gla_sample(gla_in_s.astype(F32), log_a_s, gng, state_gla[0], steps=t_s,
                                group=4 if n_s % 4 == 0 else 1)
    h_s, hn_s = _outproj(att_s, gla_y_s.astype(BF16), xs, g1_s, ang, w_out_b, n2g, sc2_s, sh2_s,
                         tm=rows_s)
    y_s = _ffn(h_s, hn_s, g2_s, fg, w_up_b, w_down_b, tm=rows_s, tf=2048)

    return (
        y_p[None],
        y_s.reshape(n_s, t_s, D_MODEL),
        win_k.reshape(1, n_p, wb_p, ATT_HEADS, ATT_HEAD_DIM),
        win_v.reshape(1, n_p, wb_p, ATT_HEADS, ATT_HEAD_DIM),
        st_p[None, None],
        new_k.reshape(1, n_s, t_s, ATT_HEADS, ATT_HEAD_DIM),
        new_v.reshape(1, n_s, t_s, ATT_HEADS, ATT_HEAD_DIM),
        st_s[None],
    )
```

```python
import functools

import jax
import jax.numpy as jnp
from jax import lax
from jax.experimental import pallas as pl
from jax.experimental.pallas import tpu as pltpu

F32 = jnp.float32
BF16 = jnp.bfloat16

D_MODEL = 2048
ATT_WIDTH = 1024
ATT_HEADS = 16
ATT_HEAD_DIM = 64
ROT_DIM = 16
ROPE_THETA = 500000.0
DILATIONS = (1, 4, 16)
BAND = 128
MAX_WINDOW = 2048
GLA_HEADS = 4
GLA_DK = 128
GLA_DV = 256
GLA_WIDTH = 1024
GLA_GATE_RANK = 16
GLA_TAU = 16.0
D_FF = 8192
EPS = 1e-6
PAST_LEN = 16384
NEG = -1e30

LANES = 128
HEAD_PAIRS = ATT_WIDTH // LANES
VMEM_LIMIT = 60000 * 1024


def _cparams(sem):
    return pltpu.CompilerParams(dimension_semantics=sem, vmem_limit_bytes=VMEM_LIMIT)


def _dot(a, b):
    return jnp.dot(a, b, preferred_element_type=F32)


def _dot_nt(a, b):
    return lax.dot_general(a, b, (((1,), (1,)), ((), ())), preferred_element_type=F32)


def _dot_tn(a, b):
    return lax.dot_general(a, b, (((0,), (0,)), ((), ())), preferred_element_type=F32)


def _silu(x):
    return x / (1.0 + jnp.exp(-x))


def _ada_kernel(c_ref, w_ref, b_ref, o_ref):
    s = _silu(c_ref[...]).astype(BF16)
    o_ref[...] = _dot(s, w_ref[...].astype(BF16)) + b_ref[...]


def _ada(c, w_ada, b_ada):
    rows = c.shape[0]
    n = w_ada.shape[1]
    tn = 1024
    return pl.pallas_call(
        _ada_kernel,
        out_shape=jax.ShapeDtypeStruct((rows, n), F32),
        grid=(n // tn,),
        in_specs=[
            pl.BlockSpec((rows, D_MODEL), lambda j: (0, 0)),
            pl.BlockSpec((D_MODEL, tn), lambda j: (0, j)),
            pl.BlockSpec((1, tn), lambda j: (0, j)),
        ],
        out_specs=pl.BlockSpec((rows, tn), lambda j: (0, j)),
        compiler_params=_cparams(("arbitrary",)),
        name="ada",
    )(c, w_ada, b_ada)


def _rope(slab, cs, sn):
    d = lax.broadcasted_iota(jnp.int32, (1, LANES), 1) % ATT_HEAD_DIM
    half = ROT_DIM // 2
    from_hi = jnp.where(d < half, -1.0, 0.0)
    from_lo = jnp.where((d >= half) & (d < ROT_DIM), 1.0, 0.0)
    return (slab * cs + pltpu.roll(slab, LANES - half, 1) * (sn * from_hi)
            + pltpu.roll(slab, half, 1) * (sn * from_lo))


def _norm_mod(x, g, sc, sh):
    rs = lax.rsqrt(jnp.mean(x * x, axis=-1, keepdims=True) + EPS)
    return x * rs * g * (1.0 + sc) + sh


def _mod_rows(ref, rows):
    m = ref[...]
    if m.shape[0] in (1, rows):
        return m
    run = rows // m.shape[0]
    return jnp.broadcast_to(m[:, None, :], (m.shape[0], run, m.shape[1])).reshape(rows, m.shape[1])


def _inproj_kernel(*refs, emit_att, first_tail):
    (x_ref, g_ref, sc_ref, sh_ref, w_ref, wag_ref, gw_ref, gb_ref, cs_ref, sn_ref), refs = (
        refs[:10], refs[10:])
    i = pl.program_id(0)
    j = pl.program_id(1)
    if emit_att:
        (out1, out4, out16), refs = refs[:3], refs[3:]
        gla_ref, la_ref, tail_ref, xn_ref, s0_ref, s1_ref, x_buf, x_sem = refs
        tm = x_buf.shape[0]
        x_hbm, x_ref = x_ref, x_buf

        def x_copy(tile):
            rows = pl.ds(pl.multiple_of(tile * tm, tm), tm)
            return pltpu.make_async_copy(x_hbm.at[rows, :], x_buf, x_sem)

        @pl.when(jnp.logical_and(i == 0, j == 0))
        def _():
            x_copy(0).start()

        @pl.when(jnp.logical_and(j == 1, i + 1 < pl.num_programs(0)))
        def _():
            x_copy(i + 1).start()
    else:
        gla_ref, la_ref, xn_ref = refs[:3]
    col = j if emit_att else j + 3

    @pl.when(j == 0)
    def _():
        if emit_att:
            x_copy(i).wait()
        rows = x_ref.shape[0]
        xnb = _norm_mod(x_ref[...], g_ref[...], _mod_rows(sc_ref, rows),
                        _mod_rows(sh_ref, rows)).astype(BF16)
        xn_ref[...] = xnb
        ag = _dot_nt(xnb, wag_ref[...])
        z = _dot(ag.astype(BF16), gw_ref[...]) + gb_ref[...]
        log_sig = jnp.minimum(z, 0.0) - jnp.log(1.0 + jnp.exp(-jnp.abs(z)))
        la_ref[...] = log_sig / GLA_TAU

    if emit_att:
        n4, n16 = tm // 4, tm // 16
        group = 2 * LANES
        in_tail = i >= first_tail

        def emit(slab_fn, keep_f32=False):
            for hp in range(HEAD_PAIRS):
                half = hp % 2
                if half == 0:
                    acc = _dot_nt(xn_ref[...], w_ref[LANES * hp:LANES * hp + group, :])
                slab = slab_fn(acc[:, LANES * half:LANES * (half + 1)])
                out1[0, hp] = slab.astype(BF16)
                if keep_f32:
                    tail_ref[0, :, LANES * hp:LANES * (hp + 1)] = slab
                s0_ref[half] = slab
                for r in range(4):
                    part = s0_ref[half, pl.ds(r, n4, stride=4), :]
                    out4[0, hp, :, LANES * r:LANES * (r + 1)] = part.astype(BF16)
                    s1_ref[half, n4 * r:n4 * (r + 1), :] = part
                for r in range(16):
                    part = s1_ref[half, pl.ds(n4 * (r % 4) + r // 4, n16, stride=4), :]
                    out16[0, hp, :, LANES * r:LANES * (r + 1)] = part.astype(BF16)

        @pl.when(col == 0)
        def _():
            emit(lambda s: _rope(s, cs_ref[...], sn_ref[...]) * (ATT_HEAD_DIM ** -0.5))

        for c, slab_fn in ((1, lambda s: _rope(s, cs_ref[...], sn_ref[...])), (2, lambda s: s)):
            pl.when(jnp.logical_and(col == c, jnp.logical_not(in_tail)))(
                functools.partial(emit, slab_fn))
            pl.when(jnp.logical_and(col == c, in_tail))(
                functools.partial(emit, slab_fn, keep_f32=True))

    @pl.when(col == 3)
    def _():
        acc = _dot_nt(xn_ref[...], w_ref[...])
        half = GLA_HEADS * GLA_DK
        gla_ref[:, :half] = (acc[:, :half] * (GLA_DK ** -0.5)).astype(BF16)
        gla_ref[:, half:] = acc[:, half:].astype(BF16)

    @pl.when(col >= 4)
    def _():
        gla_ref[...] = _dot_nt(xn_ref[...], w_ref[...]).astype(BF16)


def _mod_spec(mod_rows, tm, t):
    if mod_rows == 1:
        return pl.BlockSpec((1, D_MODEL), lambda i, j: (0, 0))
    return pl.BlockSpec((mod_rows * tm // t, D_MODEL), lambda i, j: (i, 0))


def _inproj(x, g, sc, sh, w_all, gate_w, gate_b, cs, sn, *, tm, emit_att, tail_rows=0):
    t = x.shape[0]
    col0 = 0 if emit_att else 3
    first_tail = (t - tail_rows) // tm
    mod = _mod_spec(sc.shape[0], tm, t)
    const = lambda block: pl.BlockSpec(block, lambda i, j: (0, 0))
    out_shape, out_specs, scratch = [], [], [pltpu.VMEM((tm, D_MODEL), BF16)]
    if emit_att:
        for d in DILATIONS:
            out_shape.append(jax.ShapeDtypeStruct((3, HEAD_PAIRS, t // d, d * LANES), BF16))
            out_specs.append(pl.BlockSpec((1, HEAD_PAIRS, tm // d, d * LANES),
                                          lambda i, j: (jnp.minimum(j, 2), 0, i, 0)))
        scratch += [pltpu.VMEM((2, tm, LANES), F32) for _ in range(2)]
        scratch += [pltpu.VMEM((tm, D_MODEL), F32), pltpu.SemaphoreType.DMA(())]
    out_shape += [jax.ShapeDtypeStruct((t, 3 * 1024), BF16),
                  jax.ShapeDtypeStruct((t, GLA_HEADS * GLA_DK), F32)]
    out_specs += [pl.BlockSpec((tm, 1024), lambda i, j: (i, jnp.maximum(j + col0 - 3, 0))),
                  pl.BlockSpec((tm, GLA_HEADS * GLA_DK), lambda i, j: (i, 0))]
    if emit_att:
        assert tail_rows % tm == 0 and 0 < tail_rows <= t
        out_shape.append(jax.ShapeDtypeStruct((2, tail_rows, ATT_WIDTH), F32))
        out_specs.append(pl.BlockSpec(
            (1, tm, ATT_WIDTH),
            lambda i, j: (jnp.where(i < first_tail, 0, jnp.clip(j - 1, 0, 1)),
                          jnp.maximum(i - first_tail, 0), 0)))
    gate_rows = (6 * 1024) // GLA_GATE_RANK
    return pl.pallas_call(
        functools.partial(_inproj_kernel, emit_att=emit_att, first_tail=first_tail),
        out_shape=tuple(out_shape),
        grid=(t // tm, 6 - col0),
        in_specs=[
            (pl.BlockSpec(memory_space=pl.ANY) if emit_att
             else pl.BlockSpec((tm, D_MODEL), lambda i, j: (i, 0))),
            const((1, D_MODEL)),
            mod, mod,
            pl.BlockSpec((1024, D_MODEL), lambda i, j: (j + col0, 0)),
            pl.BlockSpec((GLA_GATE_RANK, D_MODEL), lambda i, j: (gate_rows, 0)),
            const((GLA_GATE_RANK, GLA_HEADS * GLA_DK)),
            const((1, GLA_HEADS * GLA_DK)),
            pl.BlockSpec((tm, LANES), lambda i, j: (i, 0)),
            pl.BlockSpec((tm, LANES), lambda i, j: (i, 0)),
        ],
        out_specs=tuple(out_specs),
        scratch_shapes=scratch,
        compiler_params=_cparams(("arbitrary", "arbitrary")),
        name="inproj_att" if emit_att else "inproj_gla",
    )(x, g, sc, sh, w_all, w_all, gate_w, gate_b, cs, sn)


def _qkv_rows_kernel(x_ref, g_ref, sc_ref, sh_ref, w_ref, cs_ref, sn_ref, o_ref, xn_ref, *, col0):
    j = pl.program_id(1)

    @pl.when(j == 0)
    def _():
        xn_ref[...] = _norm_mod(x_ref[...], g_ref[...], _mod_rows(sc_ref, x_ref.shape[0]),
                                _mod_rows(sh_ref, x_ref.shape[0])).astype(BF16)

    acc = _dot_nt(xn_ref[...], w_ref[...])
    col = j + col0

    @pl.when(col <= 1)
    def _():
        scale = jnp.where(col == 0, ATT_HEAD_DIM ** -0.5, 1.0)
        for hp in range(HEAD_PAIRS):
            lanes = slice(LANES * hp, LANES * (hp + 1))
            o_ref[0, :, lanes] = _rope(acc[:, lanes], cs_ref[...], sn_ref[...]) * scale

    @pl.when(col == 2)
    def _():
        o_ref[0] = acc


def _qkv_rows(x, g, sc, sh, w_main, cs, sn, *, tm, first_row, rows, col0):
    first = first_row // tm
    assert sc.shape[0] == 1 or first == 0
    ncols = 3 - col0
    mod = _mod_spec(sc.shape[0], tm, rows)
    return pl.pallas_call(
        functools.partial(_qkv_rows_kernel, col0=col0),
        out_shape=jax.ShapeDtypeStruct((ncols, rows, ATT_WIDTH), F32),
        grid=(rows // tm, ncols),
        in_specs=[
            pl.BlockSpec((tm, D_MODEL), lambda i, j: (i + first, 0)),
            pl.BlockSpec((1, D_MODEL), lambda i, j: (0, 0)),
            mod, mod,
            pl.BlockSpec((1024, D_MODEL), lambda i, j: (j + col0, 0)),
            pl.BlockSpec((tm, LANES), lambda i, j: (i + first, 0)),
            pl.BlockSpec((tm, LANES), lambda i, j: (i + first, 0)),
        ],
        out_specs=pl.BlockSpec((1, tm, ATT_WIDTH), lambda i, j: (j, i, 0)),
        scratch_shapes=[pltpu.VMEM((tm, D_MODEL), BF16)],
        compiler_params=_cparams(("arbitrary", "arbitrary")),
        name="qkv_rows",
    )(x, g, sc, sh, w_main, cs, sn)


def _rope_tables(pos):
    half = ROT_DIM // 2
    inv_freq = ROPE_THETA ** (-jnp.arange(half, dtype=F32) / half)
    per_head = jnp.concatenate([inv_freq, inv_freq, jnp.zeros((ATT_HEAD_DIM - ROT_DIM,), F32)])
    ang = pos.astype(F32)[:, None] * jnp.tile(per_head, LANES // ATT_HEAD_DIM)[None, :]
    return jnp.cos(ang), jnp.sin(ang)


def _attn_kernel(*refs, n_cast):
    (q1, k1c, v1c, k1p, v1p, q4, k4c, v4c, k4p, v4p, q16, k16c, v16c, k16p, v16p), refs = (
        refs[:15], refs[15:])
    cast_in, o_ref, cast_out = refs[:n_cast], refs[n_cast], refs[n_cast + 1:2 * n_cast + 1]
    a1, m1, l1, a4, m4, l4, t16, s_scr, p_scr, bias_ref, o_scr = refs[2 * n_cast + 1:]
    for src, dst in zip(cast_in, cast_out):
        dst[...] = src[...].astype(dst.dtype)

    first_block = pl.program_id(1) == 0
    lane = lax.broadcasted_iota(jnp.int32, (1, LANES), 1)
    first = lane < ATT_HEAD_DIM
    head0 = jnp.where(first, 1.0, 0.0).astype(BF16)
    head1 = jnp.where(first, 0.0, 1.0).astype(BF16)
    ones = jnp.ones((2 * BAND, LANES), BF16)

    qi = lax.broadcasted_iota(jnp.int32, (2 * BAND, 2 * BAND), 0) % BAND
    kk = lax.broadcasted_iota(jnp.int32, (2 * BAND, 2 * BAND), 1)
    full = jnp.where(jnp.where(kk < BAND, kk - qi, qi - (kk - BAND)) >= 0, 0.0, NEG)
    bias_ref[0] = full
    bias_ref[1] = jnp.where(kk < BAND, full + jnp.where(first_block, NEG, 0.0), full)

    n1 = q1.shape[1] // BAND
    rows4 = q4.shape[1]
    n4 = rows4 // BAND

    def window(cur, prev, i, cols):
        if i == 0:
            return jnp.concatenate([prev[0, :, cols], cur[0, :BAND, cols]], axis=0)
        return cur[0, BAND * (i - 1):BAND * (i + 1), cols]

    jobs = []
    everything = slice(None)
    for i in range(n1):
        jobs.append(dict(q=(q1, i, everything), k=(k1c, k1p), v=(v1c, v1p), bias=int(i == 0),
                         dst=(a1, m1, l1, BAND * i)))
    for r in range(4):
        cols = slice(LANES * r, LANES * (r + 1))
        for i in range(n4):
            jobs.append(dict(q=(q4, i, cols), k=(k4c, k4p), v=(v4c, v4p), bias=int(i == 0),
                             dst=(a4, m4, l4, rows4 * r + BAND * i)))
    for r in range(16):
        cols = slice(LANES * r, LANES * (r + 1))
        jobs.append(dict(q=(q16, 0, cols), k=(k16c, k16p), v=(v16c, v16p), bias=1, merge=r))

    def scores(job, slot):
        ref, i, cols = job["q"]
        q = ref[0, BAND * i:BAND * (i + 1), cols]
        q2 = jnp.concatenate([q * head0, q * head1], axis=0)
        s_scr[slot] = _dot_nt(q2, window(*job["k"], i, cols))

    def softmax(job, slot):
        s = s_scr[slot] + bias_ref[job["bias"]]
        m = jnp.max(s, axis=-1, keepdims=True)
        p_scr[slot] = jnp.exp(s - m).astype(BF16)
        m = jnp.where(first, m[:BAND], m[BAND:])
        if "merge" in job:
            t16[slot] = m
        else:
            job["dst"][1][pl.ds(job["dst"][3], BAND), :] = m

    def values(job, slot):
        _, i, cols = job["q"]
        vw = jnp.concatenate([window(*job["v"], i, cols), ones], axis=1)
        o2 = _dot(p_scr[slot], vw)
        acc = jnp.where(first, o2[:BAND, :LANES], o2[BAND:, :LANES])
        den = jnp.where(first, o2[:BAND, LANES:], o2[BAND:, LANES:])
        if "merge" not in job:
            a_ref, _, l_ref, row = job["dst"]
            a_ref[pl.ds(row, BAND), :] = acc
            l_ref[pl.ds(row, BAND), :] = den
            return
        r = job["merge"]
        sl1 = pl.ds(r, BAND, stride=16)
        sl4 = pl.ds((r % 4) * rows4 + r // 4, BAND, stride=4)
        ma, mb, mc = m1[sl1, :], m4[sl4, :], t16[slot]
        m = jnp.maximum(jnp.maximum(ma, mb), mc)
        wa, wb, wc = jnp.exp(ma - m), jnp.exp(mb - m), jnp.exp(mc - m)
        num = wa * a1[sl1, :] + wb * a4[sl4, :] + wc * acc
        dsum = wa * l1[sl1, :] + wb * l4[sl4, :] + wc * den
        o_scr[sl1, :] = num / dsum

    slots = s_scr.shape[0]
    for step in range(len(jobs) + 2):
        if step < len(jobs):
            scores(jobs[step], step % slots)
        if 1 <= step <= len(jobs):
            softmax(jobs[step - 1], (step - 1) % slots)
        if step >= 2:
            values(jobs[step - 2], (step - 2) % slots)
    o_ref[0] = o_scr[...].astype(o_ref.dtype)


def _prompt_attention(views, weights):
    _, hp, t, _ = views[0].shape
    sb = 16 * BAND
    ns = t // sb
    in_specs, args = [], []
    for arr, d in zip(views, DILATIONS):
        rows, width = sb // d, d * LANES
        nb = rows // BAND

        def cur(c, rows=rows, width=width):
            return pl.BlockSpec((None, 1, rows, width), lambda h, s: (c, h, s, 0))

        def prev(c, width=width, nb=nb):
            return pl.BlockSpec((None, 1, BAND, width),
                                lambda h, s: (c, h, jnp.maximum(s * nb - 1, 0), 0))

        in_specs += [cur(0), cur(1), cur(2), prev(1), prev(2)]
        args += [arr] * 5
    cast_specs = []
    for w in weights:
        slab = w.shape[0] // (hp * ns)
        assert slab * hp * ns == w.shape[0] and slab % 16 == 0
        cast_specs.append(pl.BlockSpec((slab, w.shape[1]), lambda h, s: (h * ns + s, 0)))
    tile = (2 * BAND, 2 * BAND)
    return pl.pallas_call(
        functools.partial(_attn_kernel, n_cast=len(weights)),
        out_shape=tuple([jax.ShapeDtypeStruct((hp, t, LANES), BF16)]
                        + [jax.ShapeDtypeStruct(w.shape, BF16) for w in weights]),
        grid=(hp, ns),
        in_specs=in_specs + cast_specs,
        out_specs=tuple([pl.BlockSpec((1, sb, LANES), lambda h, s: (h, s, 0))] + cast_specs),
        scratch_shapes=(
            [pltpu.VMEM((sb, LANES), F32) for _ in range(6)]
            + [pltpu.VMEM((3, BAND, LANES), F32),
               pltpu.VMEM((3,) + tile, F32),
               pltpu.VMEM((3,) + tile, BF16),
               pltpu.VMEM((2,) + tile, F32),
               pltpu.VMEM((sb, LANES), F32)]),
        compiler_params=_cparams(("arbitrary", "arbitrary")),
        name="prompt_attn",
    )(*args, *weights)


def _gla_chunk(q, k, v, la, st_ref, h, tri, sub):
    c = q.shape[0]
    la_hi = la.astype(BF16)
    la_lo = (la - la_hi.astype(F32)).astype(BF16)
    b = _dot(tri, la_hi) + _dot(tri, la_lo)
    st = st_ref[h]
    o = _dot_nt((q * jnp.exp(b)).astype(BF16), st.astype(BF16))
    srow = lax.broadcasted_iota(jnp.int32, (c, GLA_DK), 0)
    att_rows = []
    for i in range(c // sub):
        lo, hi = i * sub, (i + 1) * sub
        ref = b[lo:lo + 1, :]
        qh = (q[lo:hi] * jnp.exp(b[lo:hi] - ref)).astype(BF16)
        kh = (k * jnp.exp(jnp.where(srow < hi, ref - b, 0.0))).astype(BF16)
        a = _dot_nt(qh, kh)
        ti = lax.broadcasted_iota(jnp.int32, (sub, c), 0) + lo
        si = lax.broadcasted_iota(jnp.int32, (sub, c), 1)
        att_rows.append(jnp.where(si <= ti, a, 0.0))
    att = att_rows[0] if len(att_rows) == 1 else jnp.concatenate(att_rows, axis=0)
    o = o + _dot(att.astype(BF16), v)
    b_last = b[c - 1:c, :]
    kbar = (k * jnp.exp(b_last - b)).astype(BF16)
    st_ref[h] = st * jnp.exp(b_last) + _dot_tn(v, kbar)
    return o


def _gla_out(o, g, gg):
    rs = lax.rsqrt(jnp.mean(o * o, axis=-1, keepdims=True) + EPS)
    return o * rs * g * _silu(gg)


def _tri(c):
    r = lax.broadcasted_iota(jnp.int32, (c, c), 0)
    s = lax.broadcasted_iota(jnp.int32, (c, c), 1)
    return jnp.where(s <= r, 1.0, 0.0).astype(BF16)


def _gla_prompt_kernel(qk_ref, v_ref, gg_ref, la_ref, g_ref, y_ref, s_ref, st_ref,
                       qd_ref, kl_ref, qs_ref, ks_ref, dec_ref, att_ref, o_ref, *, chunk, sub):
    i = pl.program_id(0)

    @pl.when(i == 0)
    def _():
        st_ref[...] = jnp.zeros_like(st_ref)

    tb = qk_ref.shape[0]
    nchunks, nsub = tb // chunk, chunk // sub
    half = GLA_HEADS * GLA_DK

    r = lax.broadcasted_iota(jnp.int32, (tb, tb), 0)
    c = lax.broadcasted_iota(jnp.int32, (tb, tb), 1)
    start = r - jnp.bitwise_and(r, chunk - 1)
    tri = jnp.where(c <= r, jnp.where(c >= start, 1.0, 0.0), 0.0).astype(BF16)
    la = la_ref[...]
    la_hi = la.astype(BF16)
    la_lo = (la - la_hi.astype(F32)).astype(BF16)
    b = _dot(tri, la_hi) + _dot(tri, la_lo)
    q = qk_ref[:, :half].astype(F32)
    k = qk_ref[:, half:].astype(F32)
    qd_ref[...] = (q * jnp.exp(b)).astype(BF16)
    srow = lax.broadcasted_iota(jnp.int32, (chunk, half), 0)
    for ci in range(nchunks):
        lo = ci * chunk
        bc, kc_ = b[lo:lo + chunk], k[lo:lo + chunk]
        b_last = bc[chunk - 1:chunk]
        dec_ref[ci:ci + 1, :] = jnp.exp(b_last)
        kl_ref[lo:lo + chunk, :] = (kc_ * jnp.exp(b_last - bc)).astype(BF16)
        for si in range(nsub):
            s_lo, s_hi = si * sub, (si + 1) * sub
            ref = bc[s_lo:s_lo + 1]
            qs_ref[lo + s_lo:lo + s_hi, :] = (
                q[lo + s_lo:lo + s_hi] * jnp.exp(bc[s_lo:s_hi] - ref)).astype(BF16)
            ks_ref[si, lo:lo + chunk, :] = (
                kc_ * jnp.exp(jnp.where(srow < s_hi, ref - bc, 0.0))).astype(BF16)

    ti = lax.broadcasted_iota(jnp.int32, (chunk, chunk), 0)
    si_ = lax.broadcasted_iota(jnp.int32, (chunk, chunk), 1)
    causal = si_ <= ti
    jobs = [(ci, h) for ci in range(nchunks) for h in range(GLA_HEADS)]

    def intra(job, slot):
        ci, h = job
        lo, kc = ci * chunk, slice(GLA_DK * h, GLA_DK * (h + 1))
        parts = [_dot_nt(qs_ref[lo + s * sub:lo + (s + 1) * sub, kc], ks_ref[s, lo:lo + chunk, kc])
                 for s in range(nsub)]
        att_ref[slot] = jnp.where(causal, jnp.concatenate(parts, axis=0), 0.0).astype(BF16)

    def state(job, slot):
        ci, h = job
        rows = slice(ci * chunk, (ci + 1) * chunk)
        kc, vc = slice(GLA_DK * h, GLA_DK * (h + 1)), slice(GLA_DV * h, GLA_DV * (h + 1))
        st = st_ref[h]
        v = v_ref[rows, vc]
        o_ref[slot] = _dot_nt(qd_ref[rows, kc], st.astype(BF16)) + _dot(att_ref[slot], v)
        st_ref[h] = st * dec_ref[ci:ci + 1, kc] + _dot_tn(v, kl_ref[rows, kc])

    def finish(job, slot):
        ci, h = job
        rows, vc = slice(ci * chunk, (ci + 1) * chunk), slice(GLA_DV * h, GLA_DV * (h + 1))
        y_ref[rows, vc] = _gla_out(o_ref[slot], g_ref[:, vc],
                                   gg_ref[rows, vc].astype(F32)).astype(y_ref.dtype)

    slots = att_ref.shape[0]
    for step in range(len(jobs) + 2):
        if step < len(jobs):
            intra(jobs[step], step % slots)
        if 1 <= step <= len(jobs):
            state(jobs[step - 1], (step - 1) % slots)
        if step >= 2:
            finish(jobs[step - 2], (step - 2) % slots)

    @pl.when(i == pl.num_programs(0) - 1)
    def _():
        for h in range(GLA_HEADS):
            s_ref[h] = st_ref[h].T


def _gla_prompt(gla_in, log_a, norm_g, *, tb, chunk, sub):
    t = gla_in.shape[0]
    return pl.pallas_call(
        functools.partial(_gla_prompt_kernel, chunk=chunk, sub=sub),
        out_shape=(
            jax.ShapeDtypeStruct((t, GLA_WIDTH), BF16),
            jax.ShapeDtypeStruct((GLA_HEADS, GLA_DK, GLA_DV), F32),
        ),
        grid=(t // tb,),
        in_specs=[
            pl.BlockSpec((tb, 1024), lambda i: (i, 0)),
            pl.BlockSpec((tb, 1024), lambda i: (i, 1)),
            pl.BlockSpec((tb, 1024), lambda i: (i, 2)),
            pl.BlockSpec((tb, GLA_HEADS * GLA_DK), lambda i: (i, 0)),
            pl.BlockSpec((1, GLA_WIDTH), lambda i: (0, 0)),
        ],
        out_specs=(
            pl.BlockSpec((tb, GLA_WIDTH), lambda i: (i, 0)),
            pl.BlockSpec((GLA_HEADS, GLA_DK, GLA_DV), lambda i: (0, 0, 0)),
        ),
        scratch_shapes=[
            pltpu.VMEM((GLA_HEADS, GLA_DV, GLA_DK), F32),
            pltpu.VMEM((tb, GLA_HEADS * GLA_DK), BF16),
            pltpu.VMEM((tb, GLA_HEADS * GLA_DK), BF16),
            pltpu.VMEM((tb, GLA_HEADS * GLA_DK), BF16),
            pltpu.VMEM((chunk // sub, tb, GLA_HEADS * GLA_DK), BF16),
            pltpu.VMEM((tb // chunk, GLA_HEADS * GLA_DK), F32),
            pltpu.VMEM((3, chunk, chunk), BF16),
            pltpu.VMEM((3, chunk, GLA_DV), F32),
        ],
        compiler_params=_cparams(("arbitrary",)),
        name="gla_prompt",
    )(gla_in, gla_in, gla_in, log_a, norm_g)


def _gla_sample_kernel(qk_ref, v_ref, gg_ref, la_ref, g_ref, s0_ref, y_ref, s_ref, st_ref, *, steps):
    group = s0_ref.shape[0]
    c = 2 * steps
    tri = _tri(c)
    half = GLA_HEADS * GLA_DK
    pad = lambda a: jnp.concatenate([a, jnp.zeros_like(a)], axis=0)
    for n in range(group):
        rows = slice(n * steps, (n + 1) * steps)
        for h in range(GLA_HEADS):
            idx = n * GLA_HEADS + h
            kc = slice(GLA_DK * h, GLA_DK * (h + 1))
            vc = slice(GLA_DV * h, GLA_DV * (h + 1))
            st_ref[idx] = s0_ref[n, h].T
            q = pad(qk_ref[rows, kc])
            k = pad(qk_ref[rows, half + GLA_DK * h:half + GLA_DK * (h + 1)])
            v = pad(v_ref[rows, vc]).astype(BF16)
            o = _gla_chunk(q, k, v, pad(la_ref[rows, kc]), st_ref, idx, tri, c)
            y_ref[rows, vc] = _gla_out(o[:steps], g_ref[:, vc], gg_ref[rows, vc])
            s_ref[n, h] = st_ref[idx].T


def _gla_sample(gla_in, log_a, norm_g, state, *, steps, group):
    t = gla_in.shape[0]
    rows = steps * group
    return pl.pallas_call(
        functools.partial(_gla_sample_kernel, steps=steps),
        out_shape=(
            jax.ShapeDtypeStruct((t, GLA_WIDTH), F32),
            jax.ShapeDtypeStruct(state.shape, F32),
        ),
        grid=(t // rows,),
        in_specs=[
            pl.BlockSpec((rows, 1024), lambda n: (n, 0)),
            pl.BlockSpec((rows, 1024), lambda n: (n, 1)),
            pl.BlockSpec((rows, 1024), lambda n: (n, 2)),
            pl.BlockSpec((rows, GLA_HEADS * GLA_DK), lambda n: (n, 0)),
            pl.BlockSpec((1, GLA_WIDTH), lambda n: (0, 0)),
            pl.BlockSpec((group, GLA_HEADS, GLA_DK, GLA_DV), lambda n: (n, 0, 0, 0)),
        ],
        out_specs=(
            pl.BlockSpec((rows, GLA_WIDTH), lambda n: (n, 0)),
            pl.BlockSpec((group, GLA_HEADS, GLA_DK, GLA_DV), lambda n: (n, 0, 0, 0)),
        ),
        scratch_shapes=[pltpu.VMEM((group * GLA_HEADS, GLA_DV, GLA_DK), F32)],
        compiler_params=_cparams(("arbitrary",)),
        name="gla_sample",
    )(gla_in, gla_in, gla_in, log_a, norm_g, state)


def _multiplicity(delta):
    total = jnp.zeros(delta.shape, F32)
    for d in DILATIONS:
        assert d & (d - 1) == 0
        hit = jnp.where(jnp.bitwise_and(delta, d - 1) == 0, 1.0, 0.0)
        total = total + jnp.where(delta <= BAND * d, hit, 0.0)
    return jnp.where(delta >= 0, total, 0.0)


def _sample_attn_kernel(q_ref, kn_ref, vn_ref, ckt_ref, cvt_ref, o_ref):
    steps = q_ref.shape[0]
    rows = ATT_HEADS * steps
    cache_len = ckt_ref.shape[2]
    qt = jnp.concatenate([q_ref[...]] * ATT_HEADS, axis=0)
    rh = lax.broadcasted_iota(jnp.int32, (rows, ATT_WIDTH), 0) // steps
    ch = lax.broadcasted_iota(jnp.int32, (rows, ATT_WIDTH), 1) // ATT_HEAD_DIM
    own = rh == ch
    qb = jnp.where(own, qt, 0.0).astype(BF16)
    pad = jnp.zeros((LANES - steps, ATT_WIDTH), F32)
    kn = jnp.concatenate([kn_ref[...], pad], axis=0).astype(BF16)
    vn = jnp.concatenate([vn_ref[...], pad], axis=0).astype(BF16)
    assert steps & (steps - 1) == 0
    t_c = jnp.bitwise_and(lax.broadcasted_iota(jnp.int32, (rows, cache_len), 0), steps - 1)
    mc = _multiplicity(cache_len + t_c - lax.broadcasted_iota(jnp.int32, (rows, cache_len), 1))
    t_n = jnp.bitwise_and(lax.broadcasted_iota(jnp.int32, (rows, LANES), 0), steps - 1)
    new = lax.broadcasted_iota(jnp.int32, (rows, LANES), 1)
    mn = jnp.where(new < steps, _multiplicity(t_n - new), 0.0)
    sc = jnp.where(mc > 0.0, _dot(qb, ckt_ref[0].astype(BF16)), NEG)
    sn = jnp.where(mn > 0.0, _dot_nt(qb, kn), NEG)
    m = jnp.maximum(jnp.max(sc, axis=-1, keepdims=True), jnp.max(sn, axis=-1, keepdims=True))
    pc = mc * jnp.exp(sc - m)
    pn = mn * jnp.exp(sn - m)
    den = jnp.sum(pc, axis=-1, keepdims=True) + jnp.sum(pn, axis=-1, keepdims=True)
    o = _dot_nt(pc.astype(BF16), cvt_ref[0].astype(BF16)) + _dot(pn.astype(BF16), vn)
    o = jnp.where(own, o / den, 0.0)
    o_ref[...] = jnp.sum(o.reshape(ATT_HEADS, steps, ATT_WIDTH), axis=0)


def _sample_attention(q, k_new, v_new, cache_kt, cache_vt, *, steps):
    nb, _, cache_len = cache_kt.shape
    tok = pl.BlockSpec((steps, ATT_WIDTH), lambda n: (n, 0))
    cache = pl.BlockSpec((1, ATT_WIDTH, cache_len), lambda n: (n, 0, 0))
    return pl.pallas_call(
        _sample_attn_kernel,
        out_shape=jax.ShapeDtypeStruct((nb * steps, ATT_WIDTH), F32),
        grid=(nb,),
        in_specs=[tok, tok, tok, cache, cache],
        out_specs=tok,
        compiler_params=_cparams(("arbitrary",)),
        name="sample_attn",
    )(q, k_new, v_new, cache_kt, cache_vt)


def _outproj_kernel(att_ref, gla_ref, x_ref, gate_ref, ng_ref, w_ref, g2_ref, sc2_ref, sh2_ref,
                    h_ref, hn_ref):
    att = jnp.concatenate([att_ref[hp].astype(F32) for hp in range(HEAD_PAIRS)], axis=1)
    rs = lax.rsqrt(jnp.mean(att * att, axis=-1, keepdims=True) + EPS)
    att_y = (att * rs * ng_ref[...]).astype(BF16)
    mix = _dot(att_y, w_ref[:ATT_WIDTH, :]) + _dot(gla_ref[...], w_ref[ATT_WIDTH:, :])
    rows = x_ref.shape[0]
    h = x_ref[...] + _mod_rows(gate_ref, rows) * mix
    h_ref[...] = h
    hn_ref[...] = _norm_mod(h, g2_ref[...], _mod_rows(sc2_ref, rows),
                            _mod_rows(sh2_ref, rows)).astype(BF16)


def _outproj(att, gla_y, x, gate, att_norm_g, w_out, g2, sc2, sh2, *, tm):
    t = x.shape[0]
    mod_rows = gate.shape[0]
    mod_map = (lambda i: (0, 0)) if mod_rows == 1 else (lambda i: (i, 0))
    mod = pl.BlockSpec((1 if mod_rows == 1 else mod_rows * tm // t, D_MODEL), mod_map)
    row = pl.BlockSpec((tm, D_MODEL), lambda i: (i, 0))
    return pl.pallas_call(
        _outproj_kernel,
        out_shape=(jax.ShapeDtypeStruct((t, D_MODEL), F32),
                   jax.ShapeDtypeStruct((t, D_MODEL), BF16)),
        grid=(t // tm,),
        in_specs=[
            pl.BlockSpec((HEAD_PAIRS, tm, LANES), lambda i: (0, i, 0)),
            pl.BlockSpec((tm, GLA_WIDTH), lambda i: (i, 0)),
            row, mod,
            pl.BlockSpec((1, ATT_WIDTH), lambda i: (0, 0)),
            pl.BlockSpec((ATT_WIDTH + GLA_WIDTH, D_MODEL), lambda i: (0, 0)),
            pl.BlockSpec((1, D_MODEL), lambda i: (0, 0)),
            mod, mod,
        ],
        out_specs=(row, row),
        compiler_params=_cparams(("arbitrary",)),
        name="outproj",
    )(att, gla_y, x, gate, att_norm_g, w_out, g2, sc2, sh2)


def _ffn_kernel(h_hbm, hn_ref, gate_ref, fg_ref, wu_ref, wd_ref, y_ref, h_ref, h_sem):
    i = pl.program_id(0)
    f = pl.program_id(1)
    last = pl.num_programs(1) - 1
    tm = hn_ref.shape[0]

    def residual_copy():
        rows = pl.ds(pl.multiple_of(i * tm, tm), tm)
        return pltpu.make_async_copy(h_hbm.at[rows, :], h_ref, h_sem)

    def hidden():
        u = jnp.maximum(_dot(hn_ref[...], wu_ref[...]), 0.0)
        return (u * u).astype(BF16)

    @pl.when(f == 0)
    def _():
        residual_copy().start()
        y_ref[...] = _dot(hidden(), wd_ref[...])

    @pl.when(jnp.logical_and(f > 0, f < last))
    def _():
        y_ref[...] += _dot(hidden(), wd_ref[...])

    @pl.when(f == last)
    def _():
        u2 = hidden()
        residual_copy().wait()
        width = D_MODEL // 4
        ss = jnp.zeros((h_ref.shape[0], 1), F32)
        gate = _mod_rows(gate_ref, h_ref.shape[0])
        for c in range(0, D_MODEL, width):
            cols = slice(c, c + width)
            y = y_ref[:, cols] + _dot(u2, wd_ref[:, cols])
            h2 = h_ref[:, cols] + gate[:, cols] * y
            y_ref[:, cols] = h2
            ss = ss + jnp.sum(h2 * h2, axis=-1, keepdims=True)
        y_ref[...] = y_ref[...] * lax.rsqrt(ss / D_MODEL + EPS) * fg_ref[...]


def _ffn(h, hn, gate, final_g, w_up, w_down, *, tm, tf):
    t = h.shape[0]
    assert D_FF // tf >= 2, "first and last hidden chunks are separate code paths"
    row = pl.BlockSpec((tm, D_MODEL), lambda i, f: (i, 0))
    return pl.pallas_call(
        _ffn_kernel,
        out_shape=jax.ShapeDtypeStruct((t, D_MODEL), F32),
        grid=(t // tm, D_FF // tf),
        in_specs=[
            pl.BlockSpec(memory_space=pl.ANY),
            row,
            _mod_spec(gate.shape[0], tm, t),
            pl.BlockSpec((1, D_MODEL), lambda i, f: (0, 0)),
            pl.BlockSpec((D_MODEL, tf), lambda i, f: (0, f)),
            pl.BlockSpec((tf, D_MODEL), lambda i, f: (f, 0)),
        ],
        out_specs=row,
        scratch_shapes=[pltpu.VMEM((tm, D_MODEL), F32), pltpu.SemaphoreType.DMA(())],
        compiler_params=_cparams(("arbitrary", "arbitrary")),
        name="ffn",
    )(h, hn, gate, final_g, w_up, w_down)


def _split_ada(ada):
    return [ada[:, D_MODEL * i:D_MODEL * (i + 1)] for i in range(6)]


def kernel(x_prompt, x_sample, c_prompt, c_sample, cache_win_k, cache_win_v, state_gla, w_in, gla_gate_w, gla_gate_b, att_norm_g, gla_norm_g, w_out, norm1_g, norm2_g, w_ada, b_ada, w_up, w_down, final_g):
    assert w_in.shape[0] == 1, "single-layer model"
    n_p, t_p, _ = x_prompt.shape
    n_s, t_s, _ = x_sample.shape
    assert n_p == 1
    rows_s = n_s * t_s
    wb_p = min(MAX_WINDOW, t_p)

    w_main = jnp.transpose(w_in[0]).astype(BF16)
    gate_w = gla_gate_w[0].astype(BF16)
    gate_b = gla_gate_b[0][None, :]
    n1g, n2g = norm1_g[0][None, :], norm2_g[0][None, :]
    ang, gng = att_norm_g[0][None, :], gla_norm_g[0][None, :]
    fg = final_g[None, :]

    c_rows = n_p + n_s
    c_pad = -c_rows % 8
    c_all = jnp.concatenate([c_prompt, c_sample, jnp.zeros((c_pad, D_MODEL), F32)], axis=0)
    ada = _ada(c_all, w_ada[0], b_ada[0][None, :])
    sh1_p, sc1_p, g1_p, sh2_p, sc2_p, g2_p = _split_ada(ada[:n_p])
    sh1_s, sc1_s, g1_s, sh2_s, sc2_s, g2_s = _split_ada(ada[n_p:c_rows])

    xp = x_prompt[0]
    cs_p, sn_p = _rope_tables(jnp.arange(t_p))
    outs = _inproj(xp, n1g, sc1_p, sh1_p, w_main, gate_w, gate_b, cs_p, sn_p,
                   tm=1024, emit_att=True, tail_rows=wb_p)
    att, w_out_b, w_up_b, w_down_b = _prompt_attention(outs[:3], (w_out[0], w_up[0], w_down[0]))
    gla_in, log_a = outs[3], outs[4]
    win_k, win_v = outs[5][0], outs[5][1]
    gla_y, st_p = _gla_prompt(gla_in, log_a, gng, tb=512, chunk=64, sub=16)
    h_p, hn_p = _outproj(att, gla_y, xp, g1_p, ang, w_out_b, n2g, sc2_p, sh2_p, tm=512)
    y_p = _ffn(h_p, hn_p, g2_p, fg, w_up_b, w_down_b, tm=1024, tf=1024)

    xs = x_sample.reshape(rows_s, D_MODEL)
    cs_s, sn_s = _rope_tables(jnp.tile(PAST_LEN + jnp.arange(t_s), n_s))
    gla_in_s, log_a_s = _inproj(xs, n1g, sc1_s, sh1_s, w_main, gate_w, gate_b, cs_s, sn_s,
                                tm=rows_s, emit_att=False)
    qkv_s = _qkv_rows(xs, n1g, sc1_s, sh1_s, w_main, cs_s, sn_s,
                      tm=rows_s, first_row=0, rows=rows_s, col0=0)
    new_k, new_v = qkv_s[1], qkv_s[2]
    wc = cache_win_k.shape[2]
    cache_t = lambda c: jnp.transpose(c[0], (0, 2, 3, 1)).reshape(n_s, ATT_WIDTH, wc)
    att_tok = _sample_attention(qkv_s[0], new_k, new_v, cache_t(cache_win_k), cache_t(cache_win_v),
                                steps=t_s)
    att_s = jnp.transpose(att_tok.reshape(rows_s, HEAD_PAIRS, LANES), (1, 0, 2)).astype(BF16)
    gla_y_s, st_s = _gla_sample(gla_in_s.astype(F32), log_a_s, gng, state_gla[0], steps=t_s,
                                group=4 if n_s % 4 == 0 else 1)
    h_s, hn_s = _outproj(att_s, gla_y_s.astype(BF16), xs, g1_s, ang, w_out_b, n2g, sc2_s, sh2_s,
                         tm=rows_s)
    y_s = _ffn(h_s, hn_s, g2_s, fg, w_up_b, w_down_b, tm=rows_s, tf=2048)

    return (
        y_p[None],
        y_s.reshape(n_s, t_s, D_MODEL),
        win_k.reshape(1, n_p, wb_p, ATT_HEADS, ATT_HEAD_DIM),
        win_v.reshape(1, n_p, wb_p, ATT_HEADS, ATT_HEAD_DIM),
        st_p[None, None],
        new_k.reshape(1, n_s, t_s, ATT_HEADS, ATT_HEAD_DIM),
        new_v.reshape(1, n_s, t_s, ATT_HEADS, ATT_HEAD_DIM),
        st_s[None],
    )
```

```python
import functools

import jax
import jax.numpy as jnp
from jax import lax
from jax.experimental import pallas as pl
from jax.experimental.pallas import tpu as pltpu

F32 = jnp.float32
BF16 = jnp.bfloat16

D_MODEL = 2048
ATT_WIDTH = 1024
ATT_HEADS = 16
ATT_HEAD_DIM = 64
ROT_DIM = 16
ROPE_THETA = 500000.0
DILATIONS = (1, 4, 16)
BAND = 128
MAX_WINDOW = 2048
GLA_HEADS = 4
GLA_DK = 128
GLA_DV = 256
GLA_WIDTH = 1024
GLA_GATE_RANK = 16
GLA_TAU = 16.0
D_FF = 8192
EPS = 1e-6
PAST_LEN = 16384
NEG = -1e30

LANES = 128
HEAD_PAIRS = ATT_WIDTH // LANES
VMEM_LIMIT = 60000 * 1024


def _cparams(sem):
    return pltpu.CompilerParams(dimension_semantics=sem, vmem_limit_bytes=VMEM_LIMIT)


def _dot(a, b):
    return jnp.dot(a, b, preferred_element_type=F32)


def _dot_nt(a, b):
    return lax.dot_general(a, b, (((1,), (1,)), ((), ())), preferred_element_type=F32)


def _dot_tn(a, b):
    return lax.dot_general(a, b, (((0,), (0,)), ((), ())), preferred_element_type=F32)


def _silu(x):
    return x / (1.0 + jnp.exp(-x))


def _ada_kernel(c_ref, w_ref, b_ref, o_ref):
    s = _silu(c_ref[...]).astype(BF16)
    o_ref[...] = _dot(s, w_ref[...].astype(BF16)) + b_ref[...]


def _ada(c, w_ada, b_ada):
    rows = c.shape[0]
    n = w_ada.shape[1]
    tn = 1024
    return pl.pallas_call(
        _ada_kernel,
        out_shape=jax.ShapeDtypeStruct((rows, n), F32),
        grid=(n // tn,),
        in_specs=[
            pl.BlockSpec((rows, D_MODEL), lambda j: (0, 0)),
            pl.BlockSpec((D_MODEL, tn), lambda j: (0, j)),
            pl.BlockSpec((1, tn), lambda j: (0, j)),
        ],
        out_specs=pl.BlockSpec((rows, tn), lambda j: (0, j)),
        compiler_params=_cparams(("arbitrary",)),
        name="ada",
    )(c, w_ada, b_ada)


def _rope(slab, cs, sn):
    d = lax.broadcasted_iota(jnp.int32, (1, LANES), 1) % ATT_HEAD_DIM
    half = ROT_DIM // 2
    from_hi = jnp.where(d < half, -1.0, 0.0)
    from_lo = jnp.where((d >= half) & (d < ROT_DIM), 1.0, 0.0)
    return (slab * cs + pltpu.roll(slab, LANES - half, 1) * (sn * from_hi)
            + pltpu.roll(slab, half, 1) * (sn * from_lo))


def _norm_mod(x, g, sc, sh):
    rs = lax.rsqrt(jnp.mean(x * x, axis=-1, keepdims=True) + EPS)
    return x * rs * g * (1.0 + sc) + sh


def _mod_rows(ref, rows):
    m = ref[...]
    if m.shape[0] in (1, rows):
        return m
    run = rows // m.shape[0]
    return jnp.broadcast_to(m[:, None, :], (m.shape[0], run, m.shape[1])).reshape(rows, m.shape[1])


def _inproj_kernel(*refs, emit_att, first_tail):
    (x_ref, g_ref, sc_ref, sh_ref, w_ref, wag_ref, gw_ref, gb_ref, cs_ref, sn_ref), refs = (
        refs[:10], refs[10:])
    i = pl.program_id(0)
    j = pl.program_id(1)
    if emit_att:
        (out1, out4, out16), refs = refs[:3], refs[3:]
        gla_ref, la_ref, tail_ref, xn_ref, s0_ref, s1_ref, x_buf, x_sem = refs
        tm = x_buf.shape[0]
        x_hbm, x_ref = x_ref, x_buf

        def x_copy(tile):
            rows = pl.ds(pl.multiple_of(tile * tm, tm), tm)
            return pltpu.make_async_copy(x_hbm.at[rows, :], x_buf, x_sem)

        @pl.when(jnp.logical_and(i == 0, j == 0))
        def _():
            x_copy(0).start()

        @pl.when(jnp.logical_and(j == 1, i + 1 < pl.num_programs(0)))
        def _():
            x_copy(i + 1).start()
    else:
        gla_ref, la_ref, xn_ref = refs[:3]
    col = j if emit_att else j + 3

    def normalise():
        if emit_att:
            x_copy(i).wait()
        rows = x_ref.shape[0]
        xnb = _norm_mod(x_ref[...], g_ref[...], _mod_rows(sc_ref, rows),
                        _mod_rows(sh_ref, rows)).astype(BF16)
        xn_ref[...] = xnb
        ag = _dot_nt(xnb, wag_ref[...])
        z = _dot(ag.astype(BF16), gw_ref[...]) + gb_ref[...]
        log_sig = jnp.minimum(z, 0.0) - jnp.log(1.0 + jnp.exp(-jnp.abs(z)))
        la_ref[...] = log_sig / GLA_TAU

    if emit_att:
        n4, n16 = tm // 4, tm // 16
        group = 2 * LANES
        in_tail = i >= first_tail

        def emit(slab_fn, keep_f32=False):
            for hp in range(HEAD_PAIRS):
                half = hp % 2
                if half == 0:
                    acc = _dot_nt(xn_ref[...], w_ref[LANES * hp:LANES * hp + group, :])
                slab = slab_fn(acc[:, LANES * half:LANES * (half + 1)])
                out1[0, hp] = slab.astype(BF16)
                if keep_f32:
                    tail_ref[0, :, LANES * hp:LANES * (hp + 1)] = slab
                s0_ref[half] = slab
                for r in range(4):
                    part = s0_ref[half, pl.ds(r, n4, stride=4), :]
                    out4[0, hp, :, LANES * r:LANES * (r + 1)] = part.astype(BF16)
                    s1_ref[half, n4 * r:n4 * (r + 1), :] = part
                for r in range(16):
                    part = s1_ref[half, pl.ds(n4 * (r % 4) + r // 4, n16, stride=4), :]
                    out16[0, hp, :, LANES * r:LANES * (r + 1)] = part.astype(BF16)

        @pl.when(col == 0)
        def _():
            normalise()
            emit(lambda s: _rope(s, cs_ref[...], sn_ref[...]) * (ATT_HEAD_DIM ** -0.5))

        for c, slab_fn in ((1, lambda s: _rope(s, cs_ref[...], sn_ref[...])), (2, lambda s: s)):
            pl.when(jnp.logical_and(col == c, jnp.logical_not(in_tail)))(
                functools.partial(emit, slab_fn))
            pl.when(jnp.logical_and(col == c, in_tail))(
                functools.partial(emit, slab_fn, keep_f32=True))
    else:
        pl.when(j == 0)(normalise)

    @pl.when(col == 3)
    def _():
        acc = _dot_nt(xn_ref[...], w_ref[...])
        half = GLA_HEADS * GLA_DK
        gla_ref[:, :half] = (acc[:, :half] * (GLA_DK ** -0.5)).astype(BF16)
        gla_ref[:, half:] = acc[:, half:].astype(BF16)

    @pl.when(col >= 4)
    def _():
        gla_ref[...] = _dot_nt(xn_ref[...], w_ref[...]).astype(BF16)


def _mod_spec(mod_rows, tm, t):
    if mod_rows == 1:
        return pl.BlockSpec((1, D_MODEL), lambda i, j: (0, 0))
    return pl.BlockSpec((mod_rows * tm // t, D_MODEL), lambda i, j: (i, 0))


def _inproj(x, g, sc, sh, w_all, gate_w, gate_b, cs, sn, *, tm, emit_att, tail_rows=0):
    t = x.shape[0]
    col0 = 0 if emit_att else 3
    first_tail = (t - tail_rows) // tm
    mod = _mod_spec(sc.shape[0], tm, t)
    const = lambda block: pl.BlockSpec(block, lambda i, j: (0, 0))
    out_shape, out_specs, scratch = [], [], [pltpu.VMEM((tm, D_MODEL), BF16)]
    if emit_att:
        for d in DILATIONS:
            out_shape.append(jax.ShapeDtypeStruct((3, HEAD_PAIRS, t // d, d * LANES), BF16))
            out_specs.append(pl.BlockSpec((1, HEAD_PAIRS, tm // d, d * LANES),
                                          lambda i, j: (jnp.minimum(j, 2), 0, i, 0)))
        scratch += [pltpu.VMEM((2, tm, LANES), F32) for _ in range(2)]
        scratch += [pltpu.VMEM((tm, D_MODEL), F32), pltpu.SemaphoreType.DMA(())]
    out_shape += [jax.ShapeDtypeStruct((t, 3 * 1024), BF16),
                  jax.ShapeDtypeStruct((t, GLA_HEADS * GLA_DK), F32)]
    out_specs += [pl.BlockSpec((tm, 1024), lambda i, j: (i, jnp.maximum(j + col0 - 3, 0))),
                  pl.BlockSpec((tm, GLA_HEADS * GLA_DK), lambda i, j: (i, 0))]
    if emit_att:
        assert tail_rows % tm == 0 and 0 < tail_rows <= t
        out_shape.append(jax.ShapeDtypeStruct((2, tail_rows, ATT_WIDTH), F32))
        out_specs.append(pl.BlockSpec(
            (1, tm, ATT_WIDTH),
            lambda i, j: (jnp.where(i < first_tail, 0, jnp.clip(j - 1, 0, 1)),
                          jnp.maximum(i - first_tail, 0), 0)))
    gate_rows = (6 * 1024) // GLA_GATE_RANK
    return pl.pallas_call(
        functools.partial(_inproj_kernel, emit_att=emit_att, first_tail=first_tail),
        out_shape=tuple(out_shape),
        grid=(t // tm, 6 - col0),
        in_specs=[
            (pl.BlockSpec(memory_space=pl.ANY) if emit_att
             else pl.BlockSpec((tm, D_MODEL), lambda i, j: (i, 0))),
            const((1, D_MODEL)),
            mod, mod,
            pl.BlockSpec((1024, D_MODEL), lambda i, j: (j + col0, 0)),
            pl.BlockSpec((GLA_GATE_RANK, D_MODEL), lambda i, j: (gate_rows, 0)),
            const((GLA_GATE_RANK, GLA_HEADS * GLA_DK)),
            const((1, GLA_HEADS * GLA_DK)),
            pl.BlockSpec((tm, LANES), lambda i, j: (i, 0)),
            pl.BlockSpec((tm, LANES), lambda i, j: (i, 0)),
        ],
        out_specs=tuple(out_specs),
        scratch_shapes=scratch,
        compiler_params=_cparams(("arbitrary", "arbitrary")),
        name="inproj_att" if emit_att else "inproj_gla",
    )(x, g, sc, sh, w_all, w_all, gate_w, gate_b, cs, sn)


def _qkv_rows_kernel(x_ref, g_ref, sc_ref, sh_ref, w_ref, cs_ref, sn_ref, o_ref, xn_ref, *, col0):
    j = pl.program_id(1)

    @pl.when(j == 0)
    def _():
        xn_ref[...] = _norm_mod(x_ref[...], g_ref[...], _mod_rows(sc_ref, x_ref.shape[0]),
                                _mod_rows(sh_ref, x_ref.shape[0])).astype(BF16)

    acc = _dot_nt(xn_ref[...], w_ref[...])
    col = j + col0

    @pl.when(col <= 1)
    def _():
        scale = jnp.where(col == 0, ATT_HEAD_DIM ** -0.5, 1.0)
        for hp in range(HEAD_PAIRS):
            lanes = slice(LANES * hp, LANES * (hp + 1))
            o_ref[0, :, lanes] = _rope(acc[:, lanes], cs_ref[...], sn_ref[...]) * scale

    @pl.when(col == 2)
    def _():
        o_ref[0] = acc


def _qkv_rows(x, g, sc, sh, w_main, cs, sn, *, tm, first_row, rows, col0):
    first = first_row // tm
    assert sc.shape[0] == 1 or first == 0
    ncols = 3 - col0
    mod = _mod_spec(sc.shape[0], tm, rows)
    return pl.pallas_call(
        functools.partial(_qkv_rows_kernel, col0=col0),
        out_shape=jax.ShapeDtypeStruct((ncols, rows, ATT_WIDTH), F32),
        grid=(rows // tm, ncols),
        in_specs=[
            pl.BlockSpec((tm, D_MODEL), lambda i, j: (i + first, 0)),
            pl.BlockSpec((1, D_MODEL), lambda i, j: (0, 0)),
            mod, mod,
            pl.BlockSpec((1024, D_MODEL), lambda i, j: (j + col0, 0)),
            pl.BlockSpec((tm, LANES), lambda i, j: (i + first, 0)),
            pl.BlockSpec((tm, LANES), lambda i, j: (i + first, 0)),
        ],
        out_specs=pl.BlockSpec((1, tm, ATT_WIDTH), lambda i, j: (j, i, 0)),
        scratch_shapes=[pltpu.VMEM((tm, D_MODEL), BF16)],
        compiler_params=_cparams(("arbitrary", "arbitrary")),
        name="qkv_rows",
    )(x, g, sc, sh, w_main, cs, sn)


def _rope_tables(pos):
    half = ROT_DIM // 2
    inv_freq = ROPE_THETA ** (-jnp.arange(half, dtype=F32) / half)
    per_head = jnp.concatenate([inv_freq, inv_freq, jnp.zeros((ATT_HEAD_DIM - ROT_DIM,), F32)])
    ang = pos.astype(F32)[:, None] * jnp.tile(per_head, LANES // ATT_HEAD_DIM)[None, :]
    return jnp.cos(ang), jnp.sin(ang)


def _attn_kernel(*refs, n_cast):
    (q1, k1c, v1c, k1p, v1p, q4, k4c, v4c, k4p, v4p, q16, k16c, v16c, k16p, v16p), refs = (
        refs[:15], refs[15:])
    cast_in, o_ref, cast_out = refs[:n_cast], refs[n_cast], refs[n_cast + 1:2 * n_cast + 1]
    a1, m1, l1, a4, m4, l4, t16, s_scr, p_scr, bias_ref, o_scr = refs[2 * n_cast + 1:]
    for src, dst in zip(cast_in, cast_out):
        dst[...] = src[...].astype(dst.dtype)

    first_block = pl.program_id(1) == 0
    lane = lax.broadcasted_iota(jnp.int32, (1, LANES), 1)
    first = lane < ATT_HEAD_DIM
    head0 = jnp.where(first, 1.0, 0.0).astype(BF16)
    head1 = jnp.where(first, 0.0, 1.0).astype(BF16)
    ones = jnp.ones((2 * BAND, LANES), BF16)

    qi = lax.broadcasted_iota(jnp.int32, (2 * BAND, 2 * BAND), 0) % BAND
    kk = lax.broadcasted_iota(jnp.int32, (2 * BAND, 2 * BAND), 1)
    full = jnp.where(jnp.where(kk < BAND, kk - qi, qi - (kk - BAND)) >= 0, 0.0, NEG)
    bias_ref[0] = full
    bias_ref[1] = jnp.where(kk < BAND, full + jnp.where(first_block, NEG, 0.0), full)

    n1 = q1.shape[1] // BAND
    rows4 = q4.shape[1]
    n4 = rows4 // BAND

    def window(cur, prev, i, cols):
        if i == 0:
            return jnp.concatenate([prev[0, :, cols], cur[0, :BAND, cols]], axis=0)
        return cur[0, BAND * (i - 1):BAND * (i + 1), cols]

    jobs = []
    everything = slice(None)
    for i in range(n1):
        jobs.append(dict(q=(q1, i, everything), k=(k1c, k1p), v=(v1c, v1p), bias=int(i == 0),
                         dst=(a1, m1, l1, BAND * i)))
    for r in range(4):
        cols = slice(LANES * r, LANES * (r + 1))
        for i in range(n4):
            jobs.append(dict(q=(q4, i, cols), k=(k4c, k4p), v=(v4c, v4p), bias=int(i == 0),
                             dst=(a4, m4, l4, rows4 * r + BAND * i)))
    for r in range(16):
        cols = slice(LANES * r, LANES * (r + 1))
        jobs.append(dict(q=(q16, 0, cols), k=(k16c, k16p), v=(v16c, v16p), bias=1, merge=r))

    def scores(job, slot):
        ref, i, cols = job["q"]
        q = ref[0, BAND * i:BAND * (i + 1), cols]
        q2 = jnp.concatenate([q * head0, q * head1], axis=0)
        s_scr[slot] = _dot_nt(q2, window(*job["k"], i, cols))

    def softmax(job, slot):
        s = s_scr[slot] + bias_ref[job["bias"]]
        m = jnp.max(s, axis=-1, keepdims=True)
        p_scr[slot] = jnp.exp(s - m).astype(BF16)
        m = jnp.where(first, m[:BAND], m[BAND:])
        if "merge" in job:
            t16[slot] = m
        else:
            job["dst"][1][pl.ds(job["dst"][3], BAND), :] = m

    def values(job, slot):
        _, i, cols = job["q"]
        vw = jnp.concatenate([window(*job["v"], i, cols), ones], axis=1)
        o2 = _dot(p_scr[slot], vw)
        acc = jnp.where(first, o2[:BAND, :LANES], o2[BAND:, :LANES])
        den = jnp.where(first, o2[:BAND, LANES:], o2[BAND:, LANES:])
        if "merge" not in job:
            a_ref, _, l_ref, row = job["dst"]
            a_ref[pl.ds(row, BAND), :] = acc
            l_ref[pl.ds(row, BAND), :] = den
            return
        r = job["merge"]
        sl1 = pl.ds(r, BAND, stride=16)
        sl4 = pl.ds((r % 4) * rows4 + r // 4, BAND, stride=4)
        ma, mb, mc = m1[sl1, :], m4[sl4, :], t16[slot]
        m = jnp.maximum(jnp.maximum(ma, mb), mc)
        wa, wb, wc = jnp.exp(ma - m), jnp.exp(mb - m), jnp.exp(mc - m)
        num = wa * a1[sl1, :] + wb * a4[sl4, :] + wc * acc
        dsum = wa * l1[sl1, :] + wb * l4[sl4, :] + wc * den
        o_scr[sl1, :] = num / dsum

    slots = s_scr.shape[0]
    for step in range(len(jobs) + 2):
        if step < len(jobs):
            scores(jobs[step], step % slots)
        if 1 <= step <= len(jobs):
            softmax(jobs[step - 1], (step - 1) % slots)
        if step >= 2:
            values(jobs[step - 2], (step - 2) % slots)
    o_ref[0] = o_scr[...].astype(o_ref.dtype)


def _prompt_attention(views, weights):
    _, hp, t, _ = views[0].shape
    sb = 16 * BAND
    ns = t // sb
    in_specs, args = [], []
    for arr, d in zip(views, DILATIONS):
        rows, width = sb // d, d * LANES
        nb = rows // BAND

        def cur(c, rows=rows, width=width):
            return pl.BlockSpec((None, 1, rows, width), lambda h, s: (c, h, s, 0))

        def prev(c, width=width, nb=nb):
            return pl.BlockSpec((None, 1, BAND, width),
                                lambda h, s: (c, h, jnp.maximum(s * nb - 1, 0), 0))

        in_specs += [cur(0), cur(1), cur(2), prev(1), prev(2)]
        args += [arr] * 5
    cast_specs = []
    for w in weights:
        slab = w.shape[0] // (hp * ns)
        assert slab * hp * ns == w.shape[0] and slab % 16 == 0
        cast_specs.append(pl.BlockSpec((slab, w.shape[1]), lambda h, s: (h * ns + s, 0)))
    tile = (2 * BAND, 2 * BAND)
    return pl.pallas_call(
        functools.partial(_attn_kernel, n_cast=len(weights)),
        out_shape=tuple([jax.ShapeDtypeStruct((hp, t, LANES), BF16)]
                        + [jax.ShapeDtypeStruct(w.shape, BF16) for w in weights]),
        grid=(hp, ns),
        in_specs=in_specs + cast_specs,
        out_specs=tuple([pl.BlockSpec((1, sb, LANES), lambda h, s: (h, s, 0))] + cast_specs),
        scratch_shapes=(
            [pltpu.VMEM((sb, LANES), F32) for _ in range(6)]
            + [pltpu.VMEM((3, BAND, LANES), F32),
               pltpu.VMEM((3,) + tile, F32),
               pltpu.VMEM((3,) + tile, BF16),
               pltpu.VMEM((2,) + tile, F32),
               pltpu.VMEM((sb, LANES), F32)]),
        compiler_params=_cparams(("arbitrary", "arbitrary")),
        name="prompt_attn",
    )(*args, *weights)


def _gla_chunk(q, k, v, la, st_ref, h, tri, sub):
    c = q.shape[0]
    la_hi = la.astype(BF16)
    la_lo = (la - la_hi.astype(F32)).astype(BF16)
    b = _dot(tri, la_hi) + _dot(tri, la_lo)
    st = st_ref[h]
    o = _dot_nt((q * jnp.exp(b)).astype(BF16), st.astype(BF16))
    srow = lax.broadcasted_iota(jnp.int32, (c, GLA_DK), 0)
    att_rows = []
    for i in range(c // sub):
        lo, hi = i * sub, (i + 1) * sub
        ref = b[lo:lo + 1, :]
        qh = (q[lo:hi] * jnp.exp(b[lo:hi] - ref)).astype(BF16)
        kh = (k * jnp.exp(jnp.where(srow < hi, ref - b, 0.0))).astype(BF16)
        a = _dot_nt(qh, kh)
        ti = lax.broadcasted_iota(jnp.int32, (sub, c), 0) + lo
        si = lax.broadcasted_iota(jnp.int32, (sub, c), 1)
        att_rows.append(jnp.where(si <= ti, a, 0.0))
    att = att_rows[0] if len(att_rows) == 1 else jnp.concatenate(att_rows, axis=0)
    o = o + _dot(att.astype(BF16), v)
    b_last = b[c - 1:c, :]
    kbar = (k * jnp.exp(b_last - b)).astype(BF16)
    st_ref[h] = st * jnp.exp(b_last) + _dot_tn(v, kbar)
    return o


def _gla_out(o, g, gg):
    rs = lax.rsqrt(jnp.mean(o * o, axis=-1, keepdims=True) + EPS)
    return o * rs * g * _silu(gg)


def _tri(c):
    r = lax.broadcasted_iota(jnp.int32, (c, c), 0)
    s = lax.broadcasted_iota(jnp.int32, (c, c), 1)
    return jnp.where(s <= r, 1.0, 0.0).astype(BF16)


def _gla_prompt_kernel(qk_ref, v_ref, gg_ref, la_ref, g_ref, y_ref, s_ref, st_ref,
                       qd_ref, kl_ref, qs_ref, ks_ref, dec_ref, att_ref, o_ref, *, chunk, sub):
    i = pl.program_id(0)

    @pl.when(i == 0)
    def _():
        st_ref[...] = jnp.zeros_like(st_ref)

    tb = qk_ref.shape[0]
    nchunks, nsub = tb // chunk, chunk // sub
    half = GLA_HEADS * GLA_DK

    r = lax.broadcasted_iota(jnp.int32, (tb, tb), 0)
    c = lax.broadcasted_iota(jnp.int32, (tb, tb), 1)
    start = r - jnp.bitwise_and(r, chunk - 1)
    tri = jnp.where(c <= r, jnp.where(c >= start, 1.0, 0.0), 0.0).astype(BF16)
    la = la_ref[...]
    la_hi = la.astype(BF16)
    la_lo = (la - la_hi.astype(F32)).astype(BF16)
    b = _dot(tri, la_hi) + _dot(tri, la_lo)
    q = qk_ref[:, :half].astype(F32)
    k = qk_ref[:, half:].astype(F32)
    qd_ref[...] = (q * jnp.exp(b)).astype(BF16)
    srow = lax.broadcasted_iota(jnp.int32, (chunk, half), 0)
    for ci in range(nchunks):
        lo = ci * chunk
        bc, kc_ = b[lo:lo + chunk], k[lo:lo + chunk]
        b_last = bc[chunk - 1:chunk]
        dec_ref[ci:ci + 1, :] = jnp.exp(b_last)
        kl_ref[lo:lo + chunk, :] = (kc_ * jnp.exp(b_last - bc)).astype(BF16)
        for si in range(nsub):
            s_lo, s_hi = si * sub, (si + 1) * sub
            ref = bc[s_lo:s_lo + 1]
            qs_ref[lo + s_lo:lo + s_hi, :] = (
                q[lo + s_lo:lo + s_hi] * jnp.exp(bc[s_lo:s_hi] - ref)).astype(BF16)
            ks_ref[si, lo:lo + chunk, :] = (
                kc_ * jnp.exp(jnp.where(srow < s_hi, ref - bc, 0.0))).astype(BF16)

    ti = lax.broadcasted_iota(jnp.int32, (chunk, chunk), 0)
    si_ = lax.broadcasted_iota(jnp.int32, (chunk, chunk), 1)
    causal = si_ <= ti
    jobs = [(ci, h) for ci in range(nchunks) for h in range(GLA_HEADS)]

    def intra(job, slot):
        ci, h = job
        lo, kc = ci * chunk, slice(GLA_DK * h, GLA_DK * (h + 1))
        parts = [_dot_nt(qs_ref[lo + s * sub:lo + (s + 1) * sub, kc], ks_ref[s, lo:lo + chunk, kc])
                 for s in range(nsub)]
        att_ref[slot] = jnp.where(causal, jnp.concatenate(parts, axis=0), 0.0).astype(BF16)

    def state(job, slot):
        ci, h = job
        rows = slice(ci * chunk, (ci + 1) * chunk)
        kc, vc = slice(GLA_DK * h, GLA_DK * (h + 1)), slice(GLA_DV * h, GLA_DV * (h + 1))
        st = st_ref[h]
        v = v_ref[rows, vc]
        o_ref[slot] = _dot_nt(qd_ref[rows, kc], st.astype(BF16)) + _dot(att_ref[slot], v)
        st_ref[h] = st * dec_ref[ci:ci + 1, kc] + _dot_tn(v, kl_ref[rows, kc])

    def finish(job, slot):
        ci, h = job
        rows, vc = slice(ci * chunk, (ci + 1) * chunk), slice(GLA_DV * h, GLA_DV * (h + 1))
        y_ref[rows, vc] = _gla_out(o_ref[slot], g_ref[:, vc],
                                   gg_ref[rows, vc].astype(F32)).astype(y_ref.dtype)

    slots = att_ref.shape[0]
    for step in range(len(jobs) + 2):
        if step < len(jobs):
            intra(jobs[step], step % slots)
        if 1 <= step <= len(jobs):
            state(jobs[step - 1], (step - 1) % slots)
        if step >= 2:
            finish(jobs[step - 2], (step - 2) % slots)

    @pl.when(i == pl.num_programs(0) - 1)
    def _():
        for h in range(GLA_HEADS):
            s_ref[h] = st_ref[h].T


def _gla_prompt(gla_in, log_a, norm_g, *, tb, chunk, sub):
    t = gla_in.shape[0]
    return pl.pallas_call(
        functools.partial(_gla_prompt_kernel, chunk=chunk, sub=sub),
        out_shape=(
            jax.ShapeDtypeStruct((t, GLA_WIDTH), BF16),
            jax.ShapeDtypeStruct((GLA_HEADS, GLA_DK, GLA_DV), F32),
        ),
        grid=(t // tb,),
        in_specs=[
            pl.BlockSpec((tb, 1024), lambda i: (i, 0)),
            pl.BlockSpec((tb, 1024), lambda i: (i, 1)),
            pl.BlockSpec((tb, 1024), lambda i: (i, 2)),
            pl.BlockSpec((tb, GLA_HEADS * GLA_DK), lambda i: (i, 0)),
            pl.BlockSpec((1, GLA_WIDTH), lambda i: (0, 0)),
        ],
        out_specs=(
            pl.BlockSpec((tb, GLA_WIDTH), lambda i: (i, 0)),
            pl.BlockSpec((GLA_HEADS, GLA_DK, GLA_DV), lambda i: (0, 0, 0)),
        ),
        scratch_shapes=[
            pltpu.VMEM((GLA_HEADS, GLA_DV, GLA_DK), F32),
            pltpu.VMEM((tb, GLA_HEADS * GLA_DK), BF16),
            pltpu.VMEM((tb, GLA_HEADS * GLA_DK), BF16),
            pltpu.VMEM((tb, GLA_HEADS * GLA_DK), BF16),
            pltpu.VMEM((chunk // sub, tb, GLA_HEADS * GLA_DK), BF16),
            pltpu.VMEM((tb // chunk, GLA_HEADS * GLA_DK), F32),
            pltpu.VMEM((3, chunk, chunk), BF16),
            pltpu.VMEM((3, chunk, GLA_DV), F32),
        ],
        compiler_params=_cparams(("arbitrary",)),
        name="gla_prompt",
    )(gla_in, gla_in, gla_in, log_a, norm_g)


def _gla_sample_kernel(qk_ref, v_ref, gg_ref, la_ref, g_ref, s0_ref, y_ref, s_ref, st_ref, *, steps):
    group = s0_ref.shape[0]
    c = 2 * steps
    tri = _tri(c)
    half = GLA_HEADS * GLA_DK
    pad = lambda a: jnp.concatenate([a, jnp.zeros_like(a)], axis=0)
    for n in range(group):
        rows = slice(n * steps, (n + 1) * steps)
        for h in range(GLA_HEADS):
            idx = n * GLA_HEADS + h
            kc = slice(GLA_DK * h, GLA_DK * (h + 1))
            vc = slice(GLA_DV * h, GLA_DV * (h + 1))
            st_ref[idx] = s0_ref[n, h].T
            q = pad(qk_ref[rows, kc])
            k = pad(qk_ref[rows, half + GLA_DK * h:half + GLA_DK * (h + 1)])
            v = pad(v_ref[rows, vc]).astype(BF16)
            o = _gla_chunk(q, k, v, pad(la_ref[rows, kc]), st_ref, idx, tri, c)
            y_ref[rows, vc] = _gla_out(o[:steps], g_ref[:, vc], gg_ref[rows, vc])
            s_ref[n, h] = st_ref[idx].T


def _gla_sample(gla_in, log_a, norm_g, state, *, steps, group):
    t = gla_in.shape[0]
    rows = steps * group
    return pl.pallas_call(
        functools.partial(_gla_sample_kernel, steps=steps),
        out_shape=(
            jax.ShapeDtypeStruct((t, GLA_WIDTH), F32),
            jax.ShapeDtypeStruct(state.shape, F32),
        ),
        grid=(t // rows,),
        in_specs=[
            pl.BlockSpec((rows, 1024), lambda n: (n, 0)),
            pl.BlockSpec((rows, 1024), lambda n: (n, 1)),
            pl.BlockSpec((rows, 1024), lambda n: (n, 2)),
            pl.BlockSpec((rows, GLA_HEADS * GLA_DK), lambda n: (n, 0)),
            pl.BlockSpec((1, GLA_WIDTH), lambda n: (0, 0)),
            pl.BlockSpec((group, GLA_HEADS, GLA_DK, GLA_DV), lambda n: (n, 0, 0, 0)),
        ],
        out_specs=(
            pl.BlockSpec((rows, GLA_WIDTH), lambda n: (n, 0)),
            pl.BlockSpec((group, GLA_HEADS, GLA_DK, GLA_DV), lambda n: (n, 0, 0, 0)),
        ),
        scratch_shapes=[pltpu.VMEM((group * GLA_HEADS, GLA_DV, GLA_DK), F32)],
        compiler_params=_cparams(("arbitrary",)),
        name="gla_sample",
    )(gla_in, gla_in, gla_in, log_a, norm_g, state)


def _multiplicity(delta):
    total = jnp.zeros(delta.shape, F32)
    for d in DILATIONS:
        assert d & (d - 1) == 0
        hit = jnp.where(jnp.bitwise_and(delta, d - 1) == 0, 1.0, 0.0)
        total = total + jnp.where(delta <= BAND * d, hit, 0.0)
    return jnp.where(delta >= 0, total, 0.0)


def _sample_attn_kernel(q_ref, kn_ref, vn_ref, ckt_ref, cvt_ref, o_ref):
    steps = q_ref.shape[0]
    rows = ATT_HEADS * steps
    cache_len = ckt_ref.shape[2]
    qt = jnp.concatenate([q_ref[...]] * ATT_HEADS, axis=0)
    rh = lax.broadcasted_iota(jnp.int32, (rows, ATT_WIDTH), 0) // steps
    ch = lax.broadcasted_iota(jnp.int32, (rows, ATT_WIDTH), 1) // ATT_HEAD_DIM
    own = rh == ch
    qb = jnp.where(own, qt, 0.0).astype(BF16)
    pad = jnp.zeros((LANES - steps, ATT_WIDTH), F32)
    kn = jnp.concatenate([kn_ref[...], pad], axis=0).astype(BF16)
    vn = jnp.concatenate([vn_ref[...], pad], axis=0).astype(BF16)
    assert steps & (steps - 1) == 0
    t_c = jnp.bitwise_and(lax.broadcasted_iota(jnp.int32, (rows, cache_len), 0), steps - 1)
    mc = _multiplicity(cache_len + t_c - lax.broadcasted_iota(jnp.int32, (rows, cache_len), 1))
    t_n = jnp.bitwise_and(lax.broadcasted_iota(jnp.int32, (rows, LANES), 0), steps - 1)
    new = lax.broadcasted_iota(jnp.int32, (rows, LANES), 1)
    mn = jnp.where(new < steps, _multiplicity(t_n - new), 0.0)
    sc = jnp.where(mc > 0.0, _dot(qb, ckt_ref[0].astype(BF16)), NEG)
    sn = jnp.where(mn > 0.0, _dot_nt(qb, kn), NEG)
    m = jnp.maximum(jnp.max(sc, axis=-1, keepdims=True), jnp.max(sn, axis=-1, keepdims=True))
    pc = mc * jnp.exp(sc - m)
    pn = mn * jnp.exp(sn - m)
    den = jnp.sum(pc, axis=-1, keepdims=True) + jnp.sum(pn, axis=-1, keepdims=True)
    o = _dot_nt(pc.astype(BF16), cvt_ref[0].astype(BF16)) + _dot(pn.astype(BF16), vn)
    o = jnp.where(own, o / den, 0.0)
    o_ref[...] = jnp.sum(o.reshape(ATT_HEADS, steps, ATT_WIDTH), axis=0)


def _sample_attention(q, k_new, v_new, cache_kt, cache_vt, *, steps):
    nb, _, cache_len = cache_kt.shape
    tok = pl.BlockSpec((steps, ATT_WIDTH), lambda n: (n, 0))
    cache = pl.BlockSpec((1, ATT_WIDTH, cache_len), lambda n: (n, 0, 0))
    return pl.pallas_call(
        _sample_attn_kernel,
        out_shape=jax.ShapeDtypeStruct((nb * steps, ATT_WIDTH), F32),
        grid=(nb,),
        in_specs=[tok, tok, tok, cache, cache],
        out_specs=tok,
        compiler_params=_cparams(("arbitrary",)),
        name="sample_attn",
    )(q, k_new, v_new, cache_kt, cache_vt)


def _outproj_kernel(att_ref, gla_ref, x_ref, gate_ref, ng_ref, w_ref, g2_ref, sc2_ref, sh2_ref,
                    h_ref, hn_ref):
    att = jnp.concatenate([att_ref[hp].astype(F32) for hp in range(HEAD_PAIRS)], axis=1)
    rs = lax.rsqrt(jnp.mean(att * att, axis=-1, keepdims=True) + EPS)
    att_y = (att * rs * ng_ref[...]).astype(BF16)
    mix = _dot(att_y, w_ref[:ATT_WIDTH, :]) + _dot(gla_ref[...], w_ref[ATT_WIDTH:, :])
    rows = x_ref.shape[0]
    h = x_ref[...] + _mod_rows(gate_ref, rows) * mix
    h_ref[...] = h
    hn_ref[...] = _norm_mod(h, g2_ref[...], _mod_rows(sc2_ref, rows),
                            _mod_rows(sh2_ref, rows)).astype(BF16)


def _outproj(att, gla_y, x, gate, att_norm_g, w_out, g2, sc2, sh2, *, tm):
    t = x.shape[0]
    mod_rows = gate.shape[0]
    mod_map = (lambda i: (0, 0)) if mod_rows == 1 else (lambda i: (i, 0))
    mod = pl.BlockSpec((1 if mod_rows == 1 else mod_rows * tm // t, D_MODEL), mod_map)
    row = pl.BlockSpec((tm, D_MODEL), lambda i: (i, 0))
    return pl.pallas_call(
        _outproj_kernel,
        out_shape=(jax.ShapeDtypeStruct((t, D_MODEL), F32),
                   jax.ShapeDtypeStruct((t, D_MODEL), BF16)),
        grid=(t // tm,),
        in_specs=[
            pl.BlockSpec((HEAD_PAIRS, tm, LANES), lambda i: (0, i, 0)),
            pl.BlockSpec((tm, GLA_WIDTH), lambda i: (i, 0)),
            row, mod,
            pl.BlockSpec((1, ATT_WIDTH), lambda i: (0, 0)),
            pl.BlockSpec((ATT_WIDTH + GLA_WIDTH, D_MODEL), lambda i: (0, 0)),
            pl.BlockSpec((1, D_MODEL), lambda i: (0, 0)),
            mod, mod,
        ],
        out_specs=(row, row),
        compiler_params=_cparams(("arbitrary",)),
        name="outproj",
    )(att, gla_y, x, gate, att_norm_g, w_out, g2, sc2, sh2)


def _ffn_kernel(h_hbm, hn_ref, gate_ref, fg_ref, wu_ref, wd_ref, y_ref, h_ref, h_sem):
    i = pl.program_id(0)
    f = pl.program_id(1)
    last = pl.num_programs(1) - 1
    tm = hn_ref.shape[0]

    def residual_copy():
        rows = pl.ds(pl.multiple_of(i * tm, tm), tm)
        return pltpu.make_async_copy(h_hbm.at[rows, :], h_ref, h_sem)

    def hidden():
        u = jnp.maximum(_dot(hn_ref[...], wu_ref[...]), 0.0)
        return (u * u).astype(BF16)

    @pl.when(f == 0)
    def _():
        residual_copy().start()
        y_ref[...] = _dot(hidden(), wd_ref[...])

    @pl.when(jnp.logical_and(f > 0, f < last))
    def _():
        y_ref[...] += _dot(hidden(), wd_ref[...])

    @pl.when(f == last)
    def _():
        u2 = hidden()
        residual_copy().wait()
        width = D_MODEL // 4
        ss = jnp.zeros((h_ref.shape[0], 1), F32)
        gate = _mod_rows(gate_ref, h_ref.shape[0])
        for c in range(0, D_MODEL, width):
            cols = slice(c, c + width)
            y = y_ref[:, cols] + _dot(u2, wd_ref[:, cols])
            h2 = h_ref[:, cols] + gate[:, cols] * y
            y_ref[:, cols] = h2
            ss = ss + jnp.sum(h2 * h2, axis=-1, keepdims=True)
        y_ref[...] = y_ref[...] * lax.rsqrt(ss / D_MODEL + EPS) * fg_ref[...]


def _ffn(h, hn, gate, final_g, w_up, w_down, *, tm, tf):
    t = h.shape[0]
    assert D_FF // tf >= 2, "first and last hidden chunks are separate code paths"
    row = pl.BlockSpec((tm, D_MODEL), lambda i, f: (i, 0))
    return pl.pallas_call(
        _ffn_kernel,
        out_shape=jax.ShapeDtypeStruct((t, D_MODEL), F32),
        grid=(t // tm, D_FF // tf),
        in_specs=[
            pl.BlockSpec(memory_space=pl.ANY),
            row,
            _mod_spec(gate.shape[0], tm, t),
            pl.BlockSpec((1, D_MODEL), lambda i, f: (0, 0)),
            pl.BlockSpec((D_MODEL, tf), lambda i, f: (0, f)),
            pl.BlockSpec((tf, D_MODEL), lambda i, f: (f, 0)),
        ],
        out_specs=row,
        scratch_shapes=[pltpu.VMEM((tm, D_MODEL), F32), pltpu.SemaphoreType.DMA(())],
        compiler_params=_cparams(("arbitrary", "arbitrary")),
        name="ffn",
    )(h, hn, gate, final_g, w_up, w_down)


def _split_ada(ada):
    return [ada[:, D_MODEL * i:D_MODEL * (i + 1)] for i in range(6)]


def kernel(x_prompt, x_sample, c_prompt, c_sample, cache_win_k, cache_win_v, state_gla, w_in, gla_gate_w, gla_gate_b, att_norm_g, gla_norm_g, w_out, norm1_g, norm2_g, w_ada, b_ada, w_up, w_down, final_g):
    assert w_in.shape[0] == 1, "single-layer model"
    n_p, t_p, _ = x_prompt.shape
    n_s, t_s, _ = x_sample.shape
    assert n_p == 1
    rows_s = n_s * t_s
    wb_p = min(MAX_WINDOW, t_p)

    w_main = jnp.transpose(w_in[0]).astype(BF16)
    gate_w = gla_gate_w[0].astype(BF16)
    gate_b = gla_gate_b[0][None, :]
    n1g, n2g = norm1_g[0][None, :], norm2_g[0][None, :]
    ang, gng = att_norm_g[0][None, :], gla_norm_g[0][None, :]
    fg = final_g[None, :]

    c_rows = n_p + n_s
    c_pad = -c_rows % 8
    c_all = jnp.concatenate([c_prompt, c_sample, jnp.zeros((c_pad, D_MODEL), F32)], axis=0)
    ada = _ada(c_all, w_ada[0], b_ada[0][None, :])
    sh1_p, sc1_p, g1_p, sh2_p, sc2_p, g2_p = _split_ada(ada[:n_p])
    sh1_s, sc1_s, g1_s, sh2_s, sc2_s, g2_s = _split_ada(ada[n_p:c_rows])

    xp = x_prompt[0]
    cs_p, sn_p = _rope_tables(jnp.arange(t_p))
    outs = _inproj(xp, n1g, sc1_p, sh1_p, w_main, gate_w, gate_b, cs_p, sn_p,
                   tm=1024, emit_att=True, tail_rows=wb_p)
    att, w_out_b, w_up_b, w_down_b = _prompt_attention(outs[:3], (w_out[0], w_up[0], w_down[0]))
    gla_in, log_a = outs[3], outs[4]
    win_k, win_v = outs[5][0], outs[5][1]
    gla_y, st_p = _gla_prompt(gla_in, log_a, gng, tb=512, chunk=64, sub=16)
    h_p, hn_p = _outproj(att, gla_y, xp, g1_p, ang, w_out_b, n2g, sc2_p, sh2_p, tm=512)
    y_p = _ffn(h_p, hn_p, g2_p, fg, w_up_b, w_down_b, tm=1024, tf=1024)

    xs = x_sample.reshape(rows_s, D_MODEL)
    cs_s, sn_s = _rope_tables(jnp.tile(PAST_LEN + jnp.arange(t_s), n_s))
    gla_in_s, log_a_s = _inproj(xs, n1g, sc1_s, sh1_s, w_main, gate_w, gate_b, cs_s, sn_s,
                                tm=rows_s, emit_att=False)
    qkv_s = _qkv_rows(xs, n1g, sc1_s, sh1_s, w_main, cs_s, sn_s,
                      tm=rows_s, first_row=0, rows=rows_s, col0=0)
    new_k, new_v = qkv_s[1], qkv_s[2]
    wc = cache_win_k.shape[2]
    cache_t = lambda c: jnp.transpose(c[0], (0, 2, 3, 1)).reshape(n_s, ATT_WIDTH, wc)
    att_tok = _sample_attention(qkv_s[0], new_k, new_v, cache_t(cache_win_k), cache_t(cache_win_v),
                                steps=t_s)
    att_s = jnp.transpose(att_tok.reshape(rows_s, HEAD_PAIRS, LANES), (1, 0, 2)).astype(BF16)
    gla_y_s, st_s = _gla_sample(gla_in_s.astype(F32), log_a_s, gng, state_gla[0], steps=t_s,
                                group=4 if n_s % 4 == 0 else 1)
    h_s, hn_s = _outproj(att_s, gla_y_s.astype(BF16), xs, g1_s, ang, w_out_b, n2g, sc2_s, sh2_s,
                         tm=rows_s)
    y_s = _ffn(h_s, hn_s, g2_s, fg, w_up_b, w_down_b, tm=rows_s, tf=2048)

    return (
        y_p[None],
        y_s.reshape(n_s, t_s, D_MODEL),
        win_k.reshape(1, n_p, wb_p, ATT_HEADS, ATT_HEAD_DIM),
        win_v.reshape(1, n_p, wb_p, ATT_HEADS, ATT_HEAD_DIM),
        st_p[None, None],
        new_k.reshape(1, n_s, t_s, ATT_HEADS, ATT_HEAD_DIM),
        new_v.reshape(1, n_s, t_s, ATT_HEADS, ATT_HEAD_DIM),
        st_s[None],
    )
```

```python
import functools

import jax
import jax.numpy as jnp
from jax import lax
from jax.experimental import pallas as pl
from jax.experimental.pallas import tpu as pltpu

F32 = jnp.float32
BF16 = jnp.bfloat16

D_MODEL = 2048
ATT_WIDTH = 1024
ATT_HEADS = 16
ATT_HEAD_DIM = 64
ROT_DIM = 16
ROPE_THETA = 500000.0
DILATIONS = (1, 4, 16)
BAND = 128
MAX_WINDOW = 2048
GLA_HEADS = 4
GLA_DK = 128
GLA_DV = 256
GLA_WIDTH = 1024
GLA_GATE_RANK = 16
GLA_TAU = 16.0
D_FF = 8192
EPS = 1e-6
PAST_LEN = 16384
NEG = -1e30

LANES = 128
HEAD_PAIRS = ATT_WIDTH // LANES
VMEM_LIMIT = 62 * 1024 * 1024


def _cparams(sem):
    return pltpu.CompilerParams(dimension_semantics=sem, vmem_limit_bytes=VMEM_LIMIT)


def _dot(a, b):
    return jnp.dot(a, b, preferred_element_type=F32)


def _dot_nt(a, b):
    return lax.dot_general(a, b, (((1,), (1,)), ((), ())), preferred_element_type=F32)


def _dot_tn(a, b):
    return lax.dot_general(a, b, (((0,), (0,)), ((), ())), preferred_element_type=F32)


def _silu(x):
    return x / (1.0 + jnp.exp(-x))


def _ada_kernel(c_ref, w_ref, b_ref, o_ref):
    s = _silu(c_ref[...]).astype(BF16)
    o_ref[...] = _dot(s, w_ref[...].astype(BF16)) + b_ref[...]


def _ada(c, w_ada, b_ada):
    rows = c.shape[0]
    n = w_ada.shape[1]
    tn = 1024
    return pl.pallas_call(
        _ada_kernel,
        out_shape=jax.ShapeDtypeStruct((rows, n), F32),
        grid=(n // tn,),
        in_specs=[
            pl.BlockSpec((rows, D_MODEL), lambda j: (0, 0)),
            pl.BlockSpec((D_MODEL, tn), lambda j: (0, j)),
            pl.BlockSpec((1, tn), lambda j: (0, j)),
        ],
        out_specs=pl.BlockSpec((rows, tn), lambda j: (0, j)),
        compiler_params=_cparams(("arbitrary",)),
        name="ada",
    )(c, w_ada, b_ada)


def _rope(slab, cs, sn):
    d = lax.broadcasted_iota(jnp.int32, (1, LANES), 1) % ATT_HEAD_DIM
    half = ROT_DIM // 2
    from_hi = jnp.where(d < half, -1.0, 0.0)
    from_lo = jnp.where((d >= half) & (d < ROT_DIM), 1.0, 0.0)
    return (slab * cs + pltpu.roll(slab, LANES - half, 1) * (sn * from_hi)
            + pltpu.roll(slab, half, 1) * (sn * from_lo))


def _norm_mod(x, g, sc, sh):
    rs = lax.rsqrt(jnp.mean(x * x, axis=-1, keepdims=True) + EPS)
    return x * rs * g * (1.0 + sc) + sh


def _mod_rows(ref, rows):
    m = ref[...]
    if m.shape[0] in (1, rows):
        return m
    run = rows // m.shape[0]
    return jnp.broadcast_to(m[:, None, :], (m.shape[0], run, m.shape[1])).reshape(rows, m.shape[1])


def _inproj_kernel(*refs, emit_att, first_tail):
    (x_ref, g_ref, sc_ref, sh_ref, w_ref, wag_ref, gw_ref, gb_ref, cs_ref, sn_ref), refs = (
        refs[:10], refs[10:])
    i = pl.program_id(0)
    j = pl.program_id(1)
    if emit_att:
        (out1, out4, out16), refs = refs[:3], refs[3:]
        gla_ref, la_ref, tail_ref, xn_ref, s0_ref, s1_ref, x_buf, x_sem, w_buf, w_sem = refs
        tm = x_buf.shape[0]
        x_hbm, x_ref = x_ref, x_buf
        w_hbm = w_ref

        ncol, nslot = pl.num_programs(1), w_buf.shape[0]
        step = i * ncol + j

        def w_copy(s):
            rows = pl.ds(pl.multiple_of((s % ncol) * w_buf.shape[1], w_buf.shape[1]), w_buf.shape[1])
            slot = s % nslot
            return pltpu.make_async_copy(w_hbm.at[rows, :], w_buf.at[slot], w_sem.at[slot])

        @pl.when(step == 0)
        def _():
            w_copy(0).start()
            w_copy(1).start()

        @pl.when(step + 2 < pl.num_programs(0) * ncol)
        def _():
            w_copy(step + 2).start()

        w_copy(step).wait()

        def x_copy(tile):
            rows = pl.ds(pl.multiple_of(tile * tm, tm), tm)
            return pltpu.make_async_copy(x_hbm.at[rows, :], x_buf, x_sem)

        @pl.when(jnp.logical_and(i == 0, j == 0))
        def _():
            x_copy(0).start()

        @pl.when(jnp.logical_and(j == 1, i + 1 < pl.num_programs(0)))
        def _():
            x_copy(i + 1).start()
    else:
        gla_ref, la_ref, xn_ref = refs[:3]
    col = j if emit_att else j + 3
    weights = (lambda c: w_buf.at[c % 3]) if emit_att else (lambda c: w_ref)

    def normalise():
        if emit_att:
            x_copy(i).wait()
        rows = x_ref.shape[0]
        xnb = _norm_mod(x_ref[...], g_ref[...], _mod_rows(sc_ref, rows),
                        _mod_rows(sh_ref, rows)).astype(BF16)
        xn_ref[...] = xnb
        ag = _dot_nt(xnb, wag_ref[...])
        z = _dot(ag.astype(BF16), gw_ref[...]) + gb_ref[...]
        log_sig = jnp.minimum(z, 0.0) - jnp.log(1.0 + jnp.exp(-jnp.abs(z)))
        la_ref[...] = log_sig / GLA_TAU

    if emit_att:
        n4, n16 = tm // 4, tm // 16
        group = 2 * LANES
        in_tail = i >= first_tail

        def emit(slab_fn, c, keep_f32=False):
            w_c = weights(c)
            for hp in range(HEAD_PAIRS):
                half = hp % 2
                if half == 0:
                    acc = _dot_nt(xn_ref[...], w_c[LANES * hp:LANES * hp + group, :])
                slab = slab_fn(acc[:, LANES * half:LANES * (half + 1)])
                out1[0, hp] = slab.astype(BF16)
                if keep_f32:
                    tail_ref[0, :, LANES * hp:LANES * (hp + 1)] = slab
                s0_ref[half] = slab
                for r in range(4):
                    part = s0_ref[half, pl.ds(r, n4, stride=4), :]
                    out4[0, hp, :, LANES * r:LANES * (r + 1)] = part.astype(BF16)
                    s1_ref[half, n4 * r:n4 * (r + 1), :] = part
                for r in range(16):
                    part = s1_ref[half, pl.ds(n4 * (r % 4) + r // 4, n16, stride=4), :]
                    out16[0, hp, :, LANES * r:LANES * (r + 1)] = part.astype(BF16)

        @pl.when(col == 0)
        def _():
            normalise()
            emit(lambda s: _rope(s, cs_ref[...], sn_ref[...]) * (ATT_HEAD_DIM ** -0.5), 0)

        for c, slab_fn in ((1, lambda s: _rope(s, cs_ref[...], sn_ref[...])), (2, lambda s: s)):
            pl.when(jnp.logical_and(col == c, jnp.logical_not(in_tail)))(
                functools.partial(emit, slab_fn, c))
            pl.when(jnp.logical_and(col == c, in_tail))(
                functools.partial(emit, slab_fn, c, keep_f32=True))
    else:
        pl.when(j == 0)(normalise)

    @pl.when(col == 3)
    def _():
        acc = _dot_nt(xn_ref[...], weights(3)[...])
        half = GLA_HEADS * GLA_DK
        gla_ref[:, :half] = (acc[:, :half] * (GLA_DK ** -0.5)).astype(BF16)
        gla_ref[:, half:] = acc[:, half:].astype(BF16)

    for c in (4, 5):
        @pl.when(col == c)
        def _():
            gla_ref[...] = _dot_nt(xn_ref[...], weights(c)[...]).astype(BF16)


def _mod_spec(mod_rows, tm, t):
    if mod_rows == 1:
        return pl.BlockSpec((1, D_MODEL), lambda i, j: (0, 0))
    return pl.BlockSpec((mod_rows * tm // t, D_MODEL), lambda i, j: (i, 0))


def _inproj(x, g, sc, sh, w_all, gate_w, gate_b, cs, sn, *, tm, emit_att, tail_rows=0):
    t = x.shape[0]
    col0 = 0 if emit_att else 3
    first_tail = (t - tail_rows) // tm
    mod = _mod_spec(sc.shape[0], tm, t)
    const = lambda block: pl.BlockSpec(block, lambda i, j: (0, 0))
    out_shape, out_specs, scratch = [], [], [pltpu.VMEM((tm, D_MODEL), BF16)]
    if emit_att:
        for d in DILATIONS:
            out_shape.append(jax.ShapeDtypeStruct((3, HEAD_PAIRS, t // d, d * LANES), BF16))
            out_specs.append(pl.BlockSpec((1, HEAD_PAIRS, tm // d, d * LANES),
                                          lambda i, j: (jnp.minimum(j, 2), 0, i, 0)))
        scratch += [pltpu.VMEM((2, tm, LANES), F32) for _ in range(2)]
        scratch += [pltpu.VMEM((tm, D_MODEL), F32), pltpu.SemaphoreType.DMA(())]
        scratch += [pltpu.VMEM((3, 1024, D_MODEL), BF16), pltpu.SemaphoreType.DMA((3,))]
    out_shape += [jax.ShapeDtypeStruct((t, 3 * 1024), BF16),
                  jax.ShapeDtypeStruct((t, GLA_HEADS * GLA_DK), F32)]
    out_specs += [pl.BlockSpec((tm, 1024), lambda i, j: (i, jnp.maximum(j + col0 - 3, 0))),
                  pl.BlockSpec((tm, GLA_HEADS * GLA_DK), lambda i, j: (i, 0))]
    if emit_att:
        assert tail_rows % tm == 0 and 0 < tail_rows <= t
        out_shape.append(jax.ShapeDtypeStruct((2, tail_rows, ATT_WIDTH), F32))
        out_specs.append(pl.BlockSpec(
            (1, tm, ATT_WIDTH),
            lambda i, j: (jnp.where(i < first_tail, 0, jnp.clip(j - 1, 0, 1)),
                          jnp.maximum(i - first_tail, 0), 0)))
    gate_rows = (6 * 1024) // GLA_GATE_RANK
    return pl.pallas_call(
        functools.partial(_inproj_kernel, emit_att=emit_att, first_tail=first_tail),
        out_shape=tuple(out_shape),
        grid=(t // tm, 6 - col0),
        in_specs=[
            (pl.BlockSpec(memory_space=pl.ANY) if emit_att
             else pl.BlockSpec((tm, D_MODEL), lambda i, j: (i, 0))),
            const((1, D_MODEL)),
            mod, mod,
            (pl.BlockSpec(memory_space=pl.ANY) if emit_att
             else pl.BlockSpec((1024, D_MODEL), lambda i, j: (j + col0, 0))),
            pl.BlockSpec((GLA_GATE_RANK, D_MODEL), lambda i, j: (gate_rows, 0)),
            const((GLA_GATE_RANK, GLA_HEADS * GLA_DK)),
            const((1, GLA_HEADS * GLA_DK)),
            pl.BlockSpec((tm, LANES), lambda i, j: (i, 0)),
            pl.BlockSpec((tm, LANES), lambda i, j: (i, 0)),
        ],
        out_specs=tuple(out_specs),
        scratch_shapes=scratch,
        compiler_params=_cparams(("arbitrary", "arbitrary")),
        name="inproj_att" if emit_att else "inproj_gla",
    )(x, g, sc, sh, w_all, w_all, gate_w, gate_b, cs, sn)


def _qkv_rows_kernel(x_ref, g_ref, sc_ref, sh_ref, w_ref, cs_ref, sn_ref, o_ref, xn_ref, *, col0):
    j = pl.program_id(1)

    @pl.when(j == 0)
    def _():
        xn_ref[...] = _norm_mod(x_ref[...], g_ref[...], _mod_rows(sc_ref, x_ref.shape[0]),
                                _mod_rows(sh_ref, x_ref.shape[0])).astype(BF16)

    acc = _dot_nt(xn_ref[...], w_ref[...])
    col = j + col0

    @pl.when(col <= 1)
    def _():
        scale = jnp.where(col == 0, ATT_HEAD_DIM ** -0.5, 1.0)
        for hp in range(HEAD_PAIRS):
            lanes = slice(LANES * hp, LANES * (hp + 1))
            o_ref[0, :, lanes] = _rope(acc[:, lanes], cs_ref[...], sn_ref[...]) * scale

    @pl.when(col == 2)
    def _():
        o_ref[0] = acc


def _qkv_rows(x, g, sc, sh, w_main, cs, sn, *, tm, first_row, rows, col0):
    first = first_row // tm
    assert sc.shape[0] == 1 or first == 0
    ncols = 3 - col0
    mod = _mod_spec(sc.shape[0], tm, rows)
    return pl.pallas_call(
        functools.partial(_qkv_rows_kernel, col0=col0),
        out_shape=jax.ShapeDtypeStruct((ncols, rows, ATT_WIDTH), F32),
        grid=(rows // tm, ncols),
        in_specs=[
            pl.BlockSpec((tm, D_MODEL), lambda i, j: (i + first, 0)),
            pl.BlockSpec((1, D_MODEL), lambda i, j: (0, 0)),
            mod, mod,
            pl.BlockSpec((1024, D_MODEL), lambda i, j: (j + col0, 0)),
            pl.BlockSpec((tm, LANES), lambda i, j: (i + first, 0)),
            pl.BlockSpec((tm, LANES), lambda i, j: (i + first, 0)),
        ],
        out_specs=pl.BlockSpec((1, tm, ATT_WIDTH), lambda i, j: (j, i, 0)),
        scratch_shapes=[pltpu.VMEM((tm, D_MODEL), BF16)],
        compiler_params=_cparams(("arbitrary", "arbitrary")),
        name="qkv_rows",
    )(x, g, sc, sh, w_main, cs, sn)


def _rope_tables(pos):
    half = ROT_DIM // 2
    inv_freq = ROPE_THETA ** (-jnp.arange(half, dtype=F32) / half)
    per_head = jnp.concatenate([inv_freq, inv_freq, jnp.zeros((ATT_HEAD_DIM - ROT_DIM,), F32)])
    ang = pos.astype(F32)[:, None] * jnp.tile(per_head, LANES // ATT_HEAD_DIM)[None, :]
    return jnp.cos(ang), jnp.sin(ang)


def _attn_kernel(*refs, n_cast):
    (q1, k1c, v1c, k1p, v1p, q4, k4c, v4c, k4p, v4p, q16, k16c, v16c, k16p, v16p), refs = (
        refs[:15], refs[15:])
    cast_in, o_ref, cast_out = refs[:n_cast], refs[n_cast], refs[n_cast + 1:2 * n_cast + 1]
    a1, m1, l1, a4, m4, l4, t16, s_scr, p_scr, bias_ref, o_scr = refs[2 * n_cast + 1:]
    for src, dst in zip(cast_in, cast_out):
        dst[...] = src[...].astype(dst.dtype)

    first_block = pl.program_id(1) == 0
    lane = lax.broadcasted_iota(jnp.int32, (1, LANES), 1)
    first = lane < ATT_HEAD_DIM
    head0 = jnp.where(first, 1.0, 0.0).astype(BF16)
    head1 = jnp.where(first, 0.0, 1.0).astype(BF16)
    ones = jnp.ones((2 * BAND, LANES), BF16)

    qi = lax.broadcasted_iota(jnp.int32, (2 * BAND, 2 * BAND), 0) % BAND
    kk = lax.broadcasted_iota(jnp.int32, (2 * BAND, 2 * BAND), 1)
    full = jnp.where(jnp.where(kk < BAND, kk - qi, qi - (kk - BAND)) >= 0, 0.0, NEG)
    bias_ref[0] = full
    bias_ref[1] = jnp.where(kk < BAND, full + jnp.where(first_block, NEG, 0.0), full)

    n1 = q1.shape[1] // BAND
    rows4 = q4.shape[1]
    n4 = rows4 // BAND

    def window(cur, prev, i, cols):
        if i == 0:
            return jnp.concatenate([prev[0, :, cols], cur[0, :BAND, cols]], axis=0)
        return cur[0, BAND * (i - 1):BAND * (i + 1), cols]

    jobs = []
    everything = slice(None)
    for i in range(n1):
        jobs.append(dict(q=(q1, i, everything), k=(k1c, k1p), v=(v1c, v1p), bias=int(i == 0),
                         dst=(a1, m1, l1, BAND * i)))
    for r in range(4):
        cols = slice(LANES * r, LANES * (r + 1))
        for i in range(n4):
            jobs.append(dict(q=(q4, i, cols), k=(k4c, k4p), v=(v4c, v4p), bias=int(i == 0),
                             dst=(a4, m4, l4, rows4 * r + BAND * i)))
    for r in range(16):
        cols = slice(LANES * r, LANES * (r + 1))
        jobs.append(dict(q=(q16, 0, cols), k=(k16c, k16p), v=(v16c, v16p), bias=1, merge=r))

    def scores(job, slot):
        ref, i, cols = job["q"]
        q = ref[0, BAND * i:BAND * (i + 1), cols]
        q2 = jnp.concatenate([q * head0, q * head1], axis=0)
        s_scr[slot] = _dot_nt(q2, window(*job["k"], i, cols))

    def softmax(job, slot):
        s = s_scr[slot] + bias_ref[job["bias"]]
        m = jnp.max(s, axis=-1, keepdims=True)
        p_scr[slot] = jnp.exp(s - m).astype(BF16)
        m = jnp.where(first, m[:BAND], m[BAND:])
        if "merge" in job:
            t16[slot] = m
        else:
            job["dst"][1][pl.ds(job["dst"][3], BAND), :] = m

    def values(job, slot):
        _, i, cols = job["q"]
        vw = jnp.concatenate([window(*job["v"], i, cols), ones], axis=1)
        o2 = _dot(p_scr[slot], vw)
        acc = jnp.where(first, o2[:BAND, :LANES], o2[BAND:, :LANES])
        den = jnp.where(first, o2[:BAND, LANES:], o2[BAND:, LANES:])
        if "merge" not in job:
            a_ref, _, l_ref, row = job["dst"]
            a_ref[pl.ds(row, BAND), :] = acc
            l_ref[pl.ds(row, BAND), :] = den
            return
        r = job["merge"]
        sl1 = pl.ds(r, BAND, stride=16)
        sl4 = pl.ds((r % 4) * rows4 + r // 4, BAND, stride=4)
        ma, mb, mc = m1[sl1, :], m4[sl4, :], t16[slot]
        m = jnp.maximum(jnp.maximum(ma, mb), mc)
        wa, wb, wc = jnp.exp(ma - m), jnp.exp(mb - m), jnp.exp(mc - m)
        num = wa * a1[sl1, :] + wb * a4[sl4, :] + wc * acc
        dsum = wa * l1[sl1, :] + wb * l4[sl4, :] + wc * den
        o_scr[sl1, :] = num / dsum

    slots = s_scr.shape[0]
    for step in range(len(jobs) + 2):
        if step < len(jobs):
            scores(jobs[step], step % slots)
        if 1 <= step <= len(jobs):
            softmax(jobs[step - 1], (step - 1) % slots)
        if step >= 2:
            values(jobs[step - 2], (step - 2) % slots)
    o_ref[0] = o_scr[...].astype(o_ref.dtype)


def _prompt_attention(views, weights):
    _, hp, t, _ = views[0].shape
    sb = 16 * BAND
    ns = t // sb
    in_specs, args = [], []
    for arr, d in zip(views, DILATIONS):
        rows, width = sb // d, d * LANES
        nb = rows // BAND

        def cur(c, rows=rows, width=width):
            return pl.BlockSpec((None, 1, rows, width), lambda h, s: (c, h, s, 0))

        def prev(c, width=width, nb=nb):
            return pl.BlockSpec((None, 1, BAND, width),
                                lambda h, s: (c, h, jnp.maximum(s * nb - 1, 0), 0))

        in_specs += [cur(0), cur(1), cur(2), prev(1), prev(2)]
        args += [arr] * 5
    cast_specs = []
    for w in weights:
        slab = w.shape[0] // (hp * ns)
        assert slab * hp * ns == w.shape[0] and slab % 16 == 0
        cast_specs.append(pl.BlockSpec((slab, w.shape[1]), lambda h, s: (h * ns + s, 0)))
    tile = (2 * BAND, 2 * BAND)
    return pl.pallas_call(
        functools.partial(_attn_kernel, n_cast=len(weights)),
        out_shape=tuple([jax.ShapeDtypeStruct((hp, t, LANES), BF16)]
                        + [jax.ShapeDtypeStruct(w.shape, BF16) for w in weights]),
        grid=(hp, ns),
        in_specs=in_specs + cast_specs,
        out_specs=tuple([pl.BlockSpec((1, sb, LANES), lambda h, s: (h, s, 0))] + cast_specs),
        scratch_shapes=(
            [pltpu.VMEM((sb, LANES), F32) for _ in range(6)]
            + [pltpu.VMEM((3, BAND, LANES), F32),
               pltpu.VMEM((3,) + tile, F32),
               pltpu.VMEM((3,) + tile, BF16),
               pltpu.VMEM((2,) + tile, F32),
               pltpu.VMEM((sb, LANES), F32)]),
        compiler_params=_cparams(("arbitrary", "arbitrary")),
        name="prompt_attn",
    )(*args, *weights)


def _gla_chunk(q, k, v, la, st_ref, h, tri, sub):
    c = q.shape[0]
    la_hi = la.astype(BF16)
    la_lo = (la - la_hi.astype(F32)).astype(BF16)
    b = _dot(tri, la_hi) + _dot(tri, la_lo)
    st = st_ref[h]
    o = _dot_nt((q * jnp.exp(b)).astype(BF16), st.astype(BF16))
    srow = lax.broadcasted_iota(jnp.int32, (c, GLA_DK), 0)
    att_rows = []
    for i in range(c // sub):
        lo, hi = i * sub, (i + 1) * sub
        ref = b[lo:lo + 1, :]
        qh = (q[lo:hi] * jnp.exp(b[lo:hi] - ref)).astype(BF16)
        kh = (k * jnp.exp(jnp.where(srow < hi, ref - b, 0.0))).astype(BF16)
        a = _dot_nt(qh, kh)
        ti = lax.broadcasted_iota(jnp.int32, (sub, c), 0) + lo
        si = lax.broadcasted_iota(jnp.int32, (sub, c), 1)
        att_rows.append(jnp.where(si <= ti, a, 0.0))
    att = att_rows[0] if len(att_rows) == 1 else jnp.concatenate(att_rows, axis=0)
    o = o + _dot(att.astype(BF16), v)
    b_last = b[c - 1:c, :]
    kbar = (k * jnp.exp(b_last - b)).astype(BF16)
    st_ref[h] = st * jnp.exp(b_last) + _dot_tn(v, kbar)
    return o


def _gla_out(o, g, gg):
    rs = lax.rsqrt(jnp.mean(o * o, axis=-1, keepdims=True) + EPS)
    return o * rs * g * _silu(gg)


def _tri(c):
    r = lax.broadcasted_iota(jnp.int32, (c, c), 0)
    s = lax.broadcasted_iota(jnp.int32, (c, c), 1)
    return jnp.where(s <= r, 1.0, 0.0).astype(BF16)


def _gla_prompt_kernel(qk_ref, v_ref, gg_ref, la_ref, g_ref, y_ref, s_ref, st_ref,
                       qd_ref, kl_ref, qs_ref, ks_ref, dec_ref, att_ref, o_ref, *, chunk, sub):
    i = pl.program_id(0)

    @pl.when(i == 0)
    def _():
        st_ref[...] = jnp.zeros_like(st_ref)

    tb = qk_ref.shape[0]
    nchunks, nsub = tb // chunk, chunk // sub
    half = GLA_HEADS * GLA_DK

    r = lax.broadcasted_iota(jnp.int32, (tb, tb), 0)
    c = lax.broadcasted_iota(jnp.int32, (tb, tb), 1)
    start = r - jnp.bitwise_and(r, chunk - 1)
    tri = jnp.where(c <= r, jnp.where(c >= start, 1.0, 0.0), 0.0).astype(BF16)
    la = la_ref[...]
    la_hi = la.astype(BF16)
    la_lo = (la - la_hi.astype(F32)).astype(BF16)
    b = _dot(tri, la_hi) + _dot(tri, la_lo)
    q = qk_ref[:, :half].astype(F32)
    k = qk_ref[:, half:].astype(F32)
    qd_ref[...] = (q * jnp.exp(b)).astype(BF16)
    srow = lax.broadcasted_iota(jnp.int32, (chunk, half), 0)
    for ci in range(nchunks):
        lo = ci * chunk
        bc, kc_ = b[lo:lo + chunk], k[lo:lo + chunk]
        b_last = bc[chunk - 1:chunk]
        dec_ref[ci:ci + 1, :] = jnp.exp(b_last)
        kl_ref[lo:lo + chunk, :] = (kc_ * jnp.exp(b_last - bc)).astype(BF16)
        for si in range(nsub):
            s_lo, s_hi = si * sub, (si + 1) * sub
            ref = bc[s_lo:s_lo + 1]
            qs_ref[lo + s_lo:lo + s_hi, :] = (
                q[lo + s_lo:lo + s_hi] * jnp.exp(bc[s_lo:s_hi] - ref)).astype(BF16)
            ks_ref[si, lo:lo + chunk, :] = (
                kc_ * jnp.exp(jnp.where(srow < s_hi, ref - bc, 0.0))).astype(BF16)

    ti = lax.broadcasted_iota(jnp.int32, (chunk, chunk), 0)
    si_ = lax.broadcasted_iota(jnp.int32, (chunk, chunk), 1)
    causal = si_ <= ti
    jobs = [(ci, h) for ci in range(nchunks) for h in range(GLA_HEADS)]

    def intra(job, slot):
        ci, h = job
        lo, kc = ci * chunk, slice(GLA_DK * h, GLA_DK * (h + 1))
        parts = [_dot_nt(qs_ref[lo + s * sub:lo + (s + 1) * sub, kc], ks_ref[s, lo:lo + chunk, kc])
                 for s in range(nsub)]
        att_ref[slot] = jnp.where(causal, jnp.concatenate(parts, axis=0), 0.0).astype(BF16)

    def state(job, slot):
        ci, h = job
        rows = slice(ci * chunk, (ci + 1) * chunk)
        kc, vc = slice(GLA_DK * h, GLA_DK * (h + 1)), slice(GLA_DV * h, GLA_DV * (h + 1))
        st = st_ref[h]
        v = v_ref[rows, vc]
        o_ref[slot] = _dot_nt(qd_ref[rows, kc], st.astype(BF16)) + _dot(att_ref[slot], v)
        st_ref[h] = st * dec_ref[ci:ci + 1, kc] + _dot_tn(v, kl_ref[rows, kc])

    def finish(job, slot):
        ci, h = job
        rows, vc = slice(ci * chunk, (ci + 1) * chunk), slice(GLA_DV * h, GLA_DV * (h + 1))
        y_ref[rows, vc] = _gla_out(o_ref[slot], g_ref[:, vc],
                                   gg_ref[rows, vc].astype(F32)).astype(y_ref.dtype)

    slots = att_ref.shape[0]
    for step in range(len(jobs) + 2):
        if step < len(jobs):
            intra(jobs[step], step % slots)
        if 1 <= step <= len(jobs):
            state(jobs[step - 1], (step - 1) % slots)
        if step >= 2:
            finish(jobs[step - 2], (step - 2) % slots)

    @pl.when(i == pl.num_programs(0) - 1)
    def _():
        for h in range(GLA_HEADS):
            s_ref[h] = st_ref[h].T


def _gla_prompt(gla_in, log_a, norm_g, *, tb, chunk, sub):
    t = gla_in.shape[0]
    return pl.pallas_call(
        functools.partial(_gla_prompt_kernel, chunk=chunk, sub=sub),
        out_shape=(
            jax.ShapeDtypeStruct((t, GLA_WIDTH), BF16),
            jax.ShapeDtypeStruct((GLA_HEADS, GLA_DK, GLA_DV), F32),
        ),
        grid=(t // tb,),
        in_specs=[
            pl.BlockSpec((tb, 1024), lambda i: (i, 0)),
            pl.BlockSpec((tb, 1024), lambda i: (i, 1)),
            pl.BlockSpec((tb, 1024), lambda i: (i, 2)),
            pl.BlockSpec((tb, GLA_HEADS * GLA_DK), lambda i: (i, 0)),
            pl.BlockSpec((1, GLA_WIDTH), lambda i: (0, 0)),
        ],
        out_specs=(
            pl.BlockSpec((tb, GLA_WIDTH), lambda i: (i, 0)),
            pl.BlockSpec((GLA_HEADS, GLA_DK, GLA_DV), lambda i: (0, 0, 0)),
        ),
        scratch_shapes=[
            pltpu.VMEM((GLA_HEADS, GLA_DV, GLA_DK), F32),
            pltpu.VMEM((tb, GLA_HEADS * GLA_DK), BF16),
            pltpu.VMEM((tb, GLA_HEADS * GLA_DK), BF16),
            pltpu.VMEM((tb, GLA_HEADS * GLA_DK), BF16),
            pltpu.VMEM((chunk // sub, tb, GLA_HEADS * GLA_DK), BF16),
            pltpu.VMEM((tb // chunk, GLA_HEADS * GLA_DK), F32),
            pltpu.VMEM((3, chunk, chunk), BF16),
            pltpu.VMEM((3, chunk, GLA_DV), F32),
        ],
        compiler_params=_cparams(("arbitrary",)),
        name="gla_prompt",
    )(gla_in, gla_in, gla_in, log_a, norm_g)


def _gla_sample_kernel(qk_ref, v_ref, gg_ref, la_ref, g_ref, s0_ref, y_ref, s_ref, st_ref, *, steps):
    group = s0_ref.shape[0]
    c = 2 * steps
    tri = _tri(c)
    half = GLA_HEADS * GLA_DK
    pad = lambda a: jnp.concatenate([a, jnp.zeros_like(a)], axis=0)
    for n in range(group):
        rows = slice(n * steps, (n + 1) * steps)
        for h in range(GLA_HEADS):
            idx = n * GLA_HEADS + h
            kc = slice(GLA_DK * h, GLA_DK * (h + 1))
            vc = slice(GLA_DV * h, GLA_DV * (h + 1))
            st_ref[idx] = s0_ref[n, h].T
            q = pad(qk_ref[rows, kc])
            k = pad(qk_ref[rows, half + GLA_DK * h:half + GLA_DK * (h + 1)])
            v = pad(v_ref[rows, vc]).astype(BF16)
            o = _gla_chunk(q, k, v, pad(la_ref[rows, kc]), st_ref, idx, tri, c)
            y_ref[rows, vc] = _gla_out(o[:steps], g_ref[:, vc], gg_ref[rows, vc])
            s_ref[n, h] = st_ref[idx].T


def _gla_sample(gla_in, log_a, norm_g, state, *, steps, group):
    t = gla_in.shape[0]
    rows = steps * group
    return pl.pallas_call(
        functools.partial(_gla_sample_kernel, steps=steps),
        out_shape=(
            jax.ShapeDtypeStruct((t, GLA_WIDTH), F32),
            jax.ShapeDtypeStruct(state.shape, F32),
        ),
        grid=(t // rows,),
        in_specs=[
            pl.BlockSpec((rows, 1024), lambda n: (n, 0)),
            pl.BlockSpec((rows, 1024), lambda n: (n, 1)),
            pl.BlockSpec((rows, 1024), lambda n: (n, 2)),
            pl.BlockSpec((rows, GLA_HEADS * GLA_DK), lambda n: (n, 0)),
            pl.BlockSpec((1, GLA_WIDTH), lambda n: (0, 0)),
            pl.BlockSpec((group, GLA_HEADS, GLA_DK, GLA_DV), lambda n: (n, 0, 0, 0)),
        ],
        out_specs=(
            pl.BlockSpec((rows, GLA_WIDTH), lambda n: (n, 0)),
            pl.BlockSpec((group, GLA_HEADS, GLA_DK, GLA_DV), lambda n: (n, 0, 0, 0)),
        ),
        scratch_shapes=[pltpu.VMEM((group * GLA_HEADS, GLA_DV, GLA_DK), F32)],
        compiler_params=_cparams(("arbitrary",)),
        name="gla_sample",
    )(gla_in, gla_in, gla_in, log_a, norm_g, state)


def _multiplicity(delta):
    total = jnp.zeros(delta.shape, F32)
    for d in DILATIONS:
        assert d & (d - 1) == 0
        hit = jnp.where(jnp.bitwise_and(delta, d - 1) == 0, 1.0, 0.0)
        total = total + jnp.where(delta <= BAND * d, hit, 0.0)
    return jnp.where(delta >= 0, total, 0.0)


def _sample_attn_kernel(q_ref, kn_ref, vn_ref, ckt_ref, cvt_ref, o_ref):
    steps = q_ref.shape[0]
    rows = ATT_HEADS * steps
    cache_len = ckt_ref.shape[2]
    qt = jnp.concatenate([q_ref[...]] * ATT_HEADS, axis=0)
    rh = lax.broadcasted_iota(jnp.int32, (rows, ATT_WIDTH), 0) // steps
    ch = lax.broadcasted_iota(jnp.int32, (rows, ATT_WIDTH), 1) // ATT_HEAD_DIM
    own = rh == ch
    qb = jnp.where(own, qt, 0.0).astype(BF16)
    pad = jnp.zeros((LANES - steps, ATT_WIDTH), F32)
    kn = jnp.concatenate([kn_ref[...], pad], axis=0).astype(BF16)
    vn = jnp.concatenate([vn_ref[...], pad], axis=0).astype(BF16)
    assert steps & (steps - 1) == 0
    t_c = jnp.bitwise_and(lax.broadcasted_iota(jnp.int32, (rows, cache_len), 0), steps - 1)
    mc = _multiplicity(cache_len + t_c - lax.broadcasted_iota(jnp.int32, (rows, cache_len), 1))
    t_n = jnp.bitwise_and(lax.broadcasted_iota(jnp.int32, (rows, LANES), 0), steps - 1)
    new = lax.broadcasted_iota(jnp.int32, (rows, LANES), 1)
    mn = jnp.where(new < steps, _multiplicity(t_n - new), 0.0)
    sc = jnp.where(mc > 0.0, _dot(qb, ckt_ref[0].astype(BF16)), NEG)
    sn = jnp.where(mn > 0.0, _dot_nt(qb, kn), NEG)
    m = jnp.maximum(jnp.max(sc, axis=-1, keepdims=True), jnp.max(sn, axis=-1, keepdims=True))
    pc = mc * jnp.exp(sc - m)
    pn = mn * jnp.exp(sn - m)
    den = jnp.sum(pc, axis=-1, keepdims=True) + jnp.sum(pn, axis=-1, keepdims=True)
    o = _dot_nt(pc.astype(BF16), cvt_ref[0].astype(BF16)) + _dot(pn.astype(BF16), vn)
    o = jnp.where(own, o / den, 0.0)
    o_ref[...] = jnp.sum(o.reshape(ATT_HEADS, steps, ATT_WIDTH), axis=0)


def _sample_attention(q, k_new, v_new, cache_kt, cache_vt, *, steps):
    nb, _, cache_len = cache_kt.shape
    tok = pl.BlockSpec((steps, ATT_WIDTH), lambda n: (n, 0))
    cache = pl.BlockSpec((1, ATT_WIDTH, cache_len), lambda n: (n, 0, 0))
    return pl.pallas_call(
        _sample_attn_kernel,
        out_shape=jax.ShapeDtypeStruct((nb * steps, ATT_WIDTH), F32),
        grid=(nb,),
        in_specs=[tok, tok, tok, cache, cache],
        out_specs=tok,
        compiler_params=_cparams(("arbitrary",)),
        name="sample_attn",
    )(q, k_new, v_new, cache_kt, cache_vt)


def _outproj_kernel(att_ref, gla_ref, x_ref, gate_ref, ng_ref, w_ref, g2_ref, sc2_ref, sh2_ref,
                    h_ref, hn_ref):
    att = jnp.concatenate([att_ref[hp].astype(F32) for hp in range(HEAD_PAIRS)], axis=1)
    rs = lax.rsqrt(jnp.mean(att * att, axis=-1, keepdims=True) + EPS)
    att_y = (att * rs * ng_ref[...]).astype(BF16)
    mix = _dot(att_y, w_ref[:ATT_WIDTH, :]) + _dot(gla_ref[...], w_ref[ATT_WIDTH:, :])
    rows = x_ref.shape[0]
    h = x_ref[...] + _mod_rows(gate_ref, rows) * mix
    h_ref[...] = h
    hn_ref[...] = _norm_mod(h, g2_ref[...], _mod_rows(sc2_ref, rows),
                            _mod_rows(sh2_ref, rows)).astype(BF16)


def _outproj(att, gla_y, x, gate, att_norm_g, w_out, g2, sc2, sh2, *, tm):
    t = x.shape[0]
    mod_rows = gate.shape[0]
    mod_map = (lambda i: (0, 0)) if mod_rows == 1 else (lambda i: (i, 0))
    mod = pl.BlockSpec((1 if mod_rows == 1 else mod_rows * tm // t, D_MODEL), mod_map)
    row = pl.BlockSpec((tm, D_MODEL), lambda i: (i, 0))
    return pl.pallas_call(
        _outproj_kernel,
        out_shape=(jax.ShapeDtypeStruct((t, D_MODEL), F32),
                   jax.ShapeDtypeStruct((t, D_MODEL), BF16)),
        grid=(t // tm,),
        in_specs=[
            pl.BlockSpec((HEAD_PAIRS, tm, LANES), lambda i: (0, i, 0)),
            pl.BlockSpec((tm, GLA_WIDTH), lambda i: (i, 0)),
            row, mod,
            pl.BlockSpec((1, ATT_WIDTH), lambda i: (0, 0)),
            pl.BlockSpec((ATT_WIDTH + GLA_WIDTH, D_MODEL), lambda i: (0, 0)),
            pl.BlockSpec((1, D_MODEL), lambda i: (0, 0)),
            mod, mod,
        ],
        out_specs=(row, row),
        compiler_params=_cparams(("arbitrary",)),
        name="outproj",
    )(att, gla_y, x, gate, att_norm_g, w_out, g2, sc2, sh2)


def _ffn_kernel(h_hbm, hn_ref, gate_ref, fg_ref, wu_ref, wd_ref, y_ref, h_ref, h_sem):
    i = pl.program_id(0)
    f = pl.program_id(1)
    last = pl.num_programs(1) - 1
    tm = hn_ref.shape[0]

    def residual_copy():
        rows = pl.ds(pl.multiple_of(i * tm, tm), tm)
        return pltpu.make_async_copy(h_hbm.at[rows, :], h_ref, h_sem)

    def hidden():
        u = jnp.maximum(_dot(hn_ref[...], wu_ref[...]), 0.0)
        return (u * u).astype(BF16)

    @pl.when(f == 0)
    def _():
        residual_copy().start()
        y_ref[...] = _dot(hidden(), wd_ref[...])

    @pl.when(jnp.logical_and(f > 0, f < last))
    def _():
        y_ref[...] += _dot(hidden(), wd_ref[...])

    @pl.when(f == last)
    def _():
        u2 = hidden()
        residual_copy().wait()
        width = D_MODEL // 4
        ss = jnp.zeros((h_ref.shape[0], 1), F32)
        gate = _mod_rows(gate_ref, h_ref.shape[0])
        for c in range(0, D_MODEL, width):
            cols = slice(c, c + width)
            y = y_ref[:, cols] + _dot(u2, wd_ref[:, cols])
            h2 = h_ref[:, cols] + gate[:, cols] * y
            y_ref[:, cols] = h2
            ss = ss + jnp.sum(h2 * h2, axis=-1, keepdims=True)
        y_ref[...] = y_ref[...] * lax.rsqrt(ss / D_MODEL + EPS) * fg_ref[...]


def _ffn(h, hn, gate, final_g, w_up, w_down, *, tm, tf):
    t = h.shape[0]
    assert D_FF // tf >= 2, "first and last hidden chunks are separate code paths"
    row = pl.BlockSpec((tm, D_MODEL), lambda i, f: (i, 0))
    return pl.pallas_call(
        _ffn_kernel,
        out_shape=jax.ShapeDtypeStruct((t, D_MODEL), F32),
        grid=(t // tm, D_FF // tf),
        in_specs=[
            pl.BlockSpec(memory_space=pl.ANY),
            row,
            _mod_spec(gate.shape[0], tm, t),
            pl.BlockSpec((1, D_MODEL), lambda i, f: (0, 0)),
            pl.BlockSpec((D_MODEL, tf), lambda i, f: (0, f)),
            pl.BlockSpec((tf, D_MODEL), lambda i, f: (f, 0)),
        ],
        out_specs=row,
        scratch_shapes=[pltpu.VMEM((tm, D_MODEL), F32), pltpu.SemaphoreType.DMA(())],
        compiler_params=_cparams(("arbitrary", "arbitrary")),
        name="ffn",
    )(h, hn, gate, final_g, w_up, w_down)


def _split_ada(ada):
    return [ada[:, D_MODEL * i:D_MODEL * (i + 1)] for i in range(6)]


def kernel(x_prompt, x_sample, c_prompt, c_sample, cache_win_k, cache_win_v, state_gla, w_in, gla_gate_w, gla_gate_b, att_norm_g, gla_norm_g, w_out, norm1_g, norm2_g, w_ada, b_ada, w_up, w_down, final_g):
    assert w_in.shape[0] == 1, "single-layer model"
    n_p, t_p, _ = x_prompt.shape
    n_s, t_s, _ = x_sample.shape
    assert n_p == 1
    rows_s = n_s * t_s
    wb_p = min(MAX_WINDOW, t_p)

    w_main = jnp.transpose(w_in[0]).astype(BF16)
    gate_w = gla_gate_w[0].astype(BF16)
    gate_b = gla_gate_b[0][None, :]
    n1g, n2g = norm1_g[0][None, :], norm2_g[0][None, :]
    ang, gng = att_norm_g[0][None, :], gla_norm_g[0][None, :]
    fg = final_g[None, :]

    c_rows = n_p + n_s
    c_pad = -c_rows % 8
    c_all = jnp.concatenate([c_prompt, c_sample, jnp.zeros((c_pad, D_MODEL), F32)], axis=0)
    ada = _ada(c_all, w_ada[0], b_ada[0][None, :])
    sh1_p, sc1_p, g1_p, sh2_p, sc2_p, g2_p = _split_ada(ada[:n_p])
    sh1_s, sc1_s, g1_s, sh2_s, sc2_s, g2_s = _split_ada(ada[n_p:c_rows])

    xp = x_prompt[0]
    cs_p, sn_p = _rope_tables(jnp.arange(t_p))
    outs = _inproj(xp, n1g, sc1_p, sh1_p, w_main, gate_w, gate_b, cs_p, sn_p,
                   tm=1024, emit_att=True, tail_rows=wb_p)
    att, w_out_b, w_up_b, w_down_b = _prompt_attention(outs[:3], (w_out[0], w_up[0], w_down[0]))
    gla_in, log_a = outs[3], outs[4]
    win_k, win_v = outs[5][0], outs[5][1]
    gla_y, st_p = _gla_prompt(gla_in, log_a, gng, tb=512, chunk=64, sub=16)
    h_p, hn_p = _outproj(att, gla_y, xp, g1_p, ang, w_out_b, n2g, sc2_p, sh2_p, tm=512)
    y_p = _ffn(h_p, hn_p, g2_p, fg, w_up_b, w_down_b, tm=1024, tf=1024)

    xs = x_sample.reshape(rows_s, D_MODEL)
    cs_s, sn_s = _rope_tables(jnp.tile(PAST_LEN + jnp.arange(t_s), n_s))
    gla_in_s, log_a_s = _inproj(xs, n1g, sc1_s, sh1_s, w_main, gate_w, gate_b, cs_s, sn_s,
                                tm=rows_s, emit_att=False)
    qkv_s = _qkv_rows(xs, n1g, sc1_s, sh1_s, w_main, cs_s, sn_s,
                      tm=rows_s, first_row=0, rows=rows_s, col0=0)
    new_k, new_v = qkv_s[1], qkv_s[2]
    wc = cache_win_k.shape[2]
    cache_t = lambda c: jnp.transpose(c[0], (0, 2, 3, 1)).reshape(n_s, ATT_WIDTH, wc)
    att_tok = _sample_attention(qkv_s[0], new_k, new_v, cache_t(cache_win_k), cache_t(cache_win_v),
                                steps=t_s)
    att_s = jnp.transpose(att_tok.reshape(rows_s, HEAD_PAIRS, LANES), (1, 0, 2)).astype(BF16)
    gla_y_s, st_s = _gla_sample(gla_in_s.astype(F32), log_a_s, gng, state_gla[0], steps=t_s,
                                group=4 if n_s % 4 == 0 else 1)
    h_s, hn_s = _outproj(att_s, gla_y_s.astype(BF16), xs, g1_s, ang, w_out_b, n2g, sc2_s, sh2_s,
                         tm=rows_s)
    y_s = _ffn(h_s, hn_s, g2_s, fg, w_up_b, w_down_b, tm=rows_s, tf=2048)

    return (
        y_p[None],
        y_s.reshape(n_s, t_s, D_MODEL),
        win_k.reshape(1, n_p, wb_p, ATT_HEADS, ATT_HEAD_DIM),
        win_v.reshape(1, n_p, wb_p, ATT_HEADS, ATT_HEAD_DIM),
        st_p[None, None],
        new_k.reshape(1, n_s, t_s, ATT_HEADS, ATT_HEAD_DIM),
        new_v.reshape(1, n_s, t_s, ATT_HEADS, ATT_HEAD_DIM),
        st_s[None],
    )
```
